```python
import math
import jax, jax.numpy as jnp
from jax import lax
import numpy as np

D_MODEL = 2048
BATCH = 4
SEQ = 2048
DEPTH = 1
DEC_BATCH = 128
DEC_SEQ = 8
PAST_LEN = 16384
PAGE_SIZE = 128

H_RET = 4
DK_RET = 128
DV_RET = 256
H_GLA = 4
DK_GLA = 128
DV_GLA = 256
GLA_RANK = 16
GLA_TEMP = 16.0
D_MIX = H_RET * DV_RET + H_GLA * DV_GLA
CHUNK = 64
ROPE_BASE = 10000.0
PEER_HEADS = 8
PEER_NKEYS = 128
PEER_DQ = 256
PEER_TOPK = 16
PEER_BLOCK = 128
N_EXPERTS = PEER_NKEYS * PEER_NKEYS
EPS = 1e-6

IN_SPLITS = (H_RET * DK_RET, H_RET * DK_RET, H_RET * DV_RET, H_RET * DV_RET,
             H_GLA * DK_GLA, H_GLA * DK_GLA, H_GLA * DV_GLA, H_GLA * DV_GLA, GLA_RANK)
D_IN = sum(IN_SPLITS)
IN_CUTS = tuple(int(c) for c in np.cumsum(IN_SPLITS)[:-1])

kernel_name = "hymba_retnet_gla_peer_step"


def rmsnorm(x, w):
    xf = x.astype(jnp.float32)
    y = xf * lax.rsqrt(jnp.mean(xf * xf, axis=-1, keepdims=True) + EPS)
    return (y * w.astype(jnp.float32)).astype(x.dtype)


def group_norm_heads(o, w, b):
    mu = jnp.mean(o, axis=-1, keepdims=True)
    var = jnp.mean(jnp.square(o - mu), axis=-1, keepdims=True)
    return (o - mu) * lax.rsqrt(var + EPS) * w.astype(jnp.float32) + b.astype(jnp.float32)


def rms_norm_heads(o, w):
    return o * lax.rsqrt(jnp.mean(o * o, axis=-1, keepdims=True) + EPS) * w.astype(jnp.float32)


def rotary(x, pos):
    d = x.shape[-1]
    freqs = ROPE_BASE ** (-jnp.arange(0, d, 2, dtype=jnp.float32) / d)
    ang = pos[:, None] * freqs[None, :]
    cos = jnp.cos(ang)[None, :, None, :]
    sin = jnp.sin(ang)[None, :, None, :]
    x1, x2 = x[..., : d // 2], x[..., d // 2:]
    return jnp.concatenate([x1 * cos - x2 * sin, x1 * sin + x2 * cos], axis=-1)


def to_chunks(t, c):
    b, l, h, d = t.shape
    return t.reshape(b, l // c, c, h, d).transpose(1, 0, 3, 2, 4)


def from_chunks(t):
    n, b, h, c, d = t.shape
    return t.transpose(1, 0, 3, 2, 4).reshape(b, n * c, h, d)


def retention_chunkwise(q, k, v, s0):
    L = q.shape[1]
    c = math.gcd(L, CHUNK)
    log_g = jnp.log1p(-jnp.exp2(-5.0 - jnp.arange(H_RET, dtype=jnp.float32)))
    idx = jnp.arange(c, dtype=jnp.float32)
    dist = idx[:, None] - idx[None, :]
    causal = dist >= 0
    decay_mask = jnp.where(causal[None], jnp.exp(jnp.maximum(dist, 0.0)[None] * log_g[:, None, None]), 0.0)
    q_decay = jnp.exp((idx + 1.0)[None, :] * log_g[:, None])
    k_decay = jnp.exp((c - 1.0 - idx)[None, :] * log_g[:, None])
    chunk_decay = jnp.exp(c * log_g)

    def step(S, inp):
        qc, kc, vc = inp
        scores = jnp.einsum('bhid,bhjd->bhij', qc, kc) * decay_mask[None]
        inner = jnp.einsum('bhij,bhjv->bhiv', scores, vc)
        cross = jnp.einsum('bhid,bhdv->bhiv', qc, S) * q_decay[None, :, :, None]
        S_new = S * chunk_decay[None, :, None, None] + jnp.einsum(
            'bhjd,bhjv->bhdv', kc * k_decay[None, :, :, None], vc)
        return S_new, inner + cross

    S_fin, o = lax.scan(step, s0, (to_chunks(q, c), to_chunks(k, c), to_chunks(v, c)))
    return from_chunks(o), S_fin


def gla_chunked(q, k, v, log_a, s0):
    L = q.shape[1]
    c = math.gcd(L, CHUNK)
    causal = jnp.tril(jnp.ones((c, c), dtype=bool))

    def step(S, inp):
        qc, kc, vc, lac = inp
        bcum = jnp.cumsum(lac, axis=2)
        diff = bcum[:, :, :, None, :] - bcum[:, :, None, :, :]
        m = causal[None, None, :, :, None]
        decay = jnp.where(m, jnp.exp(jnp.where(m, diff, 0.0)), 0.0)
        A = jnp.sum(qc[:, :, :, None, :] * kc[:, :, None, :, :] * decay, axis=-1)
        inner = jnp.einsum('bhij,bhjv->bhiv', A, vc)
        cross = jnp.einsum('bhid,bhdv->bhiv', qc * jnp.exp(bcum), S)
        b_last = bcum[:, :, -1:, :]
        S_new = S * jnp.exp(b_last[:, :, 0, :])[..., None] + jnp.einsum(
            'bhjd,bhjv->bhdv', kc * jnp.exp(b_last - bcum), vc)
        return S_new, inner + cross

    S_fin, o = lax.scan(step, s0, (to_chunks(q, c), to_chunks(k, c), to_chunks(v, c), to_chunks(log_a, c)))
    return from_chunks(o), S_fin


def mixer_block(x, pos, s_ret0, s_gla0, w_norm_mix, w_in, w_gla_a2, b_gla_a,
                gn_ret_w, gn_ret_b, gn_gla_w, w_out):
    B, L, _ = x.shape
    h = rmsnorm(x, w_norm_mix)
    proj = jnp.einsum('bld,de->ble', h, w_in).astype(jnp.float32)
    q_r, k_r, v_r, g_r, q_g, k_g, v_g, g_g, a_lr = jnp.split(proj, IN_CUTS, axis=-1)
    q_r = rotary(q_r.reshape(B, L, H_RET, DK_RET), pos)
    k_r = rotary(k_r.reshape(B, L, H_RET, DK_RET), pos) * (DK_RET ** -0.5)
    o_r, s_ret = retention_chunkwise(q_r, k_r, v_r.reshape(B, L, H_RET, DV_RET), s_ret0.astype(jnp.float32))
    o_r = group_norm_heads(o_r, gn_ret_w, gn_ret_b).reshape(B, L, H_RET * DV_RET) * jax.nn.silu(g_r)
    log_a = jax.nn.log_sigmoid(jnp.einsum('blr,re->ble', a_lr, w_gla_a2.astype(jnp.float32))
                               + b_gla_a.astype(jnp.float32)) / GLA_TEMP
    q_g = q_g.reshape(B, L, H_GLA, DK_GLA) * (DK_GLA ** -0.5)
    o_g, s_gla = gla_chunked(q_g, k_g.reshape(B, L, H_GLA, DK_GLA), v_g.reshape(B, L, H_GLA, DV_GLA),
                             log_a.reshape(B, L, H_GLA, DK_GLA), s_gla0.astype(jnp.float32))
    o_g = rms_norm_heads(o_g, gn_gla_w).reshape(B, L, H_GLA * DV_GLA) * jax.nn.silu(g_g)
    mix = jnp.concatenate([o_r, o_g], axis=-1).astype(x.dtype)
    return x + jnp.einsum('ble,ed->bld', mix, w_out), s_ret, s_gla


def peer_block(x, w_norm_ffn, w_pq, sub_keys1, sub_keys2, u_tab, v_tab):
    B, L, D = x.shape
    h = rmsnorm(x, w_norm_ffn).reshape(B * L, D)
    T = B * L
    Tp = -(-T // PEER_BLOCK) * PEER_BLOCK
    hb = jnp.pad(h, ((0, Tp - T), (0, 0))).reshape(Tp // PEER_BLOCK, PEER_BLOCK, D)

    def one_block(xb):
        q = jnp.einsum('td,de->te', xb, w_pq).astype(jnp.float32).reshape(
            PEER_BLOCK, PEER_HEADS, 2, PEER_DQ // 2)
        s1 = jnp.einsum('thd,hnd->thn', q[:, :, 0], sub_keys1.astype(jnp.float32))
        s2 = jnp.einsum('thd,hnd->thn', q[:, :, 1], sub_keys2.astype(jnp.float32))
        v1, i1 = lax.top_k(s1, PEER_TOPK)
        v2, i2 = lax.top_k(s2, PEER_TOPK)
        cand_s = (v1[..., :, None] + v2[..., None, :]).reshape(PEER_BLOCK, PEER_HEADS, PEER_TOPK * PEER_TOPK)
        cand_i = (i1[..., :, None] * PEER_NKEYS + i2[..., None, :]).reshape(PEER_BLOCK, PEER_HEADS, PEER_TOPK * PEER_TOPK)
        top_s, sel = lax.top_k(cand_s, PEER_TOPK)
        eidx = jnp.take_along_axis(cand_i, sel, axis=-1)
        gate = jax.nn.softmax(top_s, axis=-1)
        u = u_tab[eidx]
        act = jax.nn.gelu(jnp.einsum('thkd,td->thk', u, xb).astype(jnp.float32), approximate=False)
        return jnp.einsum('thk,thkd->td', (gate * act).astype(xb.dtype), v_tab[eidx])

    out = lax.map(one_block, hb).reshape(Tp, D)[:T].reshape(B, L, D)
    return x + out.astype(x.dtype)


def setup_inputs(seed: int = 0) -> dict:
    key = jax.random.key(seed)
    ks = jax.random.split(key, 20)
    n = jax.random.normal
    f32 = jnp.float32
    return {
        "x_prompt": n(ks[0], (BATCH, SEQ, D_MODEL), f32),
        "x_sample": n(ks[1], (DEC_BATCH, DEC_SEQ, D_MODEL), f32),
        "state_ret": 0.5 * n(ks[2], (DEPTH, DEC_BATCH, H_RET, DK_RET, DV_RET), f32),
        "state_gla": 0.5 * n(ks[3], (DEPTH, DEC_BATCH, H_GLA, DK_GLA, DV_GLA), f32),
        "w_norm_mix": 1.0 + 0.01 * n(ks[4], (DEPTH, D_MODEL), f32),
        "w_in": n(ks[5], (DEPTH, D_MODEL, D_IN), f32) * D_MODEL ** -0.5,
        "w_gla_a2": n(ks[6], (DEPTH, GLA_RANK, H_GLA * DK_GLA), f32) * GLA_RANK ** -0.5,
        "b_gla_a": 0.1 * n(ks[7], (DEPTH, H_GLA * DK_GLA), f32),
        "gn_ret_w": 1.0 + 0.01 * n(ks[8], (DEPTH, H_RET, DV_RET), f32),
        "gn_ret_b": 0.01 * n(ks[9], (DEPTH, H_RET, DV_RET), f32),
        "gn_gla_w": 1.0 + 0.01 * n(ks[10], (DEPTH, H_GLA, DV_GLA), f32),
        "w_out": n(ks[11], (DEPTH, D_MIX, D_MODEL), f32) * D_MIX ** -0.5,
        "w_norm_ffn": 1.0 + 0.01 * n(ks[12], (DEPTH, D_MODEL), f32),
        "w_pq": n(ks[13], (DEPTH, D_MODEL, PEER_HEADS * PEER_DQ), f32) * D_MODEL ** -0.5,
        "sub_keys1": n(ks[14], (DEPTH, PEER_HEADS, PEER_NKEYS, PEER_DQ // 2), f32) * (PEER_DQ // 2) ** -0.5,
        "sub_keys2": n(ks[15], (DEPTH, PEER_HEADS, PEER_NKEYS, PEER_DQ // 2), f32) * (PEER_DQ // 2) ** -0.5,
        "u_tab": n(ks[16], (DEPTH, N_EXPERTS, D_MODEL), f32) * D_MODEL ** -0.5,
        "v_tab": 0.5 * n(ks[17], (DEPTH, N_EXPERTS, D_MODEL), f32),
        "w_norm_final": 1.0 + 0.01 * n(ks[18], (D_MODEL,), f32),
    }


def reference(x_prompt, x_sample, state_ret, state_gla, w_norm_mix, w_in, w_gla_a2, b_gla_a,
              gn_ret_w, gn_ret_b, gn_gla_w, w_out, w_norm_ffn, w_pq, sub_keys1, sub_keys2,
              u_tab, v_tab, w_norm_final):
    Bp, Lp, _ = x_prompt.shape
    Bs, Ls, _ = x_sample.shape
    pos_p = jnp.arange(Lp, dtype=jnp.float32)
    pos_s = PAST_LEN + jnp.arange(Ls, dtype=jnp.float32)
    hp, hs = x_prompt, x_sample
    rp_list, gp_list, rs_list, gs_list = [], [], [], []
    for l in range(DEPTH):
        mix_w = (w_norm_mix[l], w_in[l], w_gla_a2[l], b_gla_a[l], gn_ret_w[l], gn_ret_b[l], gn_gla_w[l], w_out[l])
        zero_r = jnp.zeros((Bp, H_RET, DK_RET, DV_RET), jnp.float32)
        zero_g = jnp.zeros((Bp, H_GLA, DK_GLA, DV_GLA), jnp.float32)
        hp, rp, gp = mixer_block(hp, pos_p, zero_r, zero_g, *mix_w)
        hs, rs, gs = mixer_block(hs, pos_s, state_ret[l], state_gla[l], *mix_w)
        ffn_w = (w_norm_ffn[l], w_pq[l], sub_keys1[l], sub_keys2[l], u_tab[l], v_tab[l])
        hp = peer_block(hp, *ffn_w)
        hs = peer_block(hs, *ffn_w)
        rp_list.append(rp)
        gp_list.append(gp)
        rs_list.append(rs)
        gs_list.append(gs)
    y_prompt = rmsnorm(hp, w_norm_final)
    y_sample = rmsnorm(hs, w_norm_final)
    return (y_prompt, y_sample, jnp.stack(rp_list), jnp.stack(gp_list), jnp.stack(rs_list), jnp.stack(gs_list))
```

```python
import functools
import math

import numpy as np
import jax
import jax.numpy as jnp
from jax import lax
from jax.experimental import pallas as pl
from jax.experimental.pallas import tpu as pltpu

F32 = jnp.float32
BF16 = jnp.bfloat16

D_MODEL = 2048
H_RET, DK_RET, DV_RET = 4, 128, 256
H_GLA, DK_GLA, DV_GLA = 4, 128, 256
GLA_RANK = 16
GLA_TEMP = 16.0
ROPE_BASE = 10000.0
PAST_LEN = 16384
PEER_HEADS, PEER_NKEYS, PEER_DQ, PEER_TOPK = 8, 128, 256, 16
N_EXPERTS = PEER_NKEYS * PEER_NKEYS
EPS = 1e-6

LANES = 128
D_MIX = H_RET * DV_RET + H_GLA * DV_GLA
D_IN_MAIN = 2 * H_RET * DK_RET + 2 * H_RET * DV_RET + 2 * H_GLA * DK_GLA + 2 * H_GLA * DV_GLA
D_IN_PAD = 6400
OFF_QR = 0
OFF_KR = OFF_QR + H_RET * DK_RET
OFF_VR = OFF_KR + H_RET * DK_RET
OFF_GR = OFF_VR + H_RET * DV_RET
OFF_QG = OFF_GR + H_RET * DV_RET
OFF_KG = OFF_QG + H_GLA * DK_GLA
OFF_VG = OFF_KG + H_GLA * DK_GLA
OFF_GG = OFF_VG + H_GLA * DV_GLA
OFF_A = OFF_GG + H_GLA * DV_GLA

PROMPT_CHUNK = 128
SAMPLE_GROUP = 8
VMEM_LIMIT = 56 * 1024 * 1024


def _cparams(sem):
    return pltpu.CompilerParams(dimension_semantics=sem, vmem_limit_bytes=VMEM_LIMIT)


def _norm_matmul_kernel(x_ref, wn_ref, w_ref, o_ref, h_ref):
    @pl.when(pl.program_id(1) == 0)
    def _():
        x = x_ref[...]
        ms = jnp.mean(x * x, axis=-1, keepdims=True)
        h_ref[...] = (x * lax.rsqrt(ms + EPS) * wn_ref[...]).astype(BF16)

    o_ref[...] = jnp.dot(h_ref[...], w_ref[...], preferred_element_type=F32)


def _norm_matmul(x, wn, w, tm, tn):
    T, D = x.shape
    N = w.shape[1]
    return pl.pallas_call(
        _norm_matmul_kernel,
        grid=(T // tm, N // tn),
        in_specs=[
            pl.BlockSpec((tm, D), lambda i, j: (i, 0)),
            pl.BlockSpec((1, D), lambda i, j: (0, 0)),
            pl.BlockSpec((D, tn), lambda i, j: (0, j)),
        ],
        out_specs=pl.BlockSpec((tm, tn), lambda i, j: (i, j)),
        out_shape=jax.ShapeDtypeStruct((T, N), F32),
        scratch_shapes=[pltpu.VMEM((tm, D), BF16)],
        compiler_params=_cparams(("parallel", "arbitrary")),
        name="norm_in_proj",
    )(x, wn, w)


def _gla_consts(rows, blk):
    idx = np.arange(rows)
    seq = idx // blk
    t = idx[None, :]
    i = idx[:, None]
    same = seq[:, None] == seq[None, :]
    mats = [same & (t <= i), same & (t > i)]
    pmasks = []
    s = blk
    while s >= 2:
        bstart = (idx // s) * s
        mid = bstart + s // 2
        upper = idx >= mid
        mats.append(upper[:, None] & (t >= mid[:, None]) & (t <= i))
        mats.append((~upper)[:, None] & (t > i) & (t <= mid[:, None] - 1))
        pmasks.append((bstart[:, None] == bstart[None, :]) & upper[:, None] & (~upper)[None, :])
        s //= 2
    pmasks.append(np.eye(rows, dtype=bool))
    return (np.concatenate(mats, 0).astype(np.float32), np.stack(pmasks).astype(np.float32))


def _ret_consts(rows, blk):
    log_g = jnp.log1p(-jnp.exp2(-5.0 - jnp.arange(H_RET, dtype=F32)))
    idx = np.arange(rows)
    p = jnp.asarray(idx % blk, F32)
    same = jnp.asarray((idx[:, None] // blk) == (idx[None, :] // blk))
    dist = p[:, None] - p[None, :]
    causal = same & (dist >= 0)
    dmask = jnp.where(causal[None], jnp.exp(jnp.maximum(dist, 0.0)[None] * log_g[:, None, None]), 0.0)
    qdec = jnp.exp((p + 1.0)[None, :] * log_g[:, None])
    kdec = jnp.exp((blk - 1.0 - p)[None, :] * log_g[:, None])
    cdec = jnp.exp(blk * log_g)
    qdec_full = jnp.broadcast_to(qdec[:, :, None], (H_RET, rows, DV_RET))
    kdec_full = jnp.broadcast_to(kdec[:, :, None], (H_RET, rows, DK_RET))
    cdec_full = jnp.broadcast_to(cdec[:, None, None], (H_RET, DK_RET, DV_RET))
    return dmask.astype(F32), qdec_full.astype(F32), kdec_full.astype(F32), cdec_full.astype(F32)


def _rope_tables(pos):
    d = DK_RET
    freqs = ROPE_BASE ** (-jnp.arange(0, d, 2, dtype=F32) / d)
    ang = pos[:, None] * freqs[None, :]
    cos = jnp.cos(ang)
    sin = jnp.sin(ang)
    return jnp.concatenate([cos, cos], axis=-1), jnp.concatenate([-sin, sin], axis=-1)


def _dot_nt(a, b):
    return lax.dot_general(a, b, (((1,), (1,)), ((), ())), preferred_element_type=F32)


def _rotary(x, cosf, sinf):
    return x * cosf + pltpu.roll(x, DK_RET // 2, axis=1) * sinf


def _sigmoid(x):
    return 1.0 / (1.0 + jnp.exp(-x))


def _ret_head(p_ref, h, cosf, sinf, dmask_ref, qdec_ref, kdec_ref):
    q = _rotary(p_ref[:, OFF_QR + h * DK_RET:OFF_QR + (h + 1) * DK_RET], cosf, sinf)
    k = _rotary(p_ref[:, OFF_KR + h * DK_RET:OFF_KR + (h + 1) * DK_RET], cosf, sinf) * (DK_RET ** -0.5)
    v = p_ref[:, OFF_VR + h * DV_RET:OFF_VR + (h + 1) * DV_RET].astype(BF16)
    qb = q.astype(BF16)
    scores = _dot_nt(qb, k.astype(BF16)) * dmask_ref[h]
    inner = jnp.dot(scores.astype(BF16), v, preferred_element_type=F32)
    kt = (k * kdec_ref[h]).T
    return qb, inner, kt, v


def _ret_finish(o, p_ref, h, gnw_ref, gnb_ref):
    mu = jnp.mean(o, axis=-1, keepdims=True)
    d = o - mu
    var = jnp.mean(d * d, axis=-1, keepdims=True)
    sl = slice(h * DV_RET, (h + 1) * DV_RET)
    y = d * lax.rsqrt(var + EPS) * gnw_ref[:, sl] + gnb_ref[:, sl]
    g = p_ref[:, OFF_GR + h * DV_RET:OFF_GR + (h + 1) * DV_RET]
    return y * (g * _sigmoid(g))


def _gla_log_decay(p_ref, w2_ref, b2_ref):
    a = p_ref[:, OFF_A:OFF_A + LANES].astype(BF16)
    z = jnp.dot(a, w2_ref[...], preferred_element_type=F32) + b2_ref[...]
    return (jnp.minimum(z, 0.0) - jnp.log1p(jnp.exp(-jnp.abs(z)))) * (1.0 / GLA_TEMP)


def _gla_head(p_ref, h, la, sums_ref, pm_ref, rows, n_levels):
    la_h = la[:, h * DK_GLA:(h + 1) * DK_GLA]
    hi = la_h.astype(BF16)
    lo = (la_h - hi.astype(F32)).astype(BF16)
    e2 = jnp.dot(sums_ref[...], jnp.concatenate([hi, lo], axis=1), preferred_element_type=F32)
    ex = e2[:, :DK_GLA] + e2[:, DK_GLA:]
    bcum = ex[0:rows]
    brest = ex[rows:2 * rows]
    q = p_ref[:, OFF_QG + h * DK_GLA:OFF_QG + (h + 1) * DK_GLA] * (DK_GLA ** -0.5)
    k = p_ref[:, OFF_KG + h * DK_GLA:OFF_KG + (h + 1) * DK_GLA]
    v = p_ref[:, OFF_VG + h * DV_GLA:OFF_VG + (h + 1) * DV_GLA].astype(BF16)
    amat = _dot_nt(q.astype(BF16), k.astype(BF16)) * pm_ref[n_levels]
    for l in range(n_levels):
        up = ex[(2 + 2 * l) * rows:(3 + 2 * l) * rows]
        lw = ex[(3 + 2 * l) * rows:(4 + 2 * l) * rows]
        ql = (q * jnp.exp(up)).astype(BF16)
        kl = (k * jnp.exp(lw)).astype(BF16)
        amat = amat + _dot_nt(ql, kl) * pm_ref[l]
    inner = jnp.dot(amat.astype(BF16), v, preferred_element_type=F32)
    qd = (q * jnp.exp(bcum)).astype(BF16)
    kt = (k * jnp.exp(brest)).T
    return qd, inner, kt, v, bcum


def _gla_finish(o, p_ref, h, gw_ref):
    sl = slice(h * DV_GLA, (h + 1) * DV_GLA)
    y = o * lax.rsqrt(jnp.mean(o * o, axis=-1, keepdims=True) + EPS) * gw_ref[:, sl]
    g = p_ref[:, OFF_GG + h * DV_GLA:OFF_GG + (h + 1) * DV_GLA]
    return y * (g * _sigmoid(g))


def _rows_to_cols(row, n):
    return jnp.broadcast_to(row, (n, n)).T


def _mixer_prompt_kernel(p_ref, cos_ref, sin_ref, dmask_ref, qdec_ref, kdec_ref, cdec_ref,
                         w2_ref, b2_ref, sums_ref, pm_ref, gnrw_ref, gnrb_ref, gngw_ref,
                         mix_ref, sret_ref, sgla_ref, *, rows, n_levels):
    @pl.when(pl.program_id(1) == 0)
    def _():
        sret_ref[...] = jnp.zeros_like(sret_ref)
        sgla_ref[...] = jnp.zeros_like(sgla_ref)

    cosf = cos_ref[...]
    sinf = sin_ref[...]
    for h in range(H_RET):
        qb, inner, kt, v = _ret_head(p_ref, h, cosf, sinf, dmask_ref, qdec_ref, kdec_ref)
        s_old = sret_ref[0, h]
        cross = jnp.dot(qb, s_old.astype(BF16), preferred_element_type=F32) * qdec_ref[h]
        sret_ref[0, h] = s_old * cdec_ref[h] + jnp.dot(kt.astype(BF16), v, preferred_element_type=F32)
        y = _ret_finish(inner + cross, p_ref, h, gnrw_ref, gnrb_ref)
        mix_ref[:, h * DV_RET:(h + 1) * DV_RET] = y.astype(mix_ref.dtype)

    la = _gla_log_decay(p_ref, w2_ref, b2_ref)
    for h in range(H_GLA):
        qd, inner, kt, v, bcum = _gla_head(p_ref, h, la, sums_ref, pm_ref, rows, n_levels)
        s_old = sgla_ref[0, h]
        cross = jnp.dot(qd, s_old.astype(BF16), preferred_element_type=F32)
        dec = jnp.exp(_rows_to_cols(bcum[rows - 1:rows, :], DK_GLA))
        dec = jnp.concatenate([dec, dec], axis=1)
        sgla_ref[0, h] = s_old * dec + jnp.dot(kt.astype(BF16), v, preferred_element_type=F32)
        y = _gla_finish(inner + cross, p_ref, h, gngw_ref)
        off = H_RET * DV_RET + h * DV_GLA
        mix_ref[:, off:off + DV_GLA] = y.astype(mix_ref.dtype)


def _mixer_prompt(proj, B, L, consts):
    rows = PROMPT_CHUNK
    nc = L // rows
    cosf, sinf, dmask, qdec, kdec, cdec, w2, b2, sums, pm, gnrw, gnrb, gngw = consts
    n_levels = pm.shape[0] - 1
    const2 = lambda b, c: (0, 0)
    const3 = lambda b, c: (0, 0, 0)
    kern = functools.partial(_mixer_prompt_kernel, rows=rows, n_levels=n_levels)
    return pl.pallas_call(
        kern,
        grid=(B, nc),
        in_specs=[
            pl.BlockSpec((rows, D_IN_PAD), lambda b, c: (b * nc + c, 0)),
            pl.BlockSpec((rows, DK_RET), lambda b, c: (c, 0)),
            pl.BlockSpec((rows, DK_RET), lambda b, c: (c, 0)),
            pl.BlockSpec(dmask.shape, const3),
            pl.BlockSpec(qdec.shape, const3),
            pl.BlockSpec(kdec.shape, const3),
            pl.BlockSpec(cdec.shape, const3),
            pl.BlockSpec(w2.shape, const2),
            pl.BlockSpec(b2.shape, const2),
            pl.BlockSpec(sums.shape, const2),
            pl.BlockSpec(pm.shape, const3),
            pl.BlockSpec(gnrw.shape, const2),
            pl.BlockSpec(gnrb.shape, const2),
            pl.BlockSpec(gngw.shape, const2),
        ],
        out_specs=[
            pl.BlockSpec((rows, D_MIX), lambda b, c: (b * nc + c, 0)),
            pl.BlockSpec((1, H_RET, DK_RET, DV_RET), lambda b, c: (b, 0, 0, 0)),
            pl.BlockSpec((1, H_GLA, DK_GLA, DV_GLA), lambda b, c: (b, 0, 0, 0)),
        ],
        out_shape=[
            jax.ShapeDtypeStruct((B * L, D_MIX), BF16),
            jax.ShapeDtypeStruct((B, H_RET, DK_RET, DV_RET), F32),
            jax.ShapeDtypeStruct((B, H_GLA, DK_GLA, DV_GLA), F32),
        ],
        compiler_params=_cparams(("parallel", "arbitrary")),
        name="mixer_prompt",
    )(proj, cosf, sinf, dmask, qdec, kdec, cdec, w2, b2, sums, pm, gnrw, gnrb, gngw)


def _mixer_sample_kernel(p_ref, cos_ref, sin_ref, dmask_ref, qdec_ref, kdec_ref, cdec_ref,
                         w2_ref, b2_ref, sums_ref, pm_ref, gnrw_ref, gnrb_ref, gngw_ref,
                         sret0_ref, sgla0_ref,
                         mix_ref, sret_ref, sgla_ref, *, rows, seq_len, n_levels):
    n_seq = rows // seq_len
    row_seq = lax.broadcasted_iota(jnp.int32, (rows, 1), 0) // seq_len
    col_seq = lax.broadcasted_iota(jnp.int32, (1, rows), 1) // seq_len
    cosf = cos_ref[...]
    sinf = sin_ref[...]
    for h in range(H_RET):
        qb, inner, kt, v = _ret_head(p_ref, h, cosf, sinf, dmask_ref, qdec_ref, kdec_ref)
        cross = jnp.zeros((rows, DV_RET), F32)
        for s in range(n_seq):
            s_old = sret0_ref[s, h]
            c_s = jnp.dot(qb, s_old.astype(BF16), preferred_element_type=F32)
            cross = cross + jnp.where(row_seq == s, c_s, 0.0)
            kt_s = jnp.where(col_seq == s, kt, 0.0).astype(BF16)
            sret_ref[s, h] = s_old * cdec_ref[h] + jnp.dot(kt_s, v, preferred_element_type=F32)
        y = _ret_finish(inner + cross * qdec_ref[h], p_ref, h, gnrw_ref, gnrb_ref)
        mix_ref[:, h * DV_RET:(h + 1) * DV_RET] = y.astype(mix_ref.dtype)

    la = _gla_log_decay(p_ref, w2_ref, b2_ref)
    for h in range(H_GLA):
        qd, inner, kt, v, bcum = _gla_head(p_ref, h, la, sums_ref, pm_ref, rows, n_levels)
        cross = jnp.zeros((rows, DV_GLA), F32)
        for s in range(n_seq):
            s_old = sgla0_ref[s, h]
            c_s = jnp.dot(qd, s_old.astype(BF16), preferred_element_type=F32)
            cross = cross + jnp.where(row_seq == s, c_s, 0.0)
            kt_s = jnp.where(col_seq == s, kt, 0.0).astype(BF16)
            last = (s + 1) * seq_len - 1
            dec = jnp.exp(_rows_to_cols(bcum[last:last + 1, :], DK_GLA))
            dec = jnp.concatenate([dec, dec], axis=1)
            sgla_ref[s, h] = s_old * dec + jnp.dot(kt_s, v, preferred_element_type=F32)
        y = _gla_finish(inner + cross, p_ref, h, gngw_ref)
        off = H_RET * DV_RET + h * DV_GLA
        mix_ref[:, off:off + DV_GLA] = y.astype(mix_ref.dtype)


def _mixer_sample(proj, sret0, sgla0, Bs, Ls, consts):
    n_seq = SAMPLE_GROUP
    rows = n_seq * Ls
    cosf, sinf, dmask, qdec, kdec, cdec, w2, b2, sums, pm, gnrw, gnrb, gngw = consts
    n_levels = pm.shape[0] - 1
    const2 = lambda g: (0, 0)
    const3 = lambda g: (0, 0, 0)
    kern = functools.partial(_mixer_sample_kernel, rows=rows, seq_len=Ls, n_levels=n_levels)
    st_spec = pl.BlockSpec((n_seq, H_RET, DK_RET, DV_RET), lambda g: (g, 0, 0, 0))
    return pl.pallas_call(
        kern,
        grid=(Bs // n_seq,),
        in_specs=[
            pl.BlockSpec((rows, D_IN_PAD), lambda g: (g, 0)),
            pl.BlockSpec(cosf.shape, const2),
            pl.BlockSpec(sinf.shape, const2),
            pl.BlockSpec(dmask.shape, const3),
            pl.BlockSpec(qdec.shape, const3),
            pl.BlockSpec(kdec.shape, const3),
            pl.BlockSpec(cdec.shape, const3),
            pl.BlockSpec(w2.shape, const2),
            pl.BlockSpec(b2.shape, const2),
            pl.BlockSpec(sums.shape, const2),
            pl.BlockSpec(pm.shape, const3),
            pl.BlockSpec(gnrw.shape, const2),
            pl.BlockSpec(gnrb.shape, const2),
            pl.BlockSpec(gngw.shape, const2),
            st_spec,
            st_spec,
        ],
        out_specs=[
            pl.BlockSpec((rows, D_MIX), lambda g: (g, 0)),
            st_spec,
            st_spec,
        ],
        out_shape=[
            jax.ShapeDtypeStruct((Bs * Ls, D_MIX), BF16),
            jax.ShapeDtypeStruct((Bs, H_RET, DK_RET, DV_RET), F32),
            jax.ShapeDtypeStruct((Bs, H_GLA, DK_GLA, DV_GLA), F32),
        ],
        compiler_params=_cparams(("parallel",)),
        name="mixer_sample",
    )(proj, cosf, sinf, dmask, qdec, kdec, cdec, w2, b2, sums, pm, gnrw, gnrb, gngw, sret0, sgla0)


def _out_query_kernel(mix_ref, x_ref, wo_ref, wn_ref, wq_ref, k1_ref, k2_ref,
                      x1_ref, h2_ref, s1_ref, s2_ref):
    x1 = x_ref[...] + jnp.dot(mix_ref[...], wo_ref[...], preferred_element_type=F32)
    x1_ref[...] = x1
    ms = jnp.mean(x1 * x1, axis=-1, keepdims=True)
    h2 = (x1 * lax.rsqrt(ms + EPS) * wn_ref[...]).astype(BF16)
    h2_ref[...] = h2
    q = jnp.dot(h2, wq_ref[...], preferred_element_type=F32).astype(BF16)
    half = PEER_DQ // 2
    for h in range(PEER_HEADS):
        q1 = q[:, h * PEER_DQ:h * PEER_DQ + half]
        q2 = q[:, h * PEER_DQ + half:(h + 1) * PEER_DQ]
        s1_ref[h] = _dot_nt(k1_ref[h], q1)
        s2_ref[h] = _dot_nt(k2_ref[h], q2)


def _out_query(mix, x, wo, wn, wq, k1, k2, tm):
    T, D = x.shape
    resident = dict(pipeline_mode=pl.Buffered(1))
    return pl.pallas_call(
        _out_query_kernel,
        grid=(T // tm,),
        in_specs=[
            pl.BlockSpec((tm, D_MIX), lambda i: (i, 0)),
            pl.BlockSpec((tm, D), lambda i: (i, 0)),
            pl.BlockSpec(wo.shape, lambda i: (0, 0), **resident),
            pl.BlockSpec((1, D), lambda i: (0, 0)),
            pl.BlockSpec(wq.shape, lambda i: (0, 0), **resident),
            pl.BlockSpec(k1.shape, lambda i: (0, 0, 0)),
            pl.BlockSpec(k2.shape, lambda i: (0, 0, 0)),
        ],
        out_specs=[
            pl.BlockSpec((tm, D), lambda i: (i, 0)),
            pl.BlockSpec((tm, D), lambda i: (i, 0)),
            pl.BlockSpec((PEER_HEADS, PEER_NKEYS, tm), lambda i: (0, 0, i)),
            pl.BlockSpec((PEER_HEADS, PEER_NKEYS, tm), lambda i: (0, 0, i)),
        ],
        out_shape=[
            jax.ShapeDtypeStruct((T, D), F32),
            jax.ShapeDtypeStruct((T, D), BF16),
            jax.ShapeDtypeStruct((PEER_HEADS, PEER_NKEYS, T), F32),
            jax.ShapeDtypeStruct((PEER_HEADS, PEER_NKEYS, T), F32),
        ],
        compiler_params=_cparams(("parallel",)),
        name="out_proj_peer_query",
    )(mix, x, wo, wn, wq, k1, k2)


def _candidate_pairs():
    return [(a, b) for a in range(PEER_TOPK) for b in range(PEER_TOPK) if (a + 1) * (b + 1) <= PEER_TOPK]


def _top_values(s, k):
    vals = []
    for _ in range(k):
        m = jnp.max(s, axis=0, keepdims=True)
        vals.append(m)
        s = jnp.where(s == m, -jnp.inf, s)
    return vals


def _route_stats_kernel(s1_ref, s2_ref, tau_ref, e1_ref, e2_ref):
    pairs = _candidate_pairs()
    n_cand = -(-len(pairs) // 8) * 8
    tt = s1_ref.shape[-1]
    rows = lax.broadcasted_iota(jnp.int32, (n_cand, tt), 0)
    for h in range(PEER_HEADS):
        s1 = s1_ref[h]
        s2 = s2_ref[h]
        v1 = _top_values(s1, PEER_TOPK)
        v2 = _top_values(s2, PEER_TOPK)
        cand = jnp.full((n_cand, tt), -jnp.inf, F32)
        for r, (a, b) in enumerate(pairs):
            cand = jnp.where(rows == r, v1[a] + v2[b], cand)
        tau = _top_values(cand, PEER_TOPK)[-1]
        z = jnp.sum(jnp.where(cand >= tau, jnp.exp(cand - (v1[0] + v2[0])), 0.0), axis=0, keepdims=True)
        tau_ref[h:h + 1, :] = tau
        e1_ref[h] = jnp.exp(s1 - v1[0]) / z
        e2_ref[h] = jnp.exp(s2 - v2[0])


def _route_stats(s1, s2, tt):
    H, K, T = s1.shape
    blk = pl.BlockSpec((H, K, tt), lambda i: (0, 0, i))
    return pl.pallas_call(
        _route_stats_kernel,
        grid=(T // tt,),
        in_specs=[blk, blk],
        out_specs=[pl.BlockSpec((H, tt), lambda i: (0, i)), blk, blk],
        out_shape=[
            jax.ShapeDtypeStruct((H, T), F32),
            jax.ShapeDtypeStruct((H, K, T), F32),
            jax.ShapeDtypeStruct((H, K, T), F32),
        ],
        compiler_params=_cparams(("parallel",)),
        name="peer_route_stats",
    )(s1, s2)


def _peer_kernel(h2_ref, u_ref, vt_ref, s1_ref, s2_ref, e1_ref, e2_ref, tau_ref, x1_ref, wn_ref,
                 y_ref, acc_ref, w_ref, *, eb, lane_chunk):
    e = pl.program_id(1)
    tt = h2_ref.shape[0]

    @pl.when(e == 0)
    def _():
        acc_ref[...] = jnp.zeros_like(acc_ref)

    st = _dot_nt(u_ref[...], h2_ref[...])
    for r in range(eb // PEER_NKEYS):
        i1 = e * (eb // PEER_NKEYS) + r
        for c0 in range(0, tt, lane_chunk):
            cs = slice(c0, c0 + lane_chunk)
            g = jnp.zeros((PEER_NKEYS, lane_chunk), F32)
            for h in range(PEER_HEADS):
                s1c = s1_ref[h, pl.ds(i1, 1), cs]
                e1c = e1_ref[h, pl.ds(i1, 1), cs]
                comb = s2_ref[h, :, cs] + s1c
                g = g + jnp.where(comb >= tau_ref[h:h + 1, cs], e2_ref[h, :, cs] * e1c, 0.0)
            s = st[r * PEER_NKEYS:(r + 1) * PEER_NKEYS, cs]
            act = 0.5 * s * (1.0 + lax.erf(s * (2.0 ** -0.5)))
            w_ref[r * PEER_NKEYS:(r + 1) * PEER_NKEYS, cs] = (g * act).astype(BF16)
    acc_ref[...] += jnp.dot(vt_ref[...], w_ref[...], preferred_element_type=F32)

    @pl.when(e == pl.num_programs(1) - 1)
    def _():
        out = x1_ref[...] + acc_ref[...].T
        ms = jnp.mean(out * out, axis=-1, keepdims=True)
        y_ref[...] = out * lax.rsqrt(ms + EPS) * wn_ref[...]


def _peer(h2, u_bf, vt_bf, s1, s2, e1, e2, tau, x1, wn, tt, eb):
    T, D = x1.shape
    E = u_bf.shape[0]
    stat = pl.BlockSpec((PEER_HEADS, PEER_NKEYS, tt), lambda i, e: (0, 0, i))
    kern = functools.partial(_peer_kernel, eb=eb, lane_chunk=min(tt, 2 * LANES))
    return pl.pallas_call(
        kern,
        grid=(T // tt, E // eb),
        in_specs=[
            pl.BlockSpec((tt, D), lambda i, e: (i, 0)),
            pl.BlockSpec((eb, D), lambda i, e: (e, 0)),
            pl.BlockSpec((D, eb), lambda i, e: (0, e)),
            stat, stat, stat, stat,
            pl.BlockSpec((PEER_HEADS, tt), lambda i, e: (0, i)),
            pl.BlockSpec((tt, D), lambda i, e: (i, 0)),
            pl.BlockSpec((1, D), lambda i, e: (0, 0)),
        ],
        out_specs=pl.BlockSpec((tt, D), lambda i, e: (i, 0)),
        out_shape=jax.ShapeDtypeStruct((T, D), F32),
        scratch_shapes=[pltpu.VMEM((D, tt), F32), pltpu.VMEM((eb, tt), BF16)],
        compiler_params=_cparams(("parallel", "arbitrary")),
        name="peer_dense",
    )(h2, u_bf, vt_bf, s1, s2, e1, e2, tau, x1, wn)


def _token_tile(T, pref):
    t = min(pref, T)
    while T % t:
        t //= 2
    return t


def _ffn(mix, x2d, wo, wnf, wq, k1, k2, u_bf, vt_bf, wn_final):
    T = x2d.shape[0]
    x1, h2, s1, s2 = _out_query(mix, x2d, wo, wnf, wq, k1, k2, _token_tile(T, 256))
    tau, e1, e2 = _route_stats(s1, s2, _token_tile(T, 256))
    return _peer(h2, u_bf, vt_bf, s1, s2, e1, e2, tau, x1, wn_final, _token_tile(T, 512), 512)


def kernel(x_prompt, x_sample, state_ret, state_gla, w_norm_mix, w_in, w_gla_a2, b_gla_a, gn_ret_w, gn_ret_b, gn_gla_w, w_out, w_norm_ffn, w_pq, sub_keys1, sub_keys2, u_tab, v_tab, w_norm_final):
    Bp, Lp, D = x_prompt.shape
    Bs, Ls, _ = x_sample.shape
    depth = w_in.shape[0]
    assert depth == 1 and D == D_MODEL
    l = 0
    xp = x_prompt.reshape(Bp * Lp, D)
    xs = x_sample.reshape(Bs * Ls, D)

    w_in_bf = jnp.pad(w_in[l], ((0, 0), (0, D_IN_PAD - w_in.shape[2]))).astype(BF16)
    w2 = jnp.pad(w_gla_a2[l], ((0, LANES - GLA_RANK), (0, 0))).astype(BF16)
    b2 = b_gla_a[l].reshape(1, -1)
    gnrw = gn_ret_w[l].reshape(1, -1)
    gnrb = gn_ret_b[l].reshape(1, -1)
    gngw = gn_gla_w[l].reshape(1, -1)
    wnm = w_norm_mix[l].reshape(1, D)
    wnf = w_norm_ffn[l].reshape(1, D)
    wn_final = w_norm_final.reshape(1, D)
    wo = w_out[l].astype(BF16)
    wq = w_pq[l].astype(BF16)
    k1 = sub_keys1[l].astype(BF16)
    k2 = sub_keys2[l].astype(BF16)
    u_bf = u_tab[l].astype(BF16)
    vt_bf = v_tab[l].astype(BF16).T

    cos_p, sin_p = _rope_tables(jnp.arange(Lp, dtype=F32))
    pos_s = PAST_LEN + jnp.arange(Ls, dtype=F32)
    cos_s, sin_s = _rope_tables(jnp.tile(pos_s, SAMPLE_GROUP))
    sums_p, pm_p = _gla_consts(PROMPT_CHUNK, PROMPT_CHUNK)
    sums_s, pm_s = _gla_consts(SAMPLE_GROUP * Ls, Ls)
    consts_p = (cos_p, sin_p) + _ret_consts(PROMPT_CHUNK, PROMPT_CHUNK) + (
        w2, b2, jnp.asarray(sums_p, BF16), jnp.asarray(pm_p), gnrw, gnrb, gngw)
    consts_s = (cos_s, sin_s) + _ret_consts(SAMPLE_GROUP * Ls, Ls) + (
        w2, b2, jnp.asarray(sums_s, BF16), jnp.asarray(pm_s), gnrw, gnrb, gngw)

    proj_p = _norm_matmul(xp, wnm, w_in_bf, 512, 1280)
    proj_s = _norm_matmul(xs, wnm, w_in_bf, 512, 1280)
    mix_p, rp, gp = _mixer_prompt(proj_p, Bp, Lp, consts_p)
    mix_s, rs, gs = _mixer_sample(proj_s, state_ret[l], state_gla[l], Bs, Ls, consts_s)

    yp = _ffn(mix_p, xp, wo, wnf, wq, k1, k2, u_bf, vt_bf, wn_final)
    ys = _ffn(mix_s, xs, wo, wnf, wq, k1, k2, u_bf, vt_bf, wn_final)
    return (yp.reshape(Bp, Lp, D), ys.reshape(Bs, Ls, D), rp[None], gp[None], rs[None], gs[None])
```

```python
import functools
import math

import numpy as np
import jax
import jax.numpy as jnp
from jax import lax
from jax.experimental import pallas as pl
from jax.experimental.pallas import tpu as pltpu

F32 = jnp.float32
BF16 = jnp.bfloat16

D_MODEL = 2048
H_RET, DK_RET, DV_RET = 4, 128, 256
H_GLA, DK_GLA, DV_GLA = 4, 128, 256
GLA_RANK = 16
GLA_TEMP = 16.0
ROPE_BASE = 10000.0
PAST_LEN = 16384
PEER_HEADS, PEER_NKEYS, PEER_DQ, PEER_TOPK = 8, 128, 256, 16
N_EXPERTS = PEER_NKEYS * PEER_NKEYS
EPS = 1e-6

LANES = 128
D_MIX = H_RET * DV_RET + H_GLA * DV_GLA
D_IN_MAIN = 2 * H_RET * DK_RET + 2 * H_RET * DV_RET + 2 * H_GLA * DK_GLA + 2 * H_GLA * DV_GLA
D_IN_PAD = 6400
OFF_QR = 0
OFF_KR = OFF_QR + H_RET * DK_RET
OFF_VR = OFF_KR + H_RET * DK_RET
OFF_GR = OFF_VR + H_RET * DV_RET
OFF_QG = OFF_GR + H_RET * DV_RET
OFF_KG = OFF_QG + H_GLA * DK_GLA
OFF_VG = OFF_KG + H_GLA * DK_GLA
OFF_GG = OFF_VG + H_GLA * DV_GLA
OFF_A = OFF_GG + H_GLA * DV_GLA

PROMPT_CHUNK = 128
SAMPLE_GROUP = 8
VMEM_LIMIT = 56 * 1024 * 1024


def _cparams(sem):
    return pltpu.CompilerParams(dimension_semantics=sem, vmem_limit_bytes=VMEM_LIMIT)


def _norm_matmul_kernel(x_ref, wn_ref, w_ref, o_ref, h_ref):
    @pl.when(pl.program_id(1) == 0)
    def _():
        x = x_ref[...]
        ms = jnp.mean(x * x, axis=-1, keepdims=True)
        h_ref[...] = (x * lax.rsqrt(ms + EPS) * wn_ref[...]).astype(BF16)

    o_ref[...] = jnp.dot(h_ref[...], w_ref[...], preferred_element_type=F32)


def _norm_matmul(x, wn, w, tm, tn):
    T, D = x.shape
    N = w.shape[1]
    return pl.pallas_call(
        _norm_matmul_kernel,
        grid=(T // tm, N // tn),
        in_specs=[
            pl.BlockSpec((tm, D), lambda i, j: (i, 0)),
            pl.BlockSpec((1, D), lambda i, j: (0, 0)),
            pl.BlockSpec((D, tn), lambda i, j: (0, j)),
        ],
        out_specs=pl.BlockSpec((tm, tn), lambda i, j: (i, j)),
        out_shape=jax.ShapeDtypeStruct((T, N), F32),
        scratch_shapes=[pltpu.VMEM((tm, D), BF16)],
        compiler_params=_cparams(("parallel", "arbitrary")),
        name="norm_in_proj",
    )(x, wn, w)


def _gla_consts(rows, blk):
    idx = np.arange(rows)
    seq = idx // blk
    t = idx[None, :]
    i = idx[:, None]
    same = seq[:, None] == seq[None, :]
    mats = [same & (t <= i), same & (t > i)]
    pmasks = []
    s = blk
    while s >= 2:
        bstart = (idx // s) * s
        mid = bstart + s // 2
        upper = idx >= mid
        mats.append(upper[:, None] & (t >= mid[:, None]) & (t <= i))
        mats.append((~upper)[:, None] & (t > i) & (t <= mid[:, None] - 1))
        pmasks.append((bstart[:, None] == bstart[None, :]) & upper[:, None] & (~upper)[None, :])
        s //= 2
    pmasks.append(np.eye(rows, dtype=bool))
    return (np.concatenate(mats, 0).astype(np.float32), np.stack(pmasks).astype(np.float32))


def _ret_consts(rows, blk):
    log_g = jnp.log1p(-jnp.exp2(-5.0 - jnp.arange(H_RET, dtype=F32)))
    idx = np.arange(rows)
    p = jnp.asarray(idx % blk, F32)
    same = jnp.asarray((idx[:, None] // blk) == (idx[None, :] // blk))
    dist = p[:, None] - p[None, :]
    causal = same & (dist >= 0)
    dmask = jnp.where(causal[None], jnp.exp(jnp.maximum(dist, 0.0)[None] * log_g[:, None, None]), 0.0)
    qdec = jnp.exp((p + 1.0)[None, :] * log_g[:, None])
    kdec = jnp.exp((blk - 1.0 - p)[None, :] * log_g[:, None])
    cdec = jnp.exp(blk * log_g)
    qdec_full = jnp.broadcast_to(qdec[:, :, None], (H_RET, rows, DV_RET))
    kdec_full = jnp.broadcast_to(kdec[:, :, None], (H_RET, rows, DK_RET))
    cdec_full = jnp.broadcast_to(cdec[:, None, None], (H_RET, DK_RET, DV_RET))
    return dmask.astype(F32), qdec_full.astype(F32), kdec_full.astype(F32), cdec_full.astype(F32)


def _rope_tables(pos):
    d = DK_RET
    freqs = ROPE_BASE ** (-jnp.arange(0, d, 2, dtype=F32) / d)
    ang = pos[:, None] * freqs[None, :]
    cos = jnp.cos(ang)
    sin = jnp.sin(ang)
    return jnp.concatenate([cos, cos], axis=-1), jnp.concatenate([-sin, sin], axis=-1)


def _dot_nt(a, b):
    return lax.dot_general(a, b, (((1,), (1,)), ((), ())), preferred_element_type=F32)


def _rotary(x, cosf, sinf):
    return x * cosf + pltpu.roll(x, DK_RET // 2, axis=1) * sinf


def _sigmoid(x):
    return 1.0 / (1.0 + jnp.exp(-x))


def _ret_head(p_ref, h, cosf, sinf, dmask_ref, qdec_ref, kdec_ref):
    q = _rotary(p_ref[:, OFF_QR + h * DK_RET:OFF_QR + (h + 1) * DK_RET], cosf, sinf)
    k = _rotary(p_ref[:, OFF_KR + h * DK_RET:OFF_KR + (h + 1) * DK_RET], cosf, sinf) * (DK_RET ** -0.5)
    v = p_ref[:, OFF_VR + h * DV_RET:OFF_VR + (h + 1) * DV_RET].astype(BF16)
    qb = q.astype(BF16)
    scores = _dot_nt(qb, k.astype(BF16)) * dmask_ref[h]
    inner = jnp.dot(scores.astype(BF16), v, preferred_element_type=F32)
    kt = (k * kdec_ref[h]).T
    return qb, inner, kt, v


def _ret_finish(o, p_ref, h, gnw_ref, gnb_ref):
    mu = jnp.mean(o, axis=-1, keepdims=True)
    d = o - mu
    var = jnp.mean(d * d, axis=-1, keepdims=True)
    sl = slice(h * DV_RET, (h + 1) * DV_RET)
    y = d * lax.rsqrt(var + EPS) * gnw_ref[:, sl] + gnb_ref[:, sl]
    g = p_ref[:, OFF_GR + h * DV_RET:OFF_GR + (h + 1) * DV_RET]
    return y * (g * _sigmoid(g))


def _gla_log_decay(p_ref, w2_ref, b2_ref):
    a = p_ref[:, OFF_A:OFF_A + LANES].astype(BF16)
    z = jnp.dot(a, w2_ref[...], preferred_element_type=F32) + b2_ref[...]
    return (jnp.minimum(z, 0.0) - jnp.log1p(jnp.exp(-jnp.abs(z)))) * (1.0 / GLA_TEMP)


def _gla_head(p_ref, h, la, sums_ref, pm_ref, rows, n_levels):
    la_h = la[:, h * DK_GLA:(h + 1) * DK_GLA]
    hi = la_h.astype(BF16)
    lo = (la_h - hi.astype(F32)).astype(BF16)
    e2 = jnp.dot(sums_ref[...], jnp.concatenate([hi, lo], axis=1), preferred_element_type=F32)
    ex = e2[:, :DK_GLA] + e2[:, DK_GLA:]
    bcum = ex[0:rows]
    brest = ex[rows:2 * rows]
    q = p_ref[:, OFF_QG + h * DK_GLA:OFF_QG + (h + 1) * DK_GLA] * (DK_GLA ** -0.5)
    k = p_ref[:, OFF_KG + h * DK_GLA:OFF_KG + (h + 1) * DK_GLA]
    v = p_ref[:, OFF_VG + h * DV_GLA:OFF_VG + (h + 1) * DV_GLA].astype(BF16)
    amat = _dot_nt(q.astype(BF16), k.astype(BF16)) * pm_ref[n_levels]
    for l in range(n_levels):
        up = ex[(2 + 2 * l) * rows:(3 + 2 * l) * rows]
        lw = ex[(3 + 2 * l) * rows:(4 + 2 * l) * rows]
        ql = (q * jnp.exp(up)).astype(BF16)
        kl = (k * jnp.exp(lw)).astype(BF16)
        amat = amat + _dot_nt(ql, kl) * pm_ref[l]
    inner = jnp.dot(amat.astype(BF16), v, preferred_element_type=F32)
    qd = (q * jnp.exp(bcum)).astype(BF16)
    kt = (k * jnp.exp(brest)).T
    return qd, inner, kt, v, bcum


def _gla_finish(o, p_ref, h, gw_ref):
    sl = slice(h * DV_GLA, (h + 1) * DV_GLA)
    y = o * lax.rsqrt(jnp.mean(o * o, axis=-1, keepdims=True) + EPS) * gw_ref[:, sl]
    g = p_ref[:, OFF_GG + h * DV_GLA:OFF_GG + (h + 1) * DV_GLA]
    return y * (g * _sigmoid(g))


def _rows_to_cols(row, n):
    return jnp.broadcast_to(row, (n, n)).T


def _mixer_prompt_kernel(p_ref, cos_ref, sin_ref, dmask_ref, qdec_ref, kdec_ref, cdec_ref,
                         w2_ref, b2_ref, sums_ref, pm_ref, gnrw_ref, gnrb_ref, gngw_ref,
                         mix_ref, sret_ref, sgla_ref, *, rows, n_levels):
    @pl.when(pl.program_id(1) == 0)
    def _():
        sret_ref[...] = jnp.zeros_like(sret_ref)
        sgla_ref[...] = jnp.zeros_like(sgla_ref)

    cosf = cos_ref[...]
    sinf = sin_ref[...]
    for h in range(H_RET):
        qb, inner, kt, v = _ret_head(p_ref, h, cosf, sinf, dmask_ref, qdec_ref, kdec_ref)
        s_old = sret_ref[0, h]
        cross = jnp.dot(qb, s_old.astype(BF16), preferred_element_type=F32) * qdec_ref[h]
        sret_ref[0, h] = s_old * cdec_ref[h] + jnp.dot(kt.astype(BF16), v, preferred_element_type=F32)
        y = _ret_finish(inner + cross, p_ref, h, gnrw_ref, gnrb_ref)
        mix_ref[:, h * DV_RET:(h + 1) * DV_RET] = y.astype(mix_ref.dtype)

    la = _gla_log_decay(p_ref, w2_ref, b2_ref)
    for h in range(H_GLA):
        qd, inner, kt, v, bcum = _gla_head(p_ref, h, la, sums_ref, pm_ref, rows, n_levels)
        s_old = sgla_ref[0, h]
        cross = jnp.dot(qd, s_old.astype(BF16), preferred_element_type=F32)
        dec = jnp.exp(_rows_to_cols(bcum[rows - 1:rows, :], DK_GLA))
        dec = jnp.concatenate([dec, dec], axis=1)
        sgla_ref[0, h] = s_old * dec + jnp.dot(kt.astype(BF16), v, preferred_element_type=F32)
        y = _gla_finish(inner + cross, p_ref, h, gngw_ref)
        off = H_RET * DV_RET + h * DV_GLA
        mix_ref[:, off:off + DV_GLA] = y.astype(mix_ref.dtype)


def _mixer_prompt(proj, B, L, consts):
    rows = PROMPT_CHUNK
    nc = L // rows
    cosf, sinf, dmask, qdec, kdec, cdec, w2, b2, sums, pm, gnrw, gnrb, gngw = consts
    n_levels = pm.shape[0] - 1
    const2 = lambda b, c: (0, 0)
    const3 = lambda b, c: (0, 0, 0)
    kern = functools.partial(_mixer_prompt_kernel, rows=rows, n_levels=n_levels)
    return pl.pallas_call(
        kern,
        grid=(B, nc),
        in_specs=[
            pl.BlockSpec((rows, D_IN_PAD), lambda b, c: (b * nc + c, 0)),
            pl.BlockSpec((rows, DK_RET), lambda b, c: (c, 0)),
            pl.BlockSpec((rows, DK_RET), lambda b, c: (c, 0)),
            pl.BlockSpec(dmask.shape, const3),
            pl.BlockSpec(qdec.shape, const3),
            pl.BlockSpec(kdec.shape, const3),
            pl.BlockSpec(cdec.shape, const3),
            pl.BlockSpec(w2.shape, const2),
            pl.BlockSpec(b2.shape, const2),
            pl.BlockSpec(sums.shape, const2),
            pl.BlockSpec(pm.shape, const3),
            pl.BlockSpec(gnrw.shape, const2),
            pl.BlockSpec(gnrb.shape, const2),
            pl.BlockSpec(gngw.shape, const2),
        ],
        out_specs=[
            pl.BlockSpec((rows, D_MIX), lambda b, c: (b * nc + c, 0)),
            pl.BlockSpec((1, H_RET, DK_RET, DV_RET), lambda b, c: (b, 0, 0, 0)),
            pl.BlockSpec((1, H_GLA, DK_GLA, DV_GLA), lambda b, c: (b, 0, 0, 0)),
        ],
        out_shape=[
            jax.ShapeDtypeStruct((B * L, D_MIX), BF16),
            jax.ShapeDtypeStruct((B, H_RET, DK_RET, DV_RET), F32),
            jax.ShapeDtypeStruct((B, H_GLA, DK_GLA, DV_GLA), F32),
        ],
        compiler_params=_cparams(("parallel", "arbitrary")),
        name="mixer_prompt",
    )(proj, cosf, sinf, dmask, qdec, kdec, cdec, w2, b2, sums, pm, gnrw, gnrb, gngw)


def _mixer_sample_kernel(p_ref, cos_ref, sin_ref, dmask_ref, qdec_ref, kdec_ref, cdec_ref,
                         w2_ref, b2_ref, sums_ref, pm_ref, gnrw_ref, gnrb_ref, gngw_ref,
                         sret0_ref, sgla0_ref,
                         mix_ref, sret_ref, sgla_ref, *, rows, seq_len, n_levels):
    n_seq = rows // seq_len
    row_seq = lax.broadcasted_iota(jnp.int32, (rows, 1), 0) // seq_len
    col_seq = lax.broadcasted_iota(jnp.int32, (1, rows), 1) // seq_len
    cosf = cos_ref[...]
    sinf = sin_ref[...]
    for h in range(H_RET):
        qb, inner, kt, v = _ret_head(p_ref, h, cosf, sinf, dmask_ref, qdec_ref, kdec_ref)
        cross = jnp.zeros((rows, DV_RET), F32)
        for s in range(n_seq):
            s_old = sret0_ref[s, h]
            c_s = jnp.dot(qb, s_old.astype(BF16), preferred_element_type=F32)
            cross = cross + jnp.where(row_seq == s, c_s, 0.0)
            kt_s = jnp.where(col_seq == s, kt, 0.0).astype(BF16)
            sret_ref[s, h] = s_old * cdec_ref[h] + jnp.dot(kt_s, v, preferred_element_type=F32)
        y = _ret_finish(inner + cross * qdec_ref[h], p_ref, h, gnrw_ref, gnrb_ref)
        mix_ref[:, h * DV_RET:(h + 1) * DV_RET] = y.astype(mix_ref.dtype)

    la = _gla_log_decay(p_ref, w2_ref, b2_ref)
    for h in range(H_GLA):
        qd, inner, kt, v, bcum = _gla_head(p_ref, h, la, sums_ref, pm_ref, rows, n_levels)
        cross = jnp.zeros((rows, DV_GLA), F32)
        for s in range(n_seq):
            s_old = sgla0_ref[s, h]
            c_s = jnp.dot(qd, s_old.astype(BF16), preferred_element_type=F32)
            cross = cross + jnp.where(row_seq == s, c_s, 0.0)
            kt_s = jnp.where(col_seq == s, kt, 0.0).astype(BF16)
            last = (s + 1) * seq_len - 1
            dec = jnp.exp(_rows_to_cols(bcum[last:last + 1, :], DK_GLA))
            dec = jnp.concatenate([dec, dec], axis=1)
            sgla_ref[s, h] = s_old * dec + jnp.dot(kt_s, v, preferred_element_type=F32)
        y = _gla_finish(inner + cross, p_ref, h, gngw_ref)
        off = H_RET * DV_RET + h * DV_GLA
        mix_ref[:, off:off + DV_GLA] = y.astype(mix_ref.dtype)


def _mixer_sample(proj, sret0, sgla0, Bs, Ls, consts):
    n_seq = SAMPLE_GROUP
    rows = n_seq * Ls
    cosf, sinf, dmask, qdec, kdec, cdec, w2, b2, sums, pm, gnrw, gnrb, gngw = consts
    n_levels = pm.shape[0] - 1
    const2 = lambda g: (0, 0)
    const3 = lambda g: (0, 0, 0)
    kern = functools.partial(_mixer_sample_kernel, rows=rows, seq_len=Ls, n_levels=n_levels)
    st_spec = pl.BlockSpec((n_seq, H_RET, DK_RET, DV_RET), lambda g: (g, 0, 0, 0))
    return pl.pallas_call(
        kern,
        grid=(Bs // n_seq,),
        in_specs=[
            pl.BlockSpec((rows, D_IN_PAD), lambda g: (g, 0)),
            pl.BlockSpec(cosf.shape, const2),
            pl.BlockSpec(sinf.shape, const2),
            pl.BlockSpec(dmask.shape, const3),
            pl.BlockSpec(qdec.shape, const3),
            pl.BlockSpec(kdec.shape, const3),
            pl.BlockSpec(cdec.shape, const3),
            pl.BlockSpec(w2.shape, const2),
            pl.BlockSpec(b2.shape, const2),
            pl.BlockSpec(sums.shape, const2),
            pl.BlockSpec(pm.shape, const3),
            pl.BlockSpec(gnrw.shape, const2),
            pl.BlockSpec(gnrb.shape, const2),
            pl.BlockSpec(gngw.shape, const2),
            st_spec,
            st_spec,
        ],
        out_specs=[
            pl.BlockSpec((rows, D_MIX), lambda g: (g, 0)),
            st_spec,
            st_spec,
        ],
        out_shape=[
            jax.ShapeDtypeStruct((Bs * Ls, D_MIX), BF16),
            jax.ShapeDtypeStruct((Bs, H_RET, DK_RET, DV_RET), F32),
            jax.ShapeDtypeStruct((Bs, H_GLA, DK_GLA, DV_GLA), F32),
        ],
        compiler_params=_cparams(("parallel",)),
        name="mixer_sample",
    )(proj, cosf, sinf, dmask, qdec, kdec, cdec, w2, b2, sums, pm, gnrw, gnrb, gngw, sret0, sgla0)


def _out_query_kernel(mix_ref, x_ref, wo_ref, wn_ref, wq_ref, k1_ref, k2_ref,
                      x1_ref, h2_ref, s1_ref, s2_ref):
    x1 = x_ref[...] + jnp.dot(mix_ref[...], wo_ref[...], preferred_element_type=F32)
    x1_ref[...] = x1
    ms = jnp.mean(x1 * x1, axis=-1, keepdims=True)
    h2 = (x1 * lax.rsqrt(ms + EPS) * wn_ref[...]).astype(BF16)
    h2_ref[...] = h2
    q = jnp.dot(h2, wq_ref[...], preferred_element_type=F32).astype(BF16)
    half = PEER_DQ // 2
    for h in range(PEER_HEADS):
        q1 = q[:, h * PEER_DQ:h * PEER_DQ + half]
        q2 = q[:, h * PEER_DQ + half:(h + 1) * PEER_DQ]
        s1_ref[h] = _dot_nt(k1_ref[h], q1)
        s2_ref[h] = _dot_nt(k2_ref[h], q2)


def _out_query(mix, x, wo, wn, wq, k1, k2, tm):
    T, D = x.shape
    resident = dict(pipeline_mode=pl.Buffered(1))
    return pl.pallas_call(
        _out_query_kernel,
        grid=(T // tm,),
        in_specs=[
            pl.BlockSpec((tm, D_MIX), lambda i: (i, 0)),
            pl.BlockSpec((tm, D), lambda i: (i, 0)),
            pl.BlockSpec(wo.shape, lambda i: (0, 0), **resident),
            pl.BlockSpec((1, D), lambda i: (0, 0)),
            pl.BlockSpec(wq.shape, lambda i: (0, 0), **resident),
            pl.BlockSpec(k1.shape, lambda i: (0, 0, 0)),
            pl.BlockSpec(k2.shape, lambda i: (0, 0, 0)),
        ],
        out_specs=[
            pl.BlockSpec((tm, D), lambda i: (i, 0)),
            pl.BlockSpec((tm, D), lambda i: (i, 0)),
            pl.BlockSpec((PEER_HEADS, PEER_NKEYS, tm), lambda i: (0, 0, i)),
            pl.BlockSpec((PEER_HEADS, PEER_NKEYS, tm), lambda i: (0, 0, i)),
        ],
        out_shape=[
            jax.ShapeDtypeStruct((T, D), F32),
            jax.ShapeDtypeStruct((T, D), BF16),
            jax.ShapeDtypeStruct((PEER_HEADS, PEER_NKEYS, T), F32),
            jax.ShapeDtypeStruct((PEER_HEADS, PEER_NKEYS, T), F32),
        ],
        compiler_params=_cparams(("parallel",)),
        name="out_proj_peer_query",
    )(mix, x, wo, wn, wq, k1, k2)


def _candidate_pairs():
    return [(a, b) for a in range(PEER_TOPK) for b in range(PEER_TOPK) if (a + 1) * (b + 1) <= PEER_TOPK]


def _top_values(s, k):
    vals = []
    rank = jnp.full(s.shape, float(k), F32)
    for i in range(k):
        m = jnp.max(s, axis=0, keepdims=True)
        vals.append(m)
        hit = s == m
        rank = jnp.where(hit, float(i), rank)
        s = jnp.where(hit, -jnp.inf, s)
    return vals, rank


def _route_stats_kernel(s1_ref, s2_ref, n1_ref, e1_ref, r2_ref, e2_ref):
    pairs = _candidate_pairs()
    n_cand = -(-len(pairs) // 8) * 8
    tt = s1_ref.shape[-1]
    rows = lax.broadcasted_iota(jnp.int32, (n_cand, tt), 0)
    for h in range(PEER_HEADS):
        s1 = s1_ref[h]
        s2 = s2_ref[h]
        v1, _ = _top_values(s1, PEER_TOPK)
        v2, rank2 = _top_values(s2, PEER_TOPK)
        cand = jnp.full((n_cand, tt), -jnp.inf, F32)
        for r, (a, b) in enumerate(pairs):
            cand = jnp.where(rows == r, v1[a] + v2[b], cand)
        tau = _top_values(cand, PEER_TOPK)[0][-1]
        z = jnp.sum(jnp.where(cand >= tau, jnp.exp(cand - (v1[0] + v2[0])), 0.0), axis=0, keepdims=True)
        n1 = jnp.zeros(s1.shape, F32)
        for b in range(PEER_TOPK):
            n1 = n1 + jnp.where(s1 + v2[b] >= tau, 1.0, 0.0)
        n1_ref[h] = n1
        e1_ref[h] = jnp.exp(s1 - v1[0]) / z
        r2_ref[h] = rank2.astype(BF16)
        e2_ref[h] = jnp.exp(s2 - v2[0]).astype(BF16)


def _route_stats(s1, s2, tt):
    H, K, T = s1.shape
    blk = pl.BlockSpec((H, K, tt), lambda i: (0, 0, i))
    return pl.pallas_call(
        _route_stats_kernel,
        grid=(T // tt,),
        in_specs=[blk, blk],
        out_specs=[blk, blk, blk, blk],
        out_shape=[
            jax.ShapeDtypeStruct((H, K, T), F32),
            jax.ShapeDtypeStruct((H, K, T), F32),
            jax.ShapeDtypeStruct((H, K, T), BF16),
            jax.ShapeDtypeStruct((H, K, T), BF16),
        ],
        compiler_params=_cparams(("parallel",)),
        name="peer_route_stats",
    )(s1, s2)


def _peer_step(e, st_new_ref, st_old_ref, h2s_ref, u_ref, vt_ref, n1_ref, e1_ref, r2s_ref, e2s_ref,
               acc_ref, w_ref, *, eb, lane_chunk):
    tt = h2s_ref.shape[0]
    blk = jnp.maximum(e - 1, 0)
    for r in range(eb // PEER_NKEYS):
        i1 = blk * (eb // PEER_NKEYS) + r
        rs = slice(r * PEER_NKEYS, (r + 1) * PEER_NKEYS)
        n1_rows = [n1_ref[h, pl.ds(i1, 1), :] for h in range(PEER_HEADS)]
        e1_rows = [e1_ref[h, pl.ds(i1, 1), :] for h in range(PEER_HEADS)]
        for c0 in range(0, tt, lane_chunk):
            cs = slice(c0, c0 + lane_chunk)
            g = jnp.zeros((PEER_NKEYS, lane_chunk), BF16)
            for h in range(PEER_HEADS):
                n1c = jnp.broadcast_to(n1_rows[h][:, cs], (PEER_NKEYS, lane_chunk)).astype(BF16)
                e1c = jnp.broadcast_to(e1_rows[h][:, cs], (PEER_NKEYS, lane_chunk)).astype(BF16)
                g = g + jnp.where(r2s_ref[h, :, cs] < n1c, e2s_ref[h, :, cs] * e1c, jnp.zeros_like(g))
            s = st_old_ref[rs, cs]
            act = 0.5 * s * (1.0 + lax.erf(s * (2.0 ** -0.5)))
            w_ref[rs, cs] = g * act.astype(BF16)
    st_new_ref[...] = _dot_nt(u_ref[...], h2s_ref[...])
    acc_ref[...] += jnp.dot(vt_ref[...], w_ref[...], preferred_element_type=F32)


def _peer_kernel(h2_ref, u_ref, vt_ref, n1_ref, e1_ref, r2_ref, e2_ref, x1_ref, wn_ref,
                 y_ref, acc_ref, w_ref, st0_ref, st1_ref, h2s_ref, r2s_ref, e2s_ref, *, eb, lane_chunk):
    e = pl.program_id(1)

    @pl.when(e == 0)
    def _():
        acc_ref[...] = jnp.zeros_like(acc_ref)
        st1_ref[...] = jnp.zeros_like(st1_ref)
        h2s_ref[...] = h2_ref[...]
        r2s_ref[...] = r2_ref[...]
        e2s_ref[...] = e2_ref[...]

    step = functools.partial(_peer_step, e, h2s_ref=h2s_ref, u_ref=u_ref, vt_ref=vt_ref, n1_ref=n1_ref,
                             e1_ref=e1_ref, r2s_ref=r2s_ref, e2s_ref=e2s_ref, acc_ref=acc_ref,
                             w_ref=w_ref, eb=eb, lane_chunk=lane_chunk)

    @pl.when(e % 2 == 0)
    def _():
        step(st0_ref, st1_ref)

    @pl.when(e % 2 == 1)
    def _():
        step(st1_ref, st0_ref)

    @pl.when(e == pl.num_programs(1) - 1)
    def _():
        out = x1_ref[...] + acc_ref[...].T
        ms = jnp.mean(out * out, axis=-1, keepdims=True)
        y_ref[...] = out * lax.rsqrt(ms + EPS) * wn_ref[...]


def _peer(h2, u_bf, vt_bf, n1, e1, r2, e2, x1, wn, tt, eb):
    T, D = x1.shape
    E = u_bf.shape[0]
    stat = pl.BlockSpec((PEER_HEADS, PEER_NKEYS, tt), lambda i, e: (0, 0, i))
    kern = functools.partial(_peer_kernel, eb=eb, lane_chunk=min(tt, LANES))
    nb = E // eb
    return pl.pallas_call(
        kern,
        grid=(T // tt, nb + 1),
        in_specs=[
            pl.BlockSpec((tt, D), lambda i, e: (i, 0)),
            pl.BlockSpec((eb, D), lambda i, e: (jnp.minimum(e, nb - 1), 0)),
            pl.BlockSpec((D, eb), lambda i, e: (0, jnp.maximum(e - 1, 0))),
            stat, stat, stat, stat,
            pl.BlockSpec((tt, D), lambda i, e: (i, 0)),
            pl.BlockSpec((1, D), lambda i, e: (0, 0)),
        ],
        out_specs=pl.BlockSpec((tt, D), lambda i, e: (i, 0)),
        out_shape=jax.ShapeDtypeStruct((T, D), F32),
        scratch_shapes=[
            pltpu.VMEM((D, tt), F32),
            pltpu.VMEM((eb, tt), BF16),
            pltpu.VMEM((eb, tt), F32),
            pltpu.VMEM((eb, tt), F32),
            pltpu.VMEM((tt, D), BF16),
            pltpu.VMEM((PEER_HEADS, PEER_NKEYS, tt), BF16),
            pltpu.VMEM((PEER_HEADS, PEER_NKEYS, tt), BF16),
        ],
        compiler_params=_cparams(("parallel", "arbitrary")),
        name="peer_dense",
    )(h2, u_bf, vt_bf, n1, e1, r2, e2, x1, wn)


def _token_tile(T, pref):
    t = min(pref, T)
    while T % t:
        t //= 2
    return t


def _ffn(mix, x2d, wo, wnf, wq, k1, k2, u_bf, vt_bf, wn_final):
    T = x2d.shape[0]
    x1, h2, s1, s2 = _out_query(mix, x2d, wo, wnf, wq, k1, k2, _token_tile(T, 256))
    n1, e1, r2, e2 = _route_stats(s1, s2, _token_tile(T, 256))
    return _peer(h2, u_bf, vt_bf, n1, e1, r2, e2, x1, wn_final, _token_tile(T, 512), 512)


def kernel(x_prompt, x_sample, state_ret, state_gla, w_norm_mix, w_in, w_gla_a2, b_gla_a, gn_ret_w, gn_ret_b, gn_gla_w, w_out, w_norm_ffn, w_pq, sub_keys1, sub_keys2, u_tab, v_tab, w_norm_final):
    Bp, Lp, D = x_prompt.shape
    Bs, Ls, _ = x_sample.shape
    depth = w_in.shape[0]
    assert depth == 1 and D == D_MODEL
    l = 0
    xp = x_prompt.reshape(Bp * Lp, D)
    xs = x_sample.reshape(Bs * Ls, D)

    w_in_bf = jnp.pad(w_in[l], ((0, 0), (0, D_IN_PAD - w_in.shape[2]))).astype(BF16)
    w2 = jnp.pad(w_gla_a2[l], ((0, LANES - GLA_RANK), (0, 0))).astype(BF16)
    b2 = b_gla_a[l].reshape(1, -1)
    gnrw = gn_ret_w[l].reshape(1, -1)
    gnrb = gn_ret_b[l].reshape(1, -1)
    gngw = gn_gla_w[l].reshape(1, -1)
    wnm = w_norm_mix[l].reshape(1, D)
    wnf = w_norm_ffn[l].reshape(1, D)
    wn_final = w_norm_final.reshape(1, D)
    wo = w_out[l].astype(BF16)
    wq = w_pq[l].astype(BF16)
    k1 = sub_keys1[l].astype(BF16)
    k2 = sub_keys2[l].astype(BF16)
    u_bf = u_tab[l].astype(BF16)
    vt_bf = v_tab[l].astype(BF16).T

    cos_p, sin_p = _rope_tables(jnp.arange(Lp, dtype=F32))
    pos_s = PAST_LEN + jnp.arange(Ls, dtype=F32)
    cos_s, sin_s = _rope_tables(jnp.tile(pos_s, SAMPLE_GROUP))
    sums_p, pm_p = _gla_consts(PROMPT_CHUNK, PROMPT_CHUNK)
    sums_s, pm_s = _gla_consts(SAMPLE_GROUP * Ls, Ls)
    consts_p = (cos_p, sin_p) + _ret_consts(PROMPT_CHUNK, PROMPT_CHUNK) + (
        w2, b2, jnp.asarray(sums_p, BF16), jnp.asarray(pm_p), gnrw, gnrb, gngw)
    consts_s = (cos_s, sin_s) + _ret_consts(SAMPLE_GROUP * Ls, Ls) + (
        w2, b2, jnp.asarray(sums_s, BF16), jnp.asarray(pm_s), gnrw, gnrb, gngw)

    proj_p = _norm_matmul(xp, wnm, w_in_bf, 512, 1280)
    proj_s = _norm_matmul(xs, wnm, w_in_bf, 512, 1280)
    mix_p, rp, gp = _mixer_prompt(proj_p, Bp, Lp, consts_p)
    mix_s, rs, gs = _mixer_sample(proj_s, state_ret[l], state_gla[l], Bs, Ls, consts_s)

    yp = _ffn(mix_p, xp, wo, wnf, wq, k1, k2, u_bf, vt_bf, wn_final)
    ys = _ffn(mix_s, xs, wo, wnf, wq, k1, k2, u_bf, vt_bf, wn_final)
    return (yp.reshape(Bp, Lp, D), ys.reshape(Bs, Ls, D), rp[None], gp[None], rs[None], gs[None])
```

```python
import functools
import math

import numpy as np
import jax
import jax.numpy as jnp
from jax import lax
from jax.experimental import pallas as pl
from jax.experimental.pallas import tpu as pltpu

F32 = jnp.float32
BF16 = jnp.bfloat16

D_MODEL = 2048
H_RET, DK_RET, DV_RET = 4, 128, 256
H_GLA, DK_GLA, DV_GLA = 4, 128, 256
GLA_RANK = 16
GLA_TEMP = 16.0
ROPE_BASE = 10000.0
PAST_LEN = 16384
PEER_HEADS, PEER_NKEYS, PEER_DQ, PEER_TOPK = 8, 128, 256, 16
N_EXPERTS = PEER_NKEYS * PEER_NKEYS
EPS = 1e-6

LANES = 128
D_MIX = H_RET * DV_RET + H_GLA * DV_GLA
D_IN_MAIN = 2 * H_RET * DK_RET + 2 * H_RET * DV_RET + 2 * H_GLA * DK_GLA + 2 * H_GLA * DV_GLA
D_IN_PAD = 6400
OFF_QR = 0
OFF_KR = OFF_QR + H_RET * DK_RET
OFF_VR = OFF_KR + H_RET * DK_RET
OFF_GR = OFF_VR + H_RET * DV_RET
OFF_QG = OFF_GR + H_RET * DV_RET
OFF_KG = OFF_QG + H_GLA * DK_GLA
OFF_VG = OFF_KG + H_GLA * DK_GLA
OFF_GG = OFF_VG + H_GLA * DV_GLA
OFF_A = OFF_GG + H_GLA * DV_GLA

PEER_TOKEN_TILE = 512
PEER_EXPERT_BLOCK = 512
PROMPT_CHUNK = 128
SAMPLE_GROUP = 8
VMEM_LIMIT = 56 * 1024 * 1024


def _cparams(sem):
    return pltpu.CompilerParams(dimension_semantics=sem, vmem_limit_bytes=VMEM_LIMIT)


def _norm_matmul_kernel(x_ref, wn_ref, w_ref, o_ref, h_ref):
    @pl.when(pl.program_id(1) == 0)
    def _():
        x = x_ref[...]
        ms = jnp.mean(x * x, axis=-1, keepdims=True)
        h_ref[...] = (x * lax.rsqrt(ms + EPS) * wn_ref[...]).astype(BF16)

    o_ref[...] = jnp.dot(h_ref[...], w_ref[...], preferred_element_type=F32)


def _norm_matmul(x, wn, w, tm, tn):
    T, D = x.shape
    N = w.shape[1]
    return pl.pallas_call(
        _norm_matmul_kernel,
        grid=(T // tm, N // tn),
        in_specs=[
            pl.BlockSpec((tm, D), lambda i, j: (i, 0)),
            pl.BlockSpec((1, D), lambda i, j: (0, 0)),
            pl.BlockSpec((D, tn), lambda i, j: (0, j)),
        ],
        out_specs=pl.BlockSpec((tm, tn), lambda i, j: (i, j)),
        out_shape=jax.ShapeDtypeStruct((T, N), F32),
        scratch_shapes=[pltpu.VMEM((tm, D), BF16)],
        compiler_params=_cparams(("parallel", "arbitrary")),
        name="norm_in_proj",
    )(x, wn, w)


def _gla_consts(rows, blk):
    idx = np.arange(rows)
    seq = idx // blk
    t = idx[None, :]
    i = idx[:, None]
    same = seq[:, None] == seq[None, :]
    mats = [same & (t <= i), same & (t > i)]
    pmasks = []
    s = blk
    while s >= 2:
        bstart = (idx // s) * s
        mid = bstart + s // 2
        upper = idx >= mid
        mats.append(upper[:, None] & (t >= mid[:, None]) & (t <= i))
        mats.append((~upper)[:, None] & (t > i) & (t <= mid[:, None] - 1))
        pmasks.append((bstart[:, None] == bstart[None, :]) & upper[:, None] & (~upper)[None, :])
        s //= 2
    pmasks.append(np.eye(rows, dtype=bool))
    return (np.concatenate(mats, 0).astype(np.float32), np.stack(pmasks).astype(np.float32))


def _ret_consts(rows, blk):
    log_g = jnp.log1p(-jnp.exp2(-5.0 - jnp.arange(H_RET, dtype=F32)))
    idx = np.arange(rows)
    p = jnp.asarray(idx % blk, F32)
    same = jnp.asarray((idx[:, None] // blk) == (idx[None, :] // blk))
    dist = p[:, None] - p[None, :]
    causal = same & (dist >= 0)
    dmask = jnp.where(causal[None], jnp.exp(jnp.maximum(dist, 0.0)[None] * log_g[:, None, None]), 0.0)
    qdec = jnp.exp((p + 1.0)[None, :] * log_g[:, None])
    kdec = jnp.exp((blk - 1.0 - p)[None, :] * log_g[:, None])
    cdec = jnp.exp(blk * log_g)
    qdec_full = jnp.broadcast_to(qdec[:, :, None], (H_RET, rows, DV_RET))
    kdec_full = jnp.broadcast_to(kdec[:, :, None], (H_RET, rows, DK_RET))
    cdec_full = jnp.broadcast_to(cdec[:, None, None], (H_RET, DK_RET, DV_RET))
    return dmask.astype(F32), qdec_full.astype(F32), kdec_full.astype(F32), cdec_full.astype(F32)


def _rope_tables(pos):
    d = DK_RET
    freqs = ROPE_BASE ** (-jnp.arange(0, d, 2, dtype=F32) / d)
    ang = pos[:, None] * freqs[None, :]
    cos = jnp.cos(ang)
    sin = jnp.sin(ang)
    return jnp.concatenate([cos, cos], axis=-1), jnp.concatenate([-sin, sin], axis=-1)


def _dot_nt(a, b):
    return lax.dot_general(a, b, (((1,), (1,)), ((), ())), preferred_element_type=F32)


def _rotary(x, cosf, sinf):
    return x * cosf + pltpu.roll(x, DK_RET // 2, axis=1) * sinf


def _sigmoid(x):
    return 1.0 / (1.0 + jnp.exp(-x))


def _ret_head(p_ref, h, cosf, sinf, dmask_ref, qdec_ref, kdec_ref):
    q = _rotary(p_ref[:, OFF_QR + h * DK_RET:OFF_QR + (h + 1) * DK_RET], cosf, sinf)
    k = _rotary(p_ref[:, OFF_KR + h * DK_RET:OFF_KR + (h + 1) * DK_RET], cosf, sinf) * (DK_RET ** -0.5)
    v = p_ref[:, OFF_VR + h * DV_RET:OFF_VR + (h + 1) * DV_RET].astype(BF16)
    qb = q.astype(BF16)
    scores = _dot_nt(qb, k.astype(BF16)) * dmask_ref[h]
    inner = jnp.dot(scores.astype(BF16), v, preferred_element_type=F32)
    kt = (k * kdec_ref[h]).T
    return qb, inner, kt, v


def _ret_finish(o, p_ref, h, gnw_ref, gnb_ref):
    mu = jnp.mean(o, axis=-1, keepdims=True)
    d = o - mu
    var = jnp.mean(d * d, axis=-1, keepdims=True)
    sl = slice(h * DV_RET, (h + 1) * DV_RET)
    y = d * lax.rsqrt(var + EPS) * gnw_ref[:, sl] + gnb_ref[:, sl]
    g = p_ref[:, OFF_GR + h * DV_RET:OFF_GR + (h + 1) * DV_RET]
    return y * (g * _sigmoid(g))


def _gla_log_decay(p_ref, w2_ref, b2_ref):
    a = p_ref[:, OFF_A:OFF_A + LANES].astype(BF16)
    z = jnp.dot(a, w2_ref[...], preferred_element_type=F32) + b2_ref[...]
    return (jnp.minimum(z, 0.0) - jnp.log1p(jnp.exp(-jnp.abs(z)))) * (1.0 / GLA_TEMP)


def _gla_head(p_ref, h, la, sums_ref, pm_ref, rows, n_levels):
    la_h = la[:, h * DK_GLA:(h + 1) * DK_GLA]
    hi = la_h.astype(BF16)
    lo = (la_h - hi.astype(F32)).astype(BF16)
    e2 = jnp.dot(sums_ref[...], jnp.concatenate([hi, lo], axis=1), preferred_element_type=F32)
    ex = e2[:, :DK_GLA] + e2[:, DK_GLA:]
    bcum = ex[0:rows]
    brest = ex[rows:2 * rows]
    q = p_ref[:, OFF_QG + h * DK_GLA:OFF_QG + (h + 1) * DK_GLA] * (DK_GLA ** -0.5)
    k = p_ref[:, OFF_KG + h * DK_GLA:OFF_KG + (h + 1) * DK_GLA]
    v = p_ref[:, OFF_VG + h * DV_GLA:OFF_VG + (h + 1) * DV_GLA].astype(BF16)
    amat = _dot_nt(q.astype(BF16), k.astype(BF16)) * pm_ref[n_levels]
    for l in range(n_levels):
        up = ex[(2 + 2 * l) * rows:(3 + 2 * l) * rows]
        lw = ex[(3 + 2 * l) * rows:(4 + 2 * l) * rows]
        ql = (q * jnp.exp(up)).astype(BF16)
        kl = (k * jnp.exp(lw)).astype(BF16)
        amat = amat + _dot_nt(ql, kl) * pm_ref[l]
    inner = jnp.dot(amat.astype(BF16), v, preferred_element_type=F32)
    qd = (q * jnp.exp(bcum)).astype(BF16)
    kt = (k * jnp.exp(brest)).T
    return qd, inner, kt, v, bcum


def _gla_finish(o, p_ref, h, gw_ref):
    sl = slice(h * DV_GLA, (h + 1) * DV_GLA)
    y = o * lax.rsqrt(jnp.mean(o * o, axis=-1, keepdims=True) + EPS) * gw_ref[:, sl]
    g = p_ref[:, OFF_GG + h * DV_GLA:OFF_GG + (h + 1) * DV_GLA]
    return y * (g * _sigmoid(g))


def _rows_to_cols(row, n):
    return jnp.broadcast_to(row, (n, n)).T


def _mixer_prompt_kernel(p_ref, cos_ref, sin_ref, dmask_ref, qdec_ref, kdec_ref, cdec_ref,
                         w2_ref, b2_ref, sums_ref, pm_ref, gnrw_ref, gnrb_ref, gngw_ref,
                         mix_ref, sret_ref, sgla_ref, *, rows, n_levels):
    @pl.when(pl.program_id(1) == 0)
    def _():
        sret_ref[...] = jnp.zeros_like(sret_ref)
        sgla_ref[...] = jnp.zeros_like(sgla_ref)

    cosf = cos_ref[...]
    sinf = sin_ref[...]
    for h in range(H_RET):
        qb, inner, kt, v = _ret_head(p_ref, h, cosf, sinf, dmask_ref, qdec_ref, kdec_ref)
        s_old = sret_ref[0, h]
        cross = jnp.dot(qb, s_old.astype(BF16), preferred_element_type=F32) * qdec_ref[h]
        sret_ref[0, h] = s_old * cdec_ref[h] + jnp.dot(kt.astype(BF16), v, preferred_element_type=F32)
        y = _ret_finish(inner + cross, p_ref, h, gnrw_ref, gnrb_ref)
        mix_ref[:, h * DV_RET:(h + 1) * DV_RET] = y.astype(mix_ref.dtype)

    la = _gla_log_decay(p_ref, w2_ref, b2_ref)
    for h in range(H_GLA):
        qd, inner, kt, v, bcum = _gla_head(p_ref, h, la, sums_ref, pm_ref, rows, n_levels)
        s_old = sgla_ref[0, h]
        cross = jnp.dot(qd, s_old.astype(BF16), preferred_element_type=F32)
        dec = jnp.exp(_rows_to_cols(bcum[rows - 1:rows, :], DK_GLA))
        dec = jnp.concatenate([dec, dec], axis=1)
        sgla_ref[0, h] = s_old * dec + jnp.dot(kt.astype(BF16), v, preferred_element_type=F32)
        y = _gla_finish(inner + cross, p_ref, h, gngw_ref)
        off = H_RET * DV_RET + h * DV_GLA
        mix_ref[:, off:off + DV_GLA] = y.astype(mix_ref.dtype)


def _mixer_prompt(proj, B, L, consts):
    rows = PROMPT_CHUNK
    nc = L // rows
    cosf, sinf, dmask, qdec, kdec, cdec, w2, b2, sums, pm, gnrw, gnrb, gngw = consts
    n_levels = pm.shape[0] - 1
    const2 = lambda b, c: (0, 0)
    const3 = lambda b, c: (0, 0, 0)
    kern = functools.partial(_mixer_prompt_kernel, rows=rows, n_levels=n_levels)
    return pl.pallas_call(
        kern,
        grid=(B, nc),
        in_specs=[
            pl.BlockSpec((rows, D_IN_PAD), lambda b, c: (b * nc + c, 0)),
            pl.BlockSpec((rows, DK_RET), lambda b, c: (c, 0)),
            pl.BlockSpec((rows, DK_RET), lambda b, c: (c, 0)),
            pl.BlockSpec(dmask.shape, const3),
            pl.BlockSpec(qdec.shape, const3),
            pl.BlockSpec(kdec.shape, const3),
            pl.BlockSpec(cdec.shape, const3),
            pl.BlockSpec(w2.shape, const2),
            pl.BlockSpec(b2.shape, const2),
            pl.BlockSpec(sums.shape, const2),
            pl.BlockSpec(pm.shape, const3),
            pl.BlockSpec(gnrw.shape, const2),
            pl.BlockSpec(gnrb.shape, const2),
            pl.BlockSpec(gngw.shape, const2),
        ],
        out_specs=[
            pl.BlockSpec((rows, D_MIX), lambda b, c: (b * nc + c, 0)),
            pl.BlockSpec((1, H_RET, DK_RET, DV_RET), lambda b, c: (b, 0, 0, 0)),
            pl.BlockSpec((1, H_GLA, DK_GLA, DV_GLA), lambda b, c: (b, 0, 0, 0)),
        ],
        out_shape=[
            jax.ShapeDtypeStruct((B * L, D_MIX), BF16),
            jax.ShapeDtypeStruct((B, H_RET, DK_RET, DV_RET), F32),
            jax.ShapeDtypeStruct((B, H_GLA, DK_GLA, DV_GLA), F32),
        ],
        compiler_params=_cparams(("parallel", "arbitrary")),
        name="mixer_prompt",
    )(proj, cosf, sinf, dmask, qdec, kdec, cdec, w2, b2, sums, pm, gnrw, gnrb, gngw)


def _mixer_sample_kernel(p_ref, cos_ref, sin_ref, dmask_ref, qdec_ref, kdec_ref, cdec_ref,
                         w2_ref, b2_ref, sums_ref, pm_ref, gnrw_ref, gnrb_ref, gngw_ref,
                         sret0_ref, sgla0_ref,
                         mix_ref, sret_ref, sgla_ref, *, rows, seq_len, n_levels):
    n_seq = rows // seq_len
    row_seq = lax.broadcasted_iota(jnp.int32, (rows, 1), 0) // seq_len
    col_seq = lax.broadcasted_iota(jnp.int32, (1, rows), 1) // seq_len
    cosf = cos_ref[...]
    sinf = sin_ref[...]
    for h in range(H_RET):
        qb, inner, kt, v = _ret_head(p_ref, h, cosf, sinf, dmask_ref, qdec_ref, kdec_ref)
        cross = jnp.zeros((rows, DV_RET), F32)
        for s in range(n_seq):
            s_old = sret0_ref[s, h]
            c_s = jnp.dot(qb, s_old.astype(BF16), preferred_element_type=F32)
            cross = cross + jnp.where(row_seq == s, c_s, 0.0)
            kt_s = jnp.where(col_seq == s, kt, 0.0).astype(BF16)
            sret_ref[s, h] = s_old * cdec_ref[h] + jnp.dot(kt_s, v, preferred_element_type=F32)
        y = _ret_finish(inner + cross * qdec_ref[h], p_ref, h, gnrw_ref, gnrb_ref)
        mix_ref[:, h * DV_RET:(h + 1) * DV_RET] = y.astype(mix_ref.dtype)

    la = _gla_log_decay(p_ref, w2_ref, b2_ref)
    for h in range(H_GLA):
        qd, inner, kt, v, bcum = _gla_head(p_ref, h, la, sums_ref, pm_ref, rows, n_levels)
        cross = jnp.zeros((rows, DV_GLA), F32)
        for s in range(n_seq):
            s_old = sgla0_ref[s, h]
            c_s = jnp.dot(qd, s_old.astype(BF16), preferred_element_type=F32)
            cross = cross + jnp.where(row_seq == s, c_s, 0.0)
            kt_s = jnp.where(col_seq == s, kt, 0.0).astype(BF16)
            last = (s + 1) * seq_len - 1
            dec = jnp.exp(_rows_to_cols(bcum[last:last + 1, :], DK_GLA))
            dec = jnp.concatenate([dec, dec], axis=1)
            sgla_ref[s, h] = s_old * dec + jnp.dot(kt_s, v, preferred_element_type=F32)
        y = _gla_finish(inner + cross, p_ref, h, gngw_ref)
        off = H_RET * DV_RET + h * DV_GLA
        mix_ref[:, off:off + DV_GLA] = y.astype(mix_ref.dtype)


def _mixer_sample(proj, sret0, sgla0, Bs, Ls, consts):
    n_seq = SAMPLE_GROUP
    rows = n_seq * Ls
    cosf, sinf, dmask, qdec, kdec, cdec, w2, b2, sums, pm, gnrw, gnrb, gngw = consts
    n_levels = pm.shape[0] - 1
    const2 = lambda g: (0, 0)
    const3 = lambda g: (0, 0, 0)
    kern = functools.partial(_mixer_sample_kernel, rows=rows, seq_len=Ls, n_levels=n_levels)
    st_spec = pl.BlockSpec((n_seq, H_RET, DK_RET, DV_RET), lambda g: (g, 0, 0, 0))
    return pl.pallas_call(
        kern,
        grid=(Bs // n_seq,),
        in_specs=[
            pl.BlockSpec((rows, D_IN_PAD), lambda g: (g, 0)),
            pl.BlockSpec(cosf.shape, const2),
            pl.BlockSpec(sinf.shape, const2),
            pl.BlockSpec(dmask.shape, const3),
            pl.BlockSpec(qdec.shape, const3),
            pl.BlockSpec(kdec.shape, const3),
            pl.BlockSpec(cdec.shape, const3),
            pl.BlockSpec(w2.shape, const2),
            pl.BlockSpec(b2.shape, const2),
            pl.BlockSpec(sums.shape, const2),
            pl.BlockSpec(pm.shape, const3),
            pl.BlockSpec(gnrw.shape, const2),
            pl.BlockSpec(gnrb.shape, const2),
            pl.BlockSpec(gngw.shape, const2),
            st_spec,
            st_spec,
        ],
        out_specs=[
            pl.BlockSpec((rows, D_MIX), lambda g: (g, 0)),
            st_spec,
            st_spec,
        ],
        out_shape=[
            jax.ShapeDtypeStruct((Bs * Ls, D_MIX), BF16),
            jax.ShapeDtypeStruct((Bs, H_RET, DK_RET, DV_RET), F32),
            jax.ShapeDtypeStruct((Bs, H_GLA, DK_GLA, DV_GLA), F32),
        ],
        compiler_params=_cparams(("parallel",)),
        name="mixer_sample",
    )(proj, cosf, sinf, dmask, qdec, kdec, cdec, w2, b2, sums, pm, gnrw, gnrb, gngw, sret0, sgla0)


def _out_query_kernel(mix_ref, x_ref, wo_ref, wn_ref, wq_ref, k1_ref, k2_ref,
                      x1_ref, h2_ref, s1_ref, s2_ref):
    x1 = x_ref[...] + jnp.dot(mix_ref[...], wo_ref[...], preferred_element_type=F32)
    x1_ref[...] = x1
    ms = jnp.mean(x1 * x1, axis=-1, keepdims=True)
    h2 = (x1 * lax.rsqrt(ms + EPS) * wn_ref[...]).astype(BF16)
    h2_ref[...] = h2
    q = jnp.dot(h2, wq_ref[...], preferred_element_type=F32).astype(BF16)
    half = PEER_DQ // 2
    for h in range(PEER_HEADS):
        q1 = q[:, h * PEER_DQ:h * PEER_DQ + half]
        q2 = q[:, h * PEER_DQ + half:(h + 1) * PEER_DQ]
        s1_ref[h] = _dot_nt(k1_ref[h], q1)
        s2_ref[h] = _dot_nt(k2_ref[h], q2)


def _out_query(mix, x, wo, wn, wq, k1, k2, tm):
    T, D = x.shape
    resident = dict(pipeline_mode=pl.Buffered(1))
    return pl.pallas_call(
        _out_query_kernel,
        grid=(T // tm,),
        in_specs=[
            pl.BlockSpec((tm, D_MIX), lambda i: (i, 0)),
            pl.BlockSpec((tm, D), lambda i: (i, 0)),
            pl.BlockSpec(wo.shape, lambda i: (0, 0), **resident),
            pl.BlockSpec((1, D), lambda i: (0, 0)),
            pl.BlockSpec(wq.shape, lambda i: (0, 0), **resident),
            pl.BlockSpec(k1.shape, lambda i: (0, 0, 0)),
            pl.BlockSpec(k2.shape, lambda i: (0, 0, 0)),
        ],
        out_specs=[
            pl.BlockSpec((tm, D), lambda i: (i, 0)),
            pl.BlockSpec((tm, D), lambda i: (i, 0)),
            pl.BlockSpec((PEER_HEADS, PEER_NKEYS, tm), lambda i: (0, 0, i)),
            pl.BlockSpec((PEER_HEADS, PEER_NKEYS, tm), lambda i: (0, 0, i)),
        ],
        out_shape=[
            jax.ShapeDtypeStruct((T, D), F32),
            jax.ShapeDtypeStruct((T, D), BF16),
            jax.ShapeDtypeStruct((PEER_HEADS, PEER_NKEYS, T), F32),
            jax.ShapeDtypeStruct((PEER_HEADS, PEER_NKEYS, T), F32),
        ],
        compiler_params=_cparams(("parallel",)),
        name="out_proj_peer_query",
    )(mix, x, wo, wn, wq, k1, k2)


def _candidate_pairs():
    return [(a, b) for a in range(PEER_TOPK) for b in range(PEER_TOPK) if (a + 1) * (b + 1) <= PEER_TOPK]


def _top_values(s, k):
    vals = []
    rank = jnp.full(s.shape, float(k), F32)
    for i in range(k):
        m = jnp.max(s, axis=0, keepdims=True)
        vals.append(m)
        hit = s == m
        rank = jnp.where(hit, float(i), rank)
        s = jnp.where(hit, -jnp.inf, s)
    return vals, rank


def _route_stats_kernel(s1_ref, s2_ref, n1_ref, e1_ref, r2_ref, e2_ref):
    pairs = _candidate_pairs()
    n_cand = -(-len(pairs) // 8) * 8
    tt = s1_ref.shape[-1]
    rows = lax.broadcasted_iota(jnp.int32, (n_cand, tt), 0)
    for h in range(PEER_HEADS):
        s1 = s1_ref[h]
        s2 = s2_ref[h]
        v1, _ = _top_values(s1, PEER_TOPK)
        v2, rank2 = _top_values(s2, PEER_TOPK)
        cand = jnp.full((n_cand, tt), -jnp.inf, F32)
        for r, (a, b) in enumerate(pairs):
            cand = jnp.where(rows == r, v1[a] + v2[b], cand)
        tau = _top_values(cand, PEER_TOPK)[0][-1]
        z = jnp.sum(jnp.where(cand >= tau, jnp.exp(cand - (v1[0] + v2[0])), 0.0), axis=0, keepdims=True)
        n1 = jnp.zeros(s1.shape, F32)
        for b in range(PEER_TOPK):
            n1 = n1 + jnp.where(s1 + v2[b] >= tau, 1.0, 0.0)
        n1_ref[h] = n1
        e1_ref[h] = jnp.exp(s1 - v1[0]) / z
        r2_ref[h] = rank2.astype(BF16)
        e2_ref[h] = jnp.exp(s2 - v2[0]).astype(BF16)


def _route_stats(s1, s2, tt):
    H, K, T = s1.shape
    blk = pl.BlockSpec((H, K, tt), lambda i: (0, 0, i))
    return pl.pallas_call(
        _route_stats_kernel,
        grid=(T // tt,),
        in_specs=[blk, blk],
        out_specs=[blk, blk, blk, blk],
        out_shape=[
            jax.ShapeDtypeStruct((H, K, T), F32),
            jax.ShapeDtypeStruct((H, K, T), F32),
            jax.ShapeDtypeStruct((H, K, T), BF16),
            jax.ShapeDtypeStruct((H, K, T), BF16),
        ],
        compiler_params=_cparams(("parallel",)),
        name="peer_route_stats",
    )(s1, s2)


def _peer_step(e, st_new_ref, st_old_ref, h2s_ref, u_ref, vt_ref, n1_ref, e1_ref, r2s_ref, e2s_ref,
               acc_ref, w_ref, *, eb, lane_chunk):
    tt = h2s_ref.shape[0]
    blk = jnp.maximum(e - 1, 0)
    for r in range(eb // PEER_NKEYS):
        i1 = blk * (eb // PEER_NKEYS) + r
        rs = slice(r * PEER_NKEYS, (r + 1) * PEER_NKEYS)
        n1_rows = [n1_ref[h, pl.ds(i1, 1), :] for h in range(PEER_HEADS)]
        e1_rows = [e1_ref[h, pl.ds(i1, 1), :] for h in range(PEER_HEADS)]
        for c0 in range(0, tt, lane_chunk):
            cs = slice(c0, c0 + lane_chunk)
            g = jnp.zeros((PEER_NKEYS, lane_chunk), BF16)
            for h in range(PEER_HEADS):
                n1c = jnp.broadcast_to(n1_rows[h][:, cs], (PEER_NKEYS, lane_chunk)).astype(BF16)
                e1c = jnp.broadcast_to(e1_rows[h][:, cs], (PEER_NKEYS, lane_chunk)).astype(BF16)
                g = g + jnp.where(r2s_ref[h, :, cs] < n1c, e2s_ref[h, :, cs] * e1c, jnp.zeros_like(g))
            s = st_old_ref[rs, cs]
            act = 0.5 * s * (1.0 + lax.erf(s * (2.0 ** -0.5)))
            w_ref[rs, cs] = g * act.astype(BF16)
    st_new_ref[...] = _dot_nt(u_ref[...], h2s_ref[...])
    acc_ref[...] += jnp.dot(vt_ref[...], w_ref[...], preferred_element_type=F32)


def _peer_kernel(h2_ref, u_ref, vt_ref, n1_ref, e1_ref, r2_ref, e2_ref, x1_ref, wn_ref,
                 y_ref, acc_ref, w_ref, st0_ref, st1_ref, h2s_ref, r2s_ref, e2s_ref, *, eb, lane_chunk):
    e = pl.program_id(1)

    @pl.when(e == 0)
    def _():
        acc_ref[...] = jnp.zeros_like(acc_ref)
        st1_ref[...] = jnp.zeros_like(st1_ref)
        h2s_ref[...] = h2_ref[...]
        r2s_ref[...] = r2_ref[...]
        e2s_ref[...] = e2_ref[...]

    step = functools.partial(_peer_step, e, h2s_ref=h2s_ref, u_ref=u_ref, vt_ref=vt_ref, n1_ref=n1_ref,
                             e1_ref=e1_ref, r2s_ref=r2s_ref, e2s_ref=e2s_ref, acc_ref=acc_ref,
                             w_ref=w_ref, eb=eb, lane_chunk=lane_chunk)

    @pl.when(e % 2 == 0)
    def _():
        step(st0_ref, st1_ref)

    @pl.when(e % 2 == 1)
    def _():
        step(st1_ref, st0_ref)

    @pl.when(e == pl.num_programs(1) - 1)
    def _():
        out = x1_ref[...] + acc_ref[...].T
        ms = jnp.mean(out * out, axis=-1, keepdims=True)
        y_ref[...] = out * lax.rsqrt(ms + EPS) * wn_ref[...]


def _peer(h2, u_bf, vt_bf, n1, e1, r2, e2, x1, wn, tt):
    T, D = x1.shape
    nb, _, eb = vt_bf.shape
    stat = pl.BlockSpec((PEER_HEADS, PEER_NKEYS, tt), lambda i, e: (0, 0, i))
    kern = functools.partial(_peer_kernel, eb=eb, lane_chunk=min(tt, LANES))
    return pl.pallas_call(
        kern,
        grid=(T // tt, nb + 1),
        in_specs=[
            pl.BlockSpec((tt, D), lambda i, e: (i, 0)),
            pl.BlockSpec((eb, D), lambda i, e: (jnp.minimum(e, nb - 1), 0)),
            pl.BlockSpec((None, D, eb), lambda i, e: (jnp.maximum(e - 1, 0), 0, 0)),
            stat, stat, stat, stat,
            pl.BlockSpec((tt, D), lambda i, e: (i, 0)),
            pl.BlockSpec((1, D), lambda i, e: (0, 0)),
        ],
        out_specs=pl.BlockSpec((tt, D), lambda i, e: (i, 0)),
        out_shape=jax.ShapeDtypeStruct((T, D), F32),
        scratch_shapes=[
            pltpu.VMEM((D, tt), F32),
            pltpu.VMEM((eb, tt), BF16),
            pltpu.VMEM((eb, tt), F32),
            pltpu.VMEM((eb, tt), F32),
            pltpu.VMEM((tt, D), BF16),
            pltpu.VMEM((PEER_HEADS, PEER_NKEYS, tt), BF16),
            pltpu.VMEM((PEER_HEADS, PEER_NKEYS, tt), BF16),
        ],
        compiler_params=_cparams(("parallel", "arbitrary")),
        name="peer_dense",
    )(h2, u_bf, vt_bf, n1, e1, r2, e2, x1, wn)


def _token_tile(T, pref):
    t = min(pref, T)
    while T % t:
        t //= 2
    return t


def _ffn(mix, x2d, wo, wnf, wq, k1, k2, u_bf, vt_bf, wn_final):
    T = x2d.shape[0]
    x1, h2, s1, s2 = _out_query(mix, x2d, wo, wnf, wq, k1, k2, _token_tile(T, 256))
    n1, e1, r2, e2 = _route_stats(s1, s2, _token_tile(T, 256))
    return _peer(h2, u_bf, vt_bf, n1, e1, r2, e2, x1, wn_final, _token_tile(T, PEER_TOKEN_TILE))


def kernel(x_prompt, x_sample, state_ret, state_gla, w_norm_mix, w_in, w_gla_a2, b_gla_a, gn_ret_w, gn_ret_b, gn_gla_w, w_out, w_norm_ffn, w_pq, sub_keys1, sub_keys2, u_tab, v_tab, w_norm_final):
    Bp, Lp, D = x_prompt.shape
    Bs, Ls, _ = x_sample.shape
    depth = w_in.shape[0]
    assert depth == 1 and D == D_MODEL
    l = 0
    xp = x_prompt.reshape(Bp * Lp, D)
    xs = x_sample.reshape(Bs * Ls, D)

    w_in_bf = jnp.pad(w_in[l], ((0, 0), (0, D_IN_PAD - w_in.shape[2]))).astype(BF16)
    w2 = jnp.pad(w_gla_a2[l], ((0, LANES - GLA_RANK), (0, 0))).astype(BF16)
    b2 = b_gla_a[l].reshape(1, -1)
    gnrw = gn_ret_w[l].reshape(1, -1)
    gnrb = gn_ret_b[l].reshape(1, -1)
    gngw = gn_gla_w[l].reshape(1, -1)
    wnm = w_norm_mix[l].reshape(1, D)
    wnf = w_norm_ffn[l].reshape(1, D)
    wn_final = w_norm_final.reshape(1, D)
    wo = w_out[l].astype(BF16)
    wq = w_pq[l].astype(BF16)
    k1 = sub_keys1[l].astype(BF16)
    k2 = sub_keys2[l].astype(BF16)
    u_bf = u_tab[l].astype(BF16)
    vt_bf = v_tab[l].astype(BF16).reshape(N_EXPERTS // PEER_EXPERT_BLOCK, PEER_EXPERT_BLOCK, D)
    vt_bf = vt_bf.transpose(0, 2, 1)

    cos_p, sin_p = _rope_tables(jnp.arange(Lp, dtype=F32))
    pos_s = PAST_LEN + jnp.arange(Ls, dtype=F32)
    cos_s, sin_s = _rope_tables(jnp.tile(pos_s, SAMPLE_GROUP))
    sums_p, pm_p = _gla_consts(PROMPT_CHUNK, PROMPT_CHUNK)
    sums_s, pm_s = _gla_consts(SAMPLE_GROUP * Ls, Ls)
    consts_p = (cos_p, sin_p) + _ret_consts(PROMPT_CHUNK, PROMPT_CHUNK) + (
        w2, b2, jnp.asarray(sums_p, BF16), jnp.asarray(pm_p), gnrw, gnrb, gngw)
    consts_s = (cos_s, sin_s) + _ret_consts(SAMPLE_GROUP * Ls, Ls) + (
        w2, b2, jnp.asarray(sums_s, BF16), jnp.asarray(pm_s), gnrw, gnrb, gngw)

    proj_p = _norm_matmul(xp, wnm, w_in_bf, 512, 1280)
    proj_s = _norm_matmul(xs, wnm, w_in_bf, 512, 1280)
    mix_p, rp, gp = _mixer_prompt(proj_p, Bp, Lp, consts_p)
    mix_s, rs, gs = _mixer_sample(proj_s, state_ret[l], state_gla[l], Bs, Ls, consts_s)

    yp = _ffn(mix_p, xp, wo, wnf, wq, k1, k2, u_bf, vt_bf, wn_final)
    ys = _ffn(mix_s, xs, wo, wnf, wq, k1, k2, u_bf, vt_bf, wn_final)
    return (yp.reshape(Bp, Lp, D), ys.reshape(Bs, Ls, D), rp[None], gp[None], rs[None], gs[None])
```

```python
import functools
import math

import numpy as np
import jax
import jax.numpy as jnp
from jax import lax
from jax.experimental import pallas as pl
from jax.experimental.pallas import tpu as pltpu

F32 = jnp.float32
BF16 = jnp.bfloat16
ROUTE_DTYPE = BF16

D_MODEL = 2048
H_RET, DK_RET, DV_RET = 4, 128, 256
H_GLA, DK_GLA, DV_GLA = 4, 128, 256
GLA_RANK = 16
GLA_TEMP = 16.0
ROPE_BASE = 10000.0
PAST_LEN = 16384
PEER_HEADS, PEER_NKEYS, PEER_DQ, PEER_TOPK = 8, 128, 256, 16
N_EXPERTS = PEER_NKEYS * PEER_NKEYS
EPS = 1e-6

LANES = 128
D_MIX = H_RET * DV_RET + H_GLA * DV_GLA
D_IN_MAIN = 2 * H_RET * DK_RET + 2 * H_RET * DV_RET + 2 * H_GLA * DK_GLA + 2 * H_GLA * DV_GLA
D_IN_PAD = 6400
OFF_QR = 0
OFF_KR = OFF_QR + H_RET * DK_RET
OFF_VR = OFF_KR + H_RET * DK_RET
OFF_GR = OFF_VR + H_RET * DV_RET
OFF_QG = OFF_GR + H_RET * DV_RET
OFF_KG = OFF_QG + H_GLA * DK_GLA
OFF_VG = OFF_KG + H_GLA * DK_GLA
OFF_GG = OFF_VG + H_GLA * DV_GLA
OFF_A = OFF_GG + H_GLA * DV_GLA

IN_PROJ_TOKEN_TILE = 1024
IN_PROJ_COL_TILE = 1280
PEER_TOKEN_TILE = 512
PEER_EXPERT_BLOCK = 512
PROMPT_CHUNK = 128
SAMPLE_GROUP = 8
VMEM_LIMIT = 56 * 1024 * 1024


def _cparams(sem):
    return pltpu.CompilerParams(dimension_semantics=sem, vmem_limit_bytes=VMEM_LIMIT)


def _norm_matmul_kernel(x_ref, wn_ref, w_ref, o_ref, h_ref):
    @pl.when(pl.program_id(1) == 0)
    def _():
        x = x_ref[...]
        ms = jnp.mean(x * x, axis=-1, keepdims=True)
        h_ref[...] = (x * lax.rsqrt(ms + EPS) * wn_ref[...]).astype(BF16)

    o_ref[...] = jnp.dot(h_ref[...], w_ref[...], preferred_element_type=F32)


def _norm_matmul(x, wn, w, tm, tn):
    T, D = x.shape
    N = w.shape[1]
    return pl.pallas_call(
        _norm_matmul_kernel,
        grid=(T // tm, N // tn),
        in_specs=[
            pl.BlockSpec((tm, D), lambda i, j: (i, 0)),
            pl.BlockSpec((1, D), lambda i, j: (0, 0)),
            pl.BlockSpec((D, tn), lambda i, j: (0, j)),
        ],
        out_specs=pl.BlockSpec((tm, tn), lambda i, j: (i, j)),
        out_shape=jax.ShapeDtypeStruct((T, N), F32),
        scratch_shapes=[pltpu.VMEM((tm, D), BF16)],
        compiler_params=_cparams(("parallel", "arbitrary")),
        name="norm_in_proj",
    )(x, wn, w)


def _gla_consts(rows, blk):
    idx = np.arange(rows)
    seq = idx // blk
    t = idx[None, :]
    i = idx[:, None]
    same = seq[:, None] == seq[None, :]
    mats = [same & (t <= i), same & (t > i)]
    pmasks = []
    s = blk
    while s >= 2:
        bstart = (idx // s) * s
        mid = bstart + s // 2
        upper = idx >= mid
        mats.append(upper[:, None] & (t >= mid[:, None]) & (t <= i))
        mats.append((~upper)[:, None] & (t > i) & (t <= mid[:, None] - 1))
        pmasks.append((bstart[:, None] == bstart[None, :]) & upper[:, None] & (~upper)[None, :])
        s //= 2
    pmasks.append(np.eye(rows, dtype=bool))
    return (np.concatenate(mats, 0).astype(np.float32), np.stack(pmasks).astype(np.float32))


def _ret_consts(rows, blk):
    log_g = jnp.log1p(-jnp.exp2(-5.0 - jnp.arange(H_RET, dtype=F32)))
    idx = np.arange(rows)
    p = jnp.asarray(idx % blk, F32)
    same = jnp.asarray((idx[:, None] // blk) == (idx[None, :] // blk))
    dist = p[:, None] - p[None, :]
    causal = same & (dist >= 0)
    dmask = jnp.where(causal[None], jnp.exp(jnp.maximum(dist, 0.0)[None] * log_g[:, None, None]), 0.0)
    qdec = jnp.exp((p + 1.0)[None, :] * log_g[:, None])
    kdec = jnp.exp((blk - 1.0 - p)[None, :] * log_g[:, None])
    cdec = jnp.exp(blk * log_g)
    qdec_full = jnp.broadcast_to(qdec[:, :, None], (H_RET, rows, DV_RET))
    kdec_full = jnp.broadcast_to(kdec[:, :, None], (H_RET, rows, DK_RET))
    cdec_full = jnp.broadcast_to(cdec[:, None, None], (H_RET, DK_RET, DV_RET))
    return dmask.astype(F32), qdec_full.astype(F32), kdec_full.astype(F32), cdec_full.astype(F32)


def _rope_tables(pos):
    d = DK_RET
    freqs = ROPE_BASE ** (-jnp.arange(0, d, 2, dtype=F32) / d)
    ang = pos[:, None] * freqs[None, :]
    cos = jnp.cos(ang)
    sin = jnp.sin(ang)
    return jnp.concatenate([cos, cos], axis=-1), jnp.concatenate([-sin, sin], axis=-1)


def _dot_nt(a, b):
    return lax.dot_general(a, b, (((1,), (1,)), ((), ())), preferred_element_type=F32)


def _rotary(x, cosf, sinf):
    return x * cosf + pltpu.roll(x, DK_RET // 2, axis=1) * sinf


def _sigmoid(x):
    return 1.0 / (1.0 + jnp.exp(-x))


def _ret_head(p_ref, h, cosf, sinf, dmask_ref, qdec_ref, kdec_ref):
    q = _rotary(p_ref[:, OFF_QR + h * DK_RET:OFF_QR + (h + 1) * DK_RET], cosf, sinf)
    k = _rotary(p_ref[:, OFF_KR + h * DK_RET:OFF_KR + (h + 1) * DK_RET], cosf, sinf) * (DK_RET ** -0.5)
    v = p_ref[:, OFF_VR + h * DV_RET:OFF_VR + (h + 1) * DV_RET].astype(BF16)
    qb = q.astype(BF16)
    scores = _dot_nt(qb, k.astype(BF16)) * dmask_ref[h]
    inner = jnp.dot(scores.astype(BF16), v, preferred_element_type=F32)
    kt = (k * kdec_ref[h]).T
    return qb, inner, kt, v


def _ret_finish(o, p_ref, h, gnw_ref, gnb_ref):
    mu = jnp.mean(o, axis=-1, keepdims=True)
    d = o - mu
    var = jnp.mean(d * d, axis=-1, keepdims=True)
    sl = slice(h * DV_RET, (h + 1) * DV_RET)
    y = d * lax.rsqrt(var + EPS) * gnw_ref[:, sl] + gnb_ref[:, sl]
    g = p_ref[:, OFF_GR + h * DV_RET:OFF_GR + (h + 1) * DV_RET]
    return y * (g * _sigmoid(g))


def _gla_log_decay(p_ref, w2_ref, b2_ref):
    a = p_ref[:, OFF_A:OFF_A + LANES].astype(BF16)
    z = jnp.dot(a, w2_ref[...], preferred_element_type=F32) + b2_ref[...]
    return (jnp.minimum(z, 0.0) - jnp.log1p(jnp.exp(-jnp.abs(z)))) * (1.0 / GLA_TEMP)


def _gla_head(p_ref, h, la, sums_ref, pm_ref, rows, n_levels):
    la_h = la[:, h * DK_GLA:(h + 1) * DK_GLA]
    hi = la_h.astype(BF16)
    lo = (la_h - hi.astype(F32)).astype(BF16)
    e2 = jnp.dot(sums_ref[...], jnp.concatenate([hi, lo], axis=1), preferred_element_type=F32)
    ex = e2[:, :DK_GLA] + e2[:, DK_GLA:]
    bcum = ex[0:rows]
    brest = ex[rows:2 * rows]
    q = p_ref[:, OFF_QG + h * DK_GLA:OFF_QG + (h + 1) * DK_GLA] * (DK_GLA ** -0.5)
    k = p_ref[:, OFF_KG + h * DK_GLA:OFF_KG + (h + 1) * DK_GLA]
    v = p_ref[:, OFF_VG + h * DV_GLA:OFF_VG + (h + 1) * DV_GLA].astype(BF16)
    amat = _dot_nt(q.astype(BF16), k.astype(BF16)) * pm_ref[n_levels]
    for l in range(n_levels):
        up = ex[(2 + 2 * l) * rows:(3 + 2 * l) * rows]
        lw = ex[(3 + 2 * l) * rows:(4 + 2 * l) * rows]
        ql = (q * jnp.exp(up)).astype(BF16)
        kl = (k * jnp.exp(lw)).astype(BF16)
        amat = amat + _dot_nt(ql, kl) * pm_ref[l]
    inner = jnp.dot(amat.astype(BF16), v, preferred_element_type=F32)
    qd = (q * jnp.exp(bcum)).astype(BF16)
    kt = (k * jnp.exp(brest)).T
    return qd, inner, kt, v, bcum


def _gla_finish(o, p_ref, h, gw_ref):
    sl = slice(h * DV_GLA, (h + 1) * DV_GLA)
    y = o * lax.rsqrt(jnp.mean(o * o, axis=-1, keepdims=True) + EPS) * gw_ref[:, sl]
    g = p_ref[:, OFF_GG + h * DV_GLA:OFF_GG + (h + 1) * DV_GLA]
    return y * (g * _sigmoid(g))


def _rows_to_cols(row, n):
    return jnp.broadcast_to(row, (n, n)).T


def _mixer_prompt_kernel(p_ref, cos_ref, sin_ref, dmask_ref, qdec_ref, kdec_ref, cdec_ref,
                         w2_ref, b2_ref, sums_ref, pm_ref, gnrw_ref, gnrb_ref, gngw_ref,
                         mix_ref, sret_ref, sgla_ref, *, rows, n_levels):
    @pl.when(pl.program_id(1) == 0)
    def _():
        sret_ref[...] = jnp.zeros_like(sret_ref)
        sgla_ref[...] = jnp.zeros_like(sgla_ref)

    cosf = cos_ref[...]
    sinf = sin_ref[...]
    for h in range(H_RET):
        qb, inner, kt, v = _ret_head(p_ref, h, cosf, sinf, dmask_ref, qdec_ref, kdec_ref)
        s_old = sret_ref[0, h]
        cross = jnp.dot(qb, s_old.astype(BF16), preferred_element_type=F32) * qdec_ref[h]
        sret_ref[0, h] = s_old * cdec_ref[h] + jnp.dot(kt.astype(BF16), v, preferred_element_type=F32)
        y = _ret_finish(inner + cross, p_ref, h, gnrw_ref, gnrb_ref)
        mix_ref[:, h * DV_RET:(h + 1) * DV_RET] = y.astype(mix_ref.dtype)

    la = _gla_log_decay(p_ref, w2_ref, b2_ref)
    for h in range(H_GLA):
        qd, inner, kt, v, bcum = _gla_head(p_ref, h, la, sums_ref, pm_ref, rows, n_levels)
        s_old = sgla_ref[0, h]
        cross = jnp.dot(qd, s_old.astype(BF16), preferred_element_type=F32)
        dec = jnp.exp(_rows_to_cols(bcum[rows - 1:rows, :], DK_GLA))
        dec = jnp.concatenate([dec, dec], axis=1)
        sgla_ref[0, h] = s_old * dec + jnp.dot(kt.astype(BF16), v, preferred_element_type=F32)
        y = _gla_finish(inner + cross, p_ref, h, gngw_ref)
        off = H_RET * DV_RET + h * DV_GLA
        mix_ref[:, off:off + DV_GLA] = y.astype(mix_ref.dtype)


def _mixer_prompt(proj, B, L, consts):
    rows = PROMPT_CHUNK
    nc = L // rows
    cosf, sinf, dmask, qdec, kdec, cdec, w2, b2, sums, pm, gnrw, gnrb, gngw = consts
    n_levels = pm.shape[0] - 1
    const2 = lambda b, c: (0, 0)
    const3 = lambda b, c: (0, 0, 0)
    kern = functools.partial(_mixer_prompt_kernel, rows=rows, n_levels=n_levels)
    return pl.pallas_call(
        kern,
        grid=(B, nc),
        in_specs=[
            pl.BlockSpec((rows, D_IN_PAD), lambda b, c: (b * nc + c, 0)),
            pl.BlockSpec((rows, DK_RET), lambda b, c: (c, 0)),
            pl.BlockSpec((rows, DK_RET), lambda b, c: (c, 0)),
            pl.BlockSpec(dmask.shape, const3),
            pl.BlockSpec(qdec.shape, const3),
            pl.BlockSpec(kdec.shape, const3),
            pl.BlockSpec(cdec.shape, const3),
            pl.BlockSpec(w2.shape, const2),
            pl.BlockSpec(b2.shape, const2),
            pl.BlockSpec(sums.shape, const2),
            pl.BlockSpec(pm.shape, const3),
            pl.BlockSpec(gnrw.shape, const2),
            pl.BlockSpec(gnrb.shape, const2),
            pl.BlockSpec(gngw.shape, const2),
        ],
        out_specs=[
            pl.BlockSpec((rows, D_MIX), lambda b, c: (b * nc + c, 0)),
            pl.BlockSpec((1, H_RET, DK_RET, DV_RET), lambda b, c: (b, 0, 0, 0)),
            pl.BlockSpec((1, H_GLA, DK_GLA, DV_GLA), lambda b, c: (b, 0, 0, 0)),
        ],
        out_shape=[
            jax.ShapeDtypeStruct((B * L, D_MIX), BF16),
            jax.ShapeDtypeStruct((B, H_RET, DK_RET, DV_RET), F32),
            jax.ShapeDtypeStruct((B, H_GLA, DK_GLA, DV_GLA), F32),
        ],
        compiler_params=_cparams(("parallel", "arbitrary")),
        name="mixer_prompt",
    )(proj, cosf, sinf, dmask, qdec, kdec, cdec, w2, b2, sums, pm, gnrw, gnrb, gngw)


def _mixer_sample_kernel(p_ref, cos_ref, sin_ref, dmask_ref, qdec_ref, kdec_ref, cdec_ref,
                         w2_ref, b2_ref, sums_ref, pm_ref, gnrw_ref, gnrb_ref, gngw_ref,
                         sret0_ref, sgla0_ref,
                         mix_ref, sret_ref, sgla_ref, *, rows, seq_len, n_levels):
    n_seq = rows // seq_len
    row_seq = lax.broadcasted_iota(jnp.int32, (rows, 1), 0) // seq_len
    col_seq = lax.broadcasted_iota(jnp.int32, (1, rows), 1) // seq_len
    cosf = cos_ref[...]
    sinf = sin_ref[...]
    for h in range(H_RET):
        qb, inner, kt, v = _ret_head(p_ref, h, cosf, sinf, dmask_ref, qdec_ref, kdec_ref)
        cross = jnp.zeros((rows, DV_RET), F32)
        for s in range(n_seq):
            s_old = sret0_ref[s, h]
            c_s = jnp.dot(qb, s_old.astype(BF16), preferred_element_type=F32)
            cross = cross + jnp.where(row_seq == s, c_s, 0.0)
            kt_s = jnp.where(col_seq == s, kt, 0.0).astype(BF16)
            sret_ref[s, h] = s_old * cdec_ref[h] + jnp.dot(kt_s, v, preferred_element_type=F32)
        y = _ret_finish(inner + cross * qdec_ref[h], p_ref, h, gnrw_ref, gnrb_ref)
        mix_ref[:, h * DV_RET:(h + 1) * DV_RET] = y.astype(mix_ref.dtype)

    la = _gla_log_decay(p_ref, w2_ref, b2_ref)
    for h in range(H_GLA):
        qd, inner, kt, v, bcum = _gla_head(p_ref, h, la, sums_ref, pm_ref, rows, n_levels)
        cross = jnp.zeros((rows, DV_GLA), F32)
        for s in range(n_seq):
            s_old = sgla0_ref[s, h]
            c_s = jnp.dot(qd, s_old.astype(BF16), preferred_element_type=F32)
            cross = cross + jnp.where(row_seq == s, c_s, 0.0)
            kt_s = jnp.where(col_seq == s, kt, 0.0).astype(BF16)
            last = (s + 1) * seq_len - 1
            dec = jnp.exp(_rows_to_cols(bcum[last:last + 1, :], DK_GLA))
            dec = jnp.concatenate([dec, dec], axis=1)
            sgla_ref[s, h] = s_old * dec + jnp.dot(kt_s, v, preferred_element_type=F32)
        y = _gla_finish(inner + cross, p_ref, h, gngw_ref)
        off = H_RET * DV_RET + h * DV_GLA
        mix_ref[:, off:off + DV_GLA] = y.astype(mix_ref.dtype)


def _mixer_sample(proj, sret0, sgla0, Bs, Ls, consts):
    n_seq = SAMPLE_GROUP
    rows = n_seq * Ls
    cosf, sinf, dmask, qdec, kdec, cdec, w2, b2, sums, pm, gnrw, gnrb, gngw = consts
    n_levels = pm.shape[0] - 1
    const2 = lambda g: (0, 0)
    const3 = lambda g: (0, 0, 0)
    kern = functools.partial(_mixer_sample_kernel, rows=rows, seq_len=Ls, n_levels=n_levels)
    st_spec = pl.BlockSpec((n_seq, H_RET, DK_RET, DV_RET), lambda g: (g, 0, 0, 0))
    return pl.pallas_call(
        kern,
        grid=(Bs // n_seq,),
        in_specs=[
            pl.BlockSpec((rows, D_IN_PAD), lambda g: (g, 0)),
            pl.BlockSpec(cosf.shape, const2),
            pl.BlockSpec(sinf.shape, const2),
            pl.BlockSpec(dmask.shape, const3),
            pl.BlockSpec(qdec.shape, const3),
            pl.BlockSpec(kdec.shape, const3),
            pl.BlockSpec(cdec.shape, const3),
            pl.BlockSpec(w2.shape, const2),
            pl.BlockSpec(b2.shape, const2),
            pl.BlockSpec(sums.shape, const2),
            pl.BlockSpec(pm.shape, const3),
            pl.BlockSpec(gnrw.shape, const2),
            pl.BlockSpec(gnrb.shape, const2),
            pl.BlockSpec(gngw.shape, const2),
            st_spec,
            st_spec,
        ],
        out_specs=[
            pl.BlockSpec((rows, D_MIX), lambda g: (g, 0)),
            st_spec,
            st_spec,
        ],
        out_shape=[
            jax.ShapeDtypeStruct((Bs * Ls, D_MIX), BF16),
            jax.ShapeDtypeStruct((Bs, H_RET, DK_RET, DV_RET), F32),
            jax.ShapeDtypeStruct((Bs, H_GLA, DK_GLA, DV_GLA), F32),
        ],
        compiler_params=_cparams(("parallel",)),
        name="mixer_sample",
    )(proj, cosf, sinf, dmask, qdec, kdec, cdec, w2, b2, sums, pm, gnrw, gnrb, gngw, sret0, sgla0)


def _out_query_kernel(mix_ref, x_ref, wo_ref, wn_ref, wq_ref, k1_ref, k2_ref,
                      x1_ref, h2_ref, s1_ref, s2_ref):
    x1 = x_ref[...] + jnp.dot(mix_ref[...], wo_ref[...], preferred_element_type=F32)
    x1_ref[...] = x1
    ms = jnp.mean(x1 * x1, axis=-1, keepdims=True)
    h2 = (x1 * lax.rsqrt(ms + EPS) * wn_ref[...]).astype(BF16)
    h2_ref[...] = h2
    q = jnp.dot(h2, wq_ref[...], preferred_element_type=F32).astype(BF16)
    half = PEER_DQ // 2
    for h in range(PEER_HEADS):
        q1 = q[:, h * PEER_DQ:h * PEER_DQ + half]
        q2 = q[:, h * PEER_DQ + half:(h + 1) * PEER_DQ]
        s1_ref[h] = _dot_nt(k1_ref[h], q1)
        s2_ref[h] = _dot_nt(k2_ref[h], q2)


def _out_query(mix, x, wo, wn, wq, k1, k2, tm):
    T, D = x.shape
    resident = dict(pipeline_mode=pl.Buffered(1))
    return pl.pallas_call(
        _out_query_kernel,
        grid=(T // tm,),
        in_specs=[
            pl.BlockSpec((tm, D_MIX), lambda i: (i, 0)),
            pl.BlockSpec((tm, D), lambda i: (i, 0)),
            pl.BlockSpec(wo.shape, lambda i: (0, 0), **resident),
            pl.BlockSpec((1, D), lambda i: (0, 0)),
            pl.BlockSpec(wq.shape, lambda i: (0, 0), **resident),
            pl.BlockSpec(k1.shape, lambda i: (0, 0, 0)),
            pl.BlockSpec(k2.shape, lambda i: (0, 0, 0)),
        ],
        out_specs=[
            pl.BlockSpec((tm, D), lambda i: (i, 0)),
            pl.BlockSpec((tm, D), lambda i: (i, 0)),
            pl.BlockSpec((PEER_HEADS, PEER_NKEYS, tm), lambda i: (0, 0, i)),
            pl.BlockSpec((PEER_HEADS, PEER_NKEYS, tm), lambda i: (0, 0, i)),
        ],
        out_shape=[
            jax.ShapeDtypeStruct((T, D), F32),
            jax.ShapeDtypeStruct((T, D), BF16),
            jax.ShapeDtypeStruct((PEER_HEADS, PEER_NKEYS, T), F32),
            jax.ShapeDtypeStruct((PEER_HEADS, PEER_NKEYS, T), F32),
        ],
        compiler_params=_cparams(("parallel",)),
        name="out_proj_peer_query",
    )(mix, x, wo, wn, wq, k1, k2)


def _candidate_pairs():
    return [(a, b) for a in range(PEER_TOPK) for b in range(PEER_TOPK) if (a + 1) * (b + 1) <= PEER_TOPK]


def _top_values(s, k):
    vals = []
    rank = jnp.full(s.shape, float(k), F32)
    for i in range(k):
        m = jnp.max(s, axis=0, keepdims=True)
        vals.append(m)
        hit = s == m
        rank = jnp.where(hit, float(i), rank)
        s = jnp.where(hit, -jnp.inf, s)
    return vals, rank


def _route_stats_kernel(s1_ref, s2_ref, n1_ref, e1_ref, r2_ref, e2_ref):
    pairs = _candidate_pairs()
    n_cand = -(-len(pairs) // 8) * 8
    tt = s1_ref.shape[-1]
    rows = lax.broadcasted_iota(jnp.int32, (n_cand, tt), 0)
    for h in range(PEER_HEADS):
        s1 = s1_ref[h]
        s2 = s2_ref[h]
        v1, _ = _top_values(s1, PEER_TOPK)
        v2, rank2 = _top_values(s2, PEER_TOPK)
        cand = jnp.full((n_cand, tt), -jnp.inf, F32)
        for r, (a, b) in enumerate(pairs):
            cand = jnp.where(rows == r, v1[a] + v2[b], cand)
        tau = _top_values(cand, PEER_TOPK)[0][-1]
        z = jnp.sum(jnp.where(cand >= tau, jnp.exp(cand - (v1[0] + v2[0])), 0.0), axis=0, keepdims=True)
        n1 = jnp.zeros(s1.shape, F32)
        for b in range(PEER_TOPK):
            n1 = n1 + jnp.where(s1 + v2[b] >= tau, 1.0, 0.0)
        n1_ref[h] = n1
        e1_ref[h] = jnp.exp(s1 - v1[0]) / z
        r2_ref[h] = rank2.astype(ROUTE_DTYPE)
        e2_ref[h] = jnp.exp(s2 - v2[0]).astype(ROUTE_DTYPE)


def _route_stats(s1, s2, tt):
    H, K, T = s1.shape
    blk = pl.BlockSpec((H, K, tt), lambda i: (0, 0, i))
    return pl.pallas_call(
        _route_stats_kernel,
        grid=(T // tt,),
        in_specs=[blk, blk],
        out_specs=[blk, blk, blk, blk],
        out_shape=[
            jax.ShapeDtypeStruct((H, K, T), F32),
            jax.ShapeDtypeStruct((H, K, T), F32),
            jax.ShapeDtypeStruct((H, K, T), ROUTE_DTYPE),
            jax.ShapeDtypeStruct((H, K, T), ROUTE_DTYPE),
        ],
        compiler_params=_cparams(("parallel",)),
        name="peer_route_stats",
    )(s1, s2)


def _peer_step(e, st_new_ref, st_old_ref, h2s_ref, u_ref, vt_ref, n1_ref, e1_ref, r2s_ref, e2s_ref,
               acc_ref, w_ref, *, eb, lane_chunk):
    tt = h2s_ref.shape[1]
    blk = jnp.maximum(e - 1, 0)
    n_rows = eb // PEER_NKEYS
    group = 1
    for r0 in range(0, n_rows, group):
        i1s = [blk * n_rows + r0 + j for j in range(group)]
        n1_rows = [[n1_ref[h, pl.ds(i1, 1), :] for h in range(PEER_HEADS)] for i1 in i1s]
        e1_rows = [[e1_ref[h, pl.ds(i1, 1), :] for h in range(PEER_HEADS)] for i1 in i1s]
        for c0 in range(0, tt, lane_chunk):
            cs = slice(c0, c0 + lane_chunk)
            g = [None] * group
            for h in range(PEER_HEADS):
                r2 = r2s_ref[h, :, cs]
                e2 = e2s_ref[h, :, cs]
                for j in range(group):
                    n1c = jnp.broadcast_to(n1_rows[j][h][:, cs], e2.shape).astype(ROUTE_DTYPE)
                    e1c = jnp.broadcast_to(e1_rows[j][h][:, cs], e2.shape).astype(ROUTE_DTYPE)
                    sel = jnp.where(r2 < n1c, e2, jnp.zeros_like(e2))
                    g[j] = sel * e1c if g[j] is None else g[j] + sel * e1c
            for j in range(group):
                rs = slice((r0 + j) * PEER_NKEYS, (r0 + j + 1) * PEER_NKEYS)
                s = st_old_ref[rs, cs]
                act = 0.5 * s * (1.0 + lax.erf(s * (2.0 ** -0.5)))
                w_ref[rs, cs] = (g[j] * act.astype(ROUTE_DTYPE)).astype(BF16)
    st_new_ref[...] = jnp.dot(u_ref[...].astype(BF16), h2s_ref[...], preferred_element_type=F32)
    acc_ref[...] += jnp.dot(vt_ref[...], w_ref[...], preferred_element_type=F32)


def _peer_kernel(h2_ref, u_ref, vt_ref, n1_ref, e1_ref, r2_ref, e2_ref, x1_ref, wn_ref,
                 y_ref, acc_ref, w_ref, st0_ref, st1_ref, h2s_ref, r2s_ref, e2s_ref, *, eb, lane_chunk):
    e = pl.program_id(1)

    @pl.when(e == 0)
    def _():
        acc_ref[...] = jnp.zeros_like(acc_ref)
        st1_ref[...] = jnp.zeros_like(st1_ref)
        h2s_ref[...] = h2_ref[...].astype(F32).T.astype(BF16)
        r2s_ref[...] = r2_ref[...]
        e2s_ref[...] = e2_ref[...]

    step = functools.partial(_peer_step, e, h2s_ref=h2s_ref, u_ref=u_ref, vt_ref=vt_ref, n1_ref=n1_ref,
                             e1_ref=e1_ref, r2s_ref=r2s_ref, e2s_ref=e2s_ref, acc_ref=acc_ref,
                             w_ref=w_ref, eb=eb, lane_chunk=lane_chunk)

    @pl.when(e % 2 == 0)
    def _():
        step(st0_ref, st1_ref)

    @pl.when(e % 2 == 1)
    def _():
        step(st1_ref, st0_ref)

    @pl.when(e == pl.num_programs(1) - 1)
    def _():
        out = x1_ref[...] + acc_ref[...].T
        ms = jnp.mean(out * out, axis=-1, keepdims=True)
        y_ref[...] = out * lax.rsqrt(ms + EPS) * wn_ref[...]


def _peer(h2, u_bf, vt_bf, n1, e1, r2, e2, x1, wn, tt):
    T, D = x1.shape
    nb, _, eb = vt_bf.shape
    stat = pl.BlockSpec((PEER_HEADS, PEER_NKEYS, tt), lambda i, e: (0, 0, i))
    kern = functools.partial(_peer_kernel, eb=eb, lane_chunk=min(tt, LANES))
    return pl.pallas_call(
        kern,
        grid=(T // tt, nb + 1),
        in_specs=[
            pl.BlockSpec((tt, D), lambda i, e: (i, 0)),
            pl.BlockSpec((eb, D), lambda i, e: (jnp.minimum(e, nb - 1), 0)),
            pl.BlockSpec((None, D, eb), lambda i, e: (jnp.maximum(e - 1, 0), 0, 0)),
            stat, stat, stat, stat,
            pl.BlockSpec((tt, D), lambda i, e: (i, 0), pipeline_mode=pl.Buffered(1)),
            pl.BlockSpec((1, D), lambda i, e: (0, 0)),
        ],
        out_specs=pl.BlockSpec((tt, D), lambda i, e: (i, 0)),
        out_shape=jax.ShapeDtypeStruct((T, D), F32),
        scratch_shapes=[
            pltpu.VMEM((D, tt), F32),
            pltpu.VMEM((eb, tt), BF16),
            pltpu.VMEM((eb, tt), F32),
            pltpu.VMEM((eb, tt), F32),
            pltpu.VMEM((D, tt), BF16),
            pltpu.VMEM((PEER_HEADS, PEER_NKEYS, tt), ROUTE_DTYPE),
            pltpu.VMEM((PEER_HEADS, PEER_NKEYS, tt), ROUTE_DTYPE),
        ],
        compiler_params=_cparams(("parallel", "arbitrary")),
        name="peer_dense",
    )(h2, u_bf, vt_bf, n1, e1, r2, e2, x1, wn)


def _block_transpose_kernel(v_ref, o_ref):
    o_ref[...] = v_ref[...].T.astype(o_ref.dtype)


def _value_blocks(v, eb):
    E, D = v.shape
    return pl.pallas_call(
        _block_transpose_kernel,
        grid=(E // eb,),
        in_specs=[pl.BlockSpec((eb, D), lambda i: (i, 0))],
        out_specs=pl.BlockSpec((None, D, eb), lambda i: (i, 0, 0)),
        out_shape=jax.ShapeDtypeStruct((E // eb, D, eb), BF16),
        compiler_params=_cparams(("parallel",)),
        name="peer_value_blocks",
    )(v)


def _token_tile(T, pref):
    t = min(pref, T)
    while T % t:
        t //= 2
    return t


def _ffn(mix, x2d, wo, wnf, wq, k1, k2, u_bf, vt_bf, wn_final):
    T = x2d.shape[0]
    x1, h2, s1, s2 = _out_query(mix, x2d, wo, wnf, wq, k1, k2, _token_tile(T, 256))
    n1, e1, r2, e2 = _route_stats(s1, s2, _token_tile(T, 256))
    return _peer(h2, u_bf, vt_bf, n1, e1, r2, e2, x1, wn_final, _token_tile(T, PEER_TOKEN_TILE))


def kernel(x_prompt, x_sample, state_ret, state_gla, w_norm_mix, w_in, w_gla_a2, b_gla_a, gn_ret_w, gn_ret_b, gn_gla_w, w_out, w_norm_ffn, w_pq, sub_keys1, sub_keys2, u_tab, v_tab, w_norm_final):
    Bp, Lp, D = x_prompt.shape
    Bs, Ls, _ = x_sample.shape
    depth = w_in.shape[0]
    assert depth == 1 and D == D_MODEL
    l = 0
    xp = x_prompt.reshape(Bp * Lp, D)
    xs = x_sample.reshape(Bs * Ls, D)

    w_in_bf = jnp.pad(w_in[l], ((0, 0), (0, D_IN_PAD - w_in.shape[2]))).astype(BF16)
    w2 = jnp.pad(w_gla_a2[l], ((0, LANES - GLA_RANK), (0, 0))).astype(BF16)
    b2 = b_gla_a[l].reshape(1, -1)
    gnrw = gn_ret_w[l].reshape(1, -1)
    gnrb = gn_ret_b[l].reshape(1, -1)
    gngw = gn_gla_w[l].reshape(1, -1)
    wnm = w_norm_mix[l].reshape(1, D)
    wnf = w_norm_ffn[l].reshape(1, D)
    wn_final = w_norm_final.reshape(1, D)
    wo = w_out[l].astype(BF16)
    wq = w_pq[l].astype(BF16)
    k1 = sub_keys1[l].astype(BF16)
    k2 = sub_keys2[l].astype(BF16)
    u_bf = u_tab[l]
    vt_bf = _value_blocks(v_tab[l], PEER_EXPERT_BLOCK)

    cos_p, sin_p = _rope_tables(jnp.arange(Lp, dtype=F32))
    pos_s = PAST_LEN + jnp.arange(Ls, dtype=F32)
    cos_s, sin_s = _rope_tables(jnp.tile(pos_s, SAMPLE_GROUP))
    sums_p, pm_p = _gla_consts(PROMPT_CHUNK, PROMPT_CHUNK)
    sums_s, pm_s = _gla_consts(SAMPLE_GROUP * Ls, Ls)
    consts_p = (cos_p, sin_p) + _ret_consts(PROMPT_CHUNK, PROMPT_CHUNK) + (
        w2, b2, jnp.asarray(sums_p, BF16), jnp.asarray(pm_p), gnrw, gnrb, gngw)
    consts_s = (cos_s, sin_s) + _ret_consts(SAMPLE_GROUP * Ls, Ls) + (
        w2, b2, jnp.asarray(sums_s, BF16), jnp.asarray(pm_s), gnrw, gnrb, gngw)

    proj_p = _norm_matmul(xp, wnm, w_in_bf, _token_tile(Bp * Lp, IN_PROJ_TOKEN_TILE), IN_PROJ_COL_TILE)
    proj_s = _norm_matmul(xs, wnm, w_in_bf, _token_tile(Bs * Ls, IN_PROJ_TOKEN_TILE), IN_PROJ_COL_TILE)
    mix_p, rp, gp = _mixer_prompt(proj_p, Bp, Lp, consts_p)
    mix_s, rs, gs = _mixer_sample(proj_s, state_ret[l], state_gla[l], Bs, Ls, consts_s)

    yp = _ffn(mix_p, xp, wo, wnf, wq, k1, k2, u_bf, vt_bf, wn_final)
    ys = _ffn(mix_s, xs, wo, wnf, wq, k1, k2, u_bf, vt_bf, wn_final)
    return (yp.reshape(Bp, Lp, D), ys.reshape(Bs, Ls, D), rp[None], gp[None], rs[None], gs[None])
```

```python
import functools
import math

import numpy as np
import jax
import jax.numpy as jnp
from jax import lax
from jax.experimental import pallas as pl
from jax.experimental.pallas import tpu as pltpu

F32 = jnp.float32
BF16 = jnp.bfloat16
ROUTE_DTYPE = F32

D_MODEL = 2048
H_RET, DK_RET, DV_RET = 4, 128, 256
H_GLA, DK_GLA, DV_GLA = 4, 128, 256
GLA_RANK = 16
GLA_TEMP = 16.0
ROPE_BASE = 10000.0
PAST_LEN = 16384
PEER_HEADS, PEER_NKEYS, PEER_DQ, PEER_TOPK = 8, 128, 256, 16
N_EXPERTS = PEER_NKEYS * PEER_NKEYS
EPS = 1e-6

LANES = 128
D_MIX = H_RET * DV_RET + H_GLA * DV_GLA
D_IN_MAIN = 2 * H_RET * DK_RET + 2 * H_RET * DV_RET + 2 * H_GLA * DK_GLA + 2 * H_GLA * DV_GLA
OFF_QR = 0
OFF_KR = OFF_QR + H_RET * DK_RET
OFF_VR = OFF_KR + H_RET * DK_RET
OFF_GR = OFF_VR + H_RET * DV_RET
OFF_QG = OFF_GR + H_RET * DV_RET
OFF_KG = OFF_QG + H_GLA * DK_GLA
OFF_VG = OFF_KG + H_GLA * DK_GLA
OFF_GG = OFF_VG + H_GLA * DV_GLA

IN_PROJ_TOKEN_TILE = 1024
IN_PROJ_COL_TILE = 1536
ROUTE_TOKEN_TILE = 256
PEER_TOKEN_TILE = 512
PEER_EXPERT_BLOCK = 1024
PROMPT_CHUNK = 128
SAMPLE_GROUP = 8
VMEM_LIMIT = 56 * 1024 * 1024


def _cparams(sem):
    return pltpu.CompilerParams(dimension_semantics=sem, vmem_limit_bytes=VMEM_LIMIT)


def _norm_matmul_kernel(x_ref, wn_ref, w_ref, wa_ref, o_ref, oa_ref, h_ref):
    @pl.when(pl.program_id(1) == 0)
    def _():
        x = x_ref[...]
        ms = jnp.mean(x * x, axis=-1, keepdims=True)
        h = (x * lax.rsqrt(ms + EPS) * wn_ref[...]).astype(BF16)
        h_ref[...] = h
        oa_ref[...] = jnp.dot(h, wa_ref[...], preferred_element_type=F32)

    o_ref[...] = jnp.dot(h_ref[...], w_ref[...], preferred_element_type=F32)


def _norm_matmul(x, wn, w, wa, tm, tn):
    T, D = x.shape
    N = w.shape[1]
    Na = wa.shape[1]
    return pl.pallas_call(
        _norm_matmul_kernel,
        grid=(T // tm, N // tn),
        in_specs=[
            pl.BlockSpec((tm, D), lambda i, j: (i, 0)),
            pl.BlockSpec((1, D), lambda i, j: (0, 0)),
            pl.BlockSpec((D, tn), lambda i, j: (0, j)),
            pl.BlockSpec((D, Na), lambda i, j: (0, 0)),
        ],
        out_specs=[
            pl.BlockSpec((tm, tn), lambda i, j: (i, j)),
            pl.BlockSpec((tm, Na), lambda i, j: (i, 0)),
        ],
        out_shape=[jax.ShapeDtypeStruct((T, N), F32), jax.ShapeDtypeStruct((T, Na), F32)],
        scratch_shapes=[pltpu.VMEM((tm, D), BF16)],
        compiler_params=_cparams(("parallel", "arbitrary")),
        name="norm_in_proj",
    )(x, wn, w, wa)


def _gla_consts(rows, blk):
    idx = np.arange(rows)
    seq = idx // blk
    t = idx[None, :]
    i = idx[:, None]
    same = seq[:, None] == seq[None, :]
    mats = [same & (t <= i), same & (t > i)]
    pmasks = []
    s = blk
    while s >= 2:
        bstart = (idx // s) * s
        mid = bstart + s // 2
        upper = idx >= mid
        mats.append(upper[:, None] & (t >= mid[:, None]) & (t <= i))
        mats.append((~upper)[:, None] & (t > i) & (t <= mid[:, None] - 1))
        pmasks.append((bstart[:, None] == bstart[None, :]) & upper[:, None] & (~upper)[None, :])
        s //= 2
    pmasks.append(np.eye(rows, dtype=bool))
    return (np.concatenate(mats, 0).astype(np.float32), np.stack(pmasks).astype(np.float32))


def _ret_consts(rows, blk):
    log_g = jnp.log1p(-jnp.exp2(-5.0 - jnp.arange(H_RET, dtype=F32)))
    idx = np.arange(rows)
    p = jnp.asarray(idx % blk, F32)
    same = jnp.asarray((idx[:, None] // blk) == (idx[None, :] // blk))
    dist = p[:, None] - p[None, :]
    causal = same & (dist >= 0)
    dmask = jnp.where(causal[None], jnp.exp(jnp.maximum(dist, 0.0)[None] * log_g[:, None, None]), 0.0)
    qdec = jnp.exp((p + 1.0)[None, :] * log_g[:, None])
    kdec = jnp.exp((blk - 1.0 - p)[None, :] * log_g[:, None])
    cdec = jnp.exp(blk * log_g)
    qdec_full = jnp.broadcast_to(qdec[:, :, None], (H_RET, rows, DV_RET))
    kdec_full = jnp.broadcast_to(kdec[:, :, None], (H_RET, rows, DK_RET))
    cdec_full = jnp.broadcast_to(cdec[:, None, None], (H_RET, DK_RET, DV_RET))
    return dmask.astype(F32), qdec_full.astype(F32), kdec_full.astype(F32), cdec_full.astype(F32)


def _rope_tables(pos):
    d = DK_RET
    freqs = ROPE_BASE ** (-jnp.arange(0, d, 2, dtype=F32) / d)
    ang = pos[:, None] * freqs[None, :]
    cos = jnp.cos(ang)
    sin = jnp.sin(ang)
    return jnp.concatenate([cos, cos], axis=-1), jnp.concatenate([-sin, sin], axis=-1)


def _dot_nt(a, b):
    return lax.dot_general(a, b, (((1,), (1,)), ((), ())), preferred_element_type=F32)


def _rotary(x, cosf, sinf):
    return x * cosf + pltpu.roll(x, DK_RET // 2, axis=1) * sinf


def _sigmoid(x):
    return 1.0 / (1.0 + jnp.exp(-x))


def _ret_head(p_ref, h, cosf, sinf, dmask_ref, qdec_ref, kdec_ref):
    q = _rotary(p_ref[:, OFF_QR + h * DK_RET:OFF_QR + (h + 1) * DK_RET], cosf, sinf)
    k = _rotary(p_ref[:, OFF_KR + h * DK_RET:OFF_KR + (h + 1) * DK_RET], cosf, sinf) * (DK_RET ** -0.5)
    v = p_ref[:, OFF_VR + h * DV_RET:OFF_VR + (h + 1) * DV_RET].astype(BF16)
    qb = q.astype(BF16)
    scores = _dot_nt(qb, k.astype(BF16)) * dmask_ref[h]
    inner = jnp.dot(scores.astype(BF16), v, preferred_element_type=F32)
    kt = (k * kdec_ref[h]).T
    return qb, inner, kt, v


def _ret_finish(o, p_ref, h, gnw_ref, gnb_ref):
    mu = jnp.mean(o, axis=-1, keepdims=True)
    d = o - mu
    var = jnp.mean(d * d, axis=-1, keepdims=True)
    sl = slice(h * DV_RET, (h + 1) * DV_RET)
    y = d * lax.rsqrt(var + EPS) * gnw_ref[:, sl] + gnb_ref[:, sl]
    g = p_ref[:, OFF_GR + h * DV_RET:OFF_GR + (h + 1) * DV_RET]
    return y * (g * _sigmoid(g))


def _gla_log_decay(a_ref, w2_ref, b2_ref):
    a = a_ref[...].astype(BF16)
    z = jnp.dot(a, w2_ref[...], preferred_element_type=F32) + b2_ref[...]
    return (jnp.minimum(z, 0.0) - jnp.log1p(jnp.exp(-jnp.abs(z)))) * (1.0 / GLA_TEMP)


def _gla_head(p_ref, h, la, sums_ref, pm_ref, rows, n_levels):
    la_h = la[:, h * DK_GLA:(h + 1) * DK_GLA]
    hi = la_h.astype(BF16)
    lo = (la_h - hi.astype(F32)).astype(BF16)
    e2 = jnp.dot(sums_ref[...], jnp.concatenate([hi, lo], axis=1), preferred_element_type=F32)
    ex = e2[:, :DK_GLA] + e2[:, DK_GLA:]
    bcum = ex[0:rows]
    brest = ex[rows:2 * rows]
    q = p_ref[:, OFF_QG + h * DK_GLA:OFF_QG + (h + 1) * DK_GLA] * (DK_GLA ** -0.5)
    k = p_ref[:, OFF_KG + h * DK_GLA:OFF_KG + (h + 1) * DK_GLA]
    v = p_ref[:, OFF_VG + h * DV_GLA:OFF_VG + (h + 1) * DV_GLA].astype(BF16)
    amat = _dot_nt(q.astype(BF16), k.astype(BF16)) * pm_ref[n_levels]
    for l in range(n_levels):
        up = ex[(2 + 2 * l) * rows:(3 + 2 * l) * rows]
        lw = ex[(3 + 2 * l) * rows:(4 + 2 * l) * rows]
        ql = (q * jnp.exp(up)).astype(BF16)
        kl = (k * jnp.exp(lw)).astype(BF16)
        amat = amat + _dot_nt(ql, kl) * pm_ref[l]
    inner = jnp.dot(amat.astype(BF16), v, preferred_element_type=F32)
    qd = (q * jnp.exp(bcum)).astype(BF16)
    kt = (k * jnp.exp(brest)).T
    return qd, inner, kt, v, bcum


def _gla_finish(o, p_ref, h, gw_ref):
    sl = slice(h * DV_GLA, (h + 1) * DV_GLA)
    y = o * lax.rsqrt(jnp.mean(o * o, axis=-1, keepdims=True) + EPS) * gw_ref[:, sl]
    g = p_ref[:, OFF_GG + h * DV_GLA:OFF_GG + (h + 1) * DV_GLA]
    return y * (g * _sigmoid(g))


def _rows_to_cols(row, n):
    return jnp.broadcast_to(row, (n, n)).T


def _mixer_prompt_kernel(p_ref, a_ref, cos_ref, sin_ref, dmask_ref, qdec_ref, kdec_ref, cdec_ref,
                         w2_ref, b2_ref, sums_ref, pm_ref, gnrw_ref, gnrb_ref, gngw_ref,
                         mix_ref, sret_ref, sgla_ref, *, rows, n_levels):
    @pl.when(pl.program_id(1) == 0)
    def _():
        sret_ref[...] = jnp.zeros_like(sret_ref)
        sgla_ref[...] = jnp.zeros_like(sgla_ref)

    cosf = cos_ref[...]
    sinf = sin_ref[...]
    for h in range(H_RET):
        qb, inner, kt, v = _ret_head(p_ref, h, cosf, sinf, dmask_ref, qdec_ref, kdec_ref)
        s_old = sret_ref[0, h]
        cross = jnp.dot(qb, s_old.astype(BF16), preferred_element_type=F32) * qdec_ref[h]
        sret_ref[0, h] = s_old * cdec_ref[h] + jnp.dot(kt.astype(BF16), v, preferred_element_type=F32)
        y = _ret_finish(inner + cross, p_ref, h, gnrw_ref, gnrb_ref)
        mix_ref[:, h * DV_RET:(h + 1) * DV_RET] = y.astype(mix_ref.dtype)

    la = _gla_log_decay(a_ref, w2_ref, b2_ref)
    for h in range(H_GLA):
        qd, inner, kt, v, bcum = _gla_head(p_ref, h, la, sums_ref, pm_ref, rows, n_levels)
        s_old = sgla_ref[0, h]
        cross = jnp.dot(qd, s_old.astype(BF16), preferred_element_type=F32)
        dec = jnp.exp(_rows_to_cols(bcum[rows - 1:rows, :], DK_GLA))
        dec = jnp.concatenate([dec, dec], axis=1)
        sgla_ref[0, h] = s_old * dec + jnp.dot(kt.astype(BF16), v, preferred_element_type=F32)
        y = _gla_finish(inner + cross, p_ref, h, gngw_ref)
        off = H_RET * DV_RET + h * DV_GLA
        mix_ref[:, off:off + DV_GLA] = y.astype(mix_ref.dtype)


def _mixer_prompt(proj, a_lr, B, L, consts):
    rows = PROMPT_CHUNK
    nc = L // rows
    cosf, sinf, dmask, qdec, kdec, cdec, w2, b2, sums, pm, gnrw, gnrb, gngw = consts
    n_levels = pm.shape[0] - 1
    const2 = lambda b, c: (0, 0)
    const3 = lambda b, c: (0, 0, 0)
    kern = functools.partial(_mixer_prompt_kernel, rows=rows, n_levels=n_levels)
    return pl.pallas_call(
        kern,
        grid=(B, nc),
        in_specs=[
            pl.BlockSpec((rows, D_IN_MAIN), lambda b, c: (b * nc + c, 0)),
            pl.BlockSpec((rows, LANES), lambda b, c: (b * nc + c, 0)),
            pl.BlockSpec((rows, DK_RET), lambda b, c: (c, 0)),
            pl.BlockSpec((rows, DK_RET), lambda b, c: (c, 0)),
            pl.BlockSpec(dmask.shape, const3),
            pl.BlockSpec(qdec.shape, const3),
            pl.BlockSpec(kdec.shape, const3),
            pl.BlockSpec(cdec.shape, const3),
            pl.BlockSpec(w2.shape, const2),
            pl.BlockSpec(b2.shape, const2),
            pl.BlockSpec(sums.shape, const2),
            pl.BlockSpec(pm.shape, const3),
            pl.BlockSpec(gnrw.shape, const2),
            pl.BlockSpec(gnrb.shape, const2),
            pl.BlockSpec(gngw.shape, const2),
        ],
        out_specs=[
            pl.BlockSpec((rows, D_MIX), lambda b, c: (b * nc + c, 0)),
            pl.BlockSpec((1, H_RET, DK_RET, DV_RET), lambda b, c: (b, 0, 0, 0)),
            pl.BlockSpec((1, H_GLA, DK_GLA, DV_GLA), lambda b, c: (b, 0, 0, 0)),
        ],
        out_shape=[
            jax.ShapeDtypeStruct((B * L, D_MIX), BF16),
            jax.ShapeDtypeStruct((B, H_RET, DK_RET, DV_RET), F32),
            jax.ShapeDtypeStruct((B, H_GLA, DK_GLA, DV_GLA), F32),
        ],
        compiler_params=_cparams(("parallel", "arbitrary")),
        name="mixer_prompt",
    )(proj, a_lr, cosf, sinf, dmask, qdec, kdec, cdec, w2, b2, sums, pm, gnrw, gnrb, gngw)


def _mixer_sample_kernel(p_ref, a_ref, cos_ref, sin_ref, dmask_ref, qdec_ref, kdec_ref, cdec_ref,
                         w2_ref, b2_ref, sums_ref, pm_ref, gnrw_ref, gnrb_ref, gngw_ref,
                         sret0_ref, sgla0_ref,
                         mix_ref, sret_ref, sgla_ref, *, rows, seq_len, n_levels):
    n_seq = rows // seq_len
    row_seq = lax.broadcasted_iota(jnp.int32, (rows, 1), 0) // seq_len
    col_seq = lax.broadcasted_iota(jnp.int32, (1, rows), 1) // seq_len
    cosf = cos_ref[...]
    sinf = sin_ref[...]
    for h in range(H_RET):
        qb, inner, kt, v = _ret_head(p_ref, h, cosf, sinf, dmask_ref, qdec_ref, kdec_ref)
        cross = jnp.zeros((rows, DV_RET), F32)
        for s in range(n_seq):
            s_old = sret0_ref[s, h]
            c_s = jnp.dot(qb, s_old.astype(BF16), preferred_element_type=F32)
            cross = cross + jnp.where(row_seq == s, c_s, 0.0)
            kt_s = jnp.where(col_seq == s, kt, 0.0).astype(BF16)
            sret_ref[s, h] = s_old * cdec_ref[h] + jnp.dot(kt_s, v, preferred_element_type=F32)
        y = _ret_finish(inner + cross * qdec_ref[h], p_ref, h, gnrw_ref, gnrb_ref)
        mix_ref[:, h * DV_RET:(h + 1) * DV_RET] = y.astype(mix_ref.dtype)

    la = _gla_log_decay(a_ref, w2_ref, b2_ref)
    for h in range(H_GLA):
        qd, inner, kt, v, bcum = _gla_head(p_ref, h, la, sums_ref, pm_ref, rows, n_levels)
        cross = jnp.zeros((rows, DV_GLA), F32)
        for s in range(n_seq):
            s_old = sgla0_ref[s, h]
            c_s = jnp.dot(qd, s_old.astype(BF16), preferred_element_type=F32)
            cross = cross + jnp.where(row_seq == s, c_s, 0.0)
            kt_s = jnp.where(col_seq == s, kt, 0.0).astype(BF16)
            last = (s + 1) * seq_len - 1
            dec = jnp.exp(_rows_to_cols(bcum[last:last + 1, :], DK_GLA))
            dec = jnp.concatenate([dec, dec], axis=1)
            sgla_ref[s, h] = s_old * dec + jnp.dot(kt_s, v, preferred_element_type=F32)
        y = _gla_finish(inner + cross, p_ref, h, gngw_ref)
        off = H_RET * DV_RET + h * DV_GLA
        mix_ref[:, off:off + DV_GLA] = y.astype(mix_ref.dtype)


def _mixer_sample(proj, a_lr, sret0, sgla0, Bs, Ls, consts):
    n_seq = SAMPLE_GROUP
    rows = n_seq * Ls
    cosf, sinf, dmask, qdec, kdec, cdec, w2, b2, sums, pm, gnrw, gnrb, gngw = consts
    n_levels = pm.shape[0] - 1
    const2 = lambda g: (0, 0)
    const3 = lambda g: (0, 0, 0)
    kern = functools.partial(_mixer_sample_kernel, rows=rows, seq_len=Ls, n_levels=n_levels)
    st_spec = pl.BlockSpec((None, n_seq, H_RET, DK_RET, DV_RET), lambda g: (0, g, 0, 0, 0))
    return pl.pallas_call(
        kern,
        grid=(Bs // n_seq,),
        in_specs=[
            pl.BlockSpec((rows, D_IN_MAIN), lambda g: (g, 0)),
            pl.BlockSpec((rows, LANES), lambda g: (g, 0)),
            pl.BlockSpec(cosf.shape, const2),
            pl.BlockSpec(sinf.shape, const2),
            pl.BlockSpec(dmask.shape, const3),
            pl.BlockSpec(qdec.shape, const3),
            pl.BlockSpec(kdec.shape, const3),
            pl.BlockSpec(cdec.shape, const3),
            pl.BlockSpec(w2.shape, const2),
            pl.BlockSpec(b2.shape, const2),
            pl.BlockSpec(sums.shape, const2),
            pl.BlockSpec(pm.shape, const3),
            pl.BlockSpec(gnrw.shape, const2),
            pl.BlockSpec(gnrb.shape, const2),
            pl.BlockSpec(gngw.shape, const2),
            st_spec,
            st_spec,
        ],
        out_specs=[
            pl.BlockSpec((rows, D_MIX), lambda g: (g, 0)),
            st_spec,
            st_spec,
        ],
        out_shape=[
            jax.ShapeDtypeStruct((Bs * Ls, D_MIX), BF16),
            jax.ShapeDtypeStruct((1, Bs, H_RET, DK_RET, DV_RET), F32),
            jax.ShapeDtypeStruct((1, Bs, H_GLA, DK_GLA, DV_GLA), F32),
        ],
        compiler_params=_cparams(("parallel",)),
        name="mixer_sample",
    )(proj, a_lr, cosf, sinf, dmask, qdec, kdec, cdec, w2, b2, sums, pm, gnrw, gnrb, gngw, sret0, sgla0)


def _out_query_kernel(mix_ref, x_ref, wo_ref, wn_ref, wq_ref, k1_ref, k2_ref,
                      x1_ref, h2_ref, s1_ref, s2_ref):
    x1 = x_ref[...] + jnp.dot(mix_ref[...], wo_ref[...], preferred_element_type=F32)
    x1_ref[...] = x1
    ms = jnp.mean(x1 * x1, axis=-1, keepdims=True)
    h2 = (x1 * lax.rsqrt(ms + EPS) * wn_ref[...]).astype(BF16)
    h2_ref[...] = h2
    q = jnp.dot(h2, wq_ref[...], preferred_element_type=F32).astype(BF16)
    half = PEER_DQ // 2
    for h in range(PEER_HEADS):
        q1 = q[:, h * PEER_DQ:h * PEER_DQ + half]
        q2 = q[:, h * PEER_DQ + half:(h + 1) * PEER_DQ]
        s1_ref[h] = _dot_nt(k1_ref[h], q1)
        s2_ref[h] = _dot_nt(k2_ref[h], q2)


def _out_query(mix, x, wo, wn, wq, k1, k2, tm):
    T, D = x.shape
    resident = dict(pipeline_mode=pl.Buffered(1))
    return pl.pallas_call(
        _out_query_kernel,
        grid=(T // tm,),
        in_specs=[
            pl.BlockSpec((tm, D_MIX), lambda i: (i, 0)),
            pl.BlockSpec((tm, D), lambda i: (i, 0)),
            pl.BlockSpec(wo.shape, lambda i: (0, 0), **resident),
            pl.BlockSpec((1, D), lambda i: (0, 0)),
            pl.BlockSpec(wq.shape, lambda i: (0, 0), **resident),
            pl.BlockSpec(k1.shape, lambda i: (0, 0, 0)),
            pl.BlockSpec(k2.shape, lambda i: (0, 0, 0)),
        ],
        out_specs=[
            pl.BlockSpec((tm, D), lambda i: (i, 0)),
            pl.BlockSpec((tm, D), lambda i: (i, 0)),
            pl.BlockSpec((PEER_HEADS, PEER_NKEYS, tm), lambda i: (0, 0, i)),
            pl.BlockSpec((PEER_HEADS, PEER_NKEYS, tm), lambda i: (0, 0, i)),
        ],
        out_shape=[
            jax.ShapeDtypeStruct((T, D), F32),
            jax.ShapeDtypeStruct((T, D), BF16),
            jax.ShapeDtypeStruct((PEER_HEADS, PEER_NKEYS, T), F32),
            jax.ShapeDtypeStruct((PEER_HEADS, PEER_NKEYS, T), F32),
        ],
        compiler_params=_cparams(("parallel",)),
        name="out_proj_peer_query",
    )(mix, x, wo, wn, wq, k1, k2)


def _candidate_pairs():
    return [(a, b) for a in range(PEER_TOPK) for b in range(PEER_TOPK) if (a + 1) * (b + 1) <= PEER_TOPK]


def _top_values(s, k):
    vals = []
    rank = jnp.full(s.shape, float(k), F32)
    for i in range(k):
        m = jnp.max(s, axis=0, keepdims=True)
        vals.append(m)
        hit = s == m
        rank = jnp.where(hit, float(i), rank)
        s = jnp.where(hit, -jnp.inf, s)
    return vals, rank


def _route_weights_kernel(s1_ref, s2_ref, g_ref, n1_ref, e1_ref, r2_ref, e2_ref, *, lane_chunk):
    pairs = _candidate_pairs()
    n_cand = -(-len(pairs) // 8) * 8
    tt = s1_ref.shape[-1]
    rows = lax.broadcasted_iota(jnp.int32, (n_cand, tt), 0)
    for h in range(PEER_HEADS):
        s1 = s1_ref[h]
        s2 = s2_ref[h]
        v1, _ = _top_values(s1, PEER_TOPK)
        v2, rank2 = _top_values(s2, PEER_TOPK)
        cand = jnp.full((n_cand, tt), -jnp.inf, F32)
        for r, (a, b) in enumerate(pairs):
            cand = jnp.where(rows == r, v1[a] + v2[b], cand)
        tau = _top_values(cand, PEER_TOPK)[0][-1]
        z = jnp.sum(jnp.where(cand >= tau, jnp.exp(cand - (v1[0] + v2[0])), 0.0), axis=0, keepdims=True)
        n1 = jnp.zeros(s1.shape, F32)
        for b in range(PEER_TOPK):
            n1 = n1 + jnp.where(s1 + v2[b] >= tau, 1.0, 0.0)
        n1_ref[h] = n1
        e1_ref[h] = jnp.exp(s1 - v1[0]) / z
        r2_ref[h] = rank2.astype(ROUTE_DTYPE)
        e2_ref[h] = jnp.exp(s2 - v2[0]).astype(ROUTE_DTYPE)

    def row_block(i1, carry):
        n1_rows = [n1_ref[h, pl.ds(i1, 1), :] for h in range(PEER_HEADS)]
        e1_rows = [e1_ref[h, pl.ds(i1, 1), :] for h in range(PEER_HEADS)]
        row0 = pl.multiple_of(i1 * PEER_NKEYS, PEER_NKEYS)
        for c0 in range(0, tt, lane_chunk):
            cs = slice(c0, c0 + lane_chunk)
            g = None
            for h in range(PEER_HEADS):
                r2 = r2_ref[h, :, cs]
                e2 = e2_ref[h, :, cs]
                n1c = jnp.broadcast_to(n1_rows[h][:, cs], e2.shape).astype(ROUTE_DTYPE)
                e1c = jnp.broadcast_to(e1_rows[h][:, cs], e2.shape).astype(ROUTE_DTYPE)
                term = jnp.where(r2 < n1c, e2, jnp.zeros_like(e2)) * e1c
                g = term if g is None else g + term
            g_ref[pl.ds(row0, PEER_NKEYS), cs] = g.astype(g_ref.dtype)
        return carry

    lax.fori_loop(0, PEER_NKEYS, row_block, 0)


def _route_weights(s1, s2, tt):
    H, K, T = s1.shape
    blk = pl.BlockSpec((H, K, tt), lambda i: (0, 0, i))
    kern = functools.partial(_route_weights_kernel, lane_chunk=min(tt, LANES))
    return pl.pallas_call(
        kern,
        grid=(T // tt,),
        in_specs=[blk, blk],
        out_specs=pl.BlockSpec((None, N_EXPERTS, tt), lambda i: (i, 0, 0)),
        out_shape=jax.ShapeDtypeStruct((T // tt, N_EXPERTS, tt), BF16),
        scratch_shapes=[
            pltpu.VMEM((H, K, tt), F32),
            pltpu.VMEM((H, K, tt), F32),
            pltpu.VMEM((H, K, tt), ROUTE_DTYPE),
            pltpu.VMEM((H, K, tt), ROUTE_DTYPE),
        ],
        compiler_params=_cparams(("parallel",)),
        name="peer_route_weights",
    )(s1, s2)


def _peer_kernel(h2_ref, u_ref, vt_ref, g_ref, x1_ref, wn_ref, y_ref, acc_ref, w_ref, h2s_ref):
    e = pl.program_id(1)

    @pl.when(e == 0)
    def _():
        acc_ref[...] = jnp.zeros_like(acc_ref)
        h2s_ref[...] = h2_ref[...].astype(F32).T.astype(BF16)

    n_sub, _, sub = g_ref.shape
    for t in range(n_sub):
        cols = slice(t * sub, (t + 1) * sub)
        s = jnp.dot(u_ref[...].astype(BF16), h2s_ref[:, cols], preferred_element_type=F32)
        act = 0.5 * s * (1.0 + lax.erf(s * (2.0 ** -0.5)))
        w_ref[:, cols] = g_ref[t] * act.astype(BF16)
    acc_ref[...] += jnp.dot(vt_ref[...], w_ref[...], preferred_element_type=F32)

    @pl.when(e == pl.num_programs(1) - 1)
    def _():
        out = x1_ref[...] + acc_ref[...].T
        ms = jnp.mean(out * out, axis=-1, keepdims=True)
        y_ref[...] = out * lax.rsqrt(ms + EPS) * wn_ref[...]


def _peer(h2, u, vt_bf, g, x1, wn, tt):
    T, D = x1.shape
    nb, _, eb = vt_bf.shape
    sub = g.shape[2]
    return pl.pallas_call(
        _peer_kernel,
        grid=(T // tt, nb),
        in_specs=[
            pl.BlockSpec((tt, D), lambda i, e: (i, 0)),
            pl.BlockSpec((eb, D), lambda i, e: (e, 0)),
            pl.BlockSpec((None, D, eb), lambda i, e: (e, 0, 0)),
            pl.BlockSpec((tt // sub, eb, sub), lambda i, e: (i, e, 0)),
            pl.BlockSpec((tt, D), lambda i, e: (i, 0), pipeline_mode=pl.Buffered(1)),
            pl.BlockSpec((1, D), lambda i, e: (0, 0)),
        ],
        out_specs=pl.BlockSpec((tt, D), lambda i, e: (i, 0), pipeline_mode=pl.Buffered(1)),
        out_shape=jax.ShapeDtypeStruct((T, D), F32),
        scratch_shapes=[
            pltpu.VMEM((D, tt), F32),
            pltpu.VMEM((eb, tt), BF16),
            pltpu.VMEM((D, tt), BF16),
        ],
        compiler_params=_cparams(("parallel", "arbitrary")),
        name="peer_dense",
    )(h2, u, vt_bf, g, x1, wn)


def _block_transpose_kernel(v_ref, o_ref):
    o_ref[...] = v_ref[...].T.astype(o_ref.dtype)


def _value_blocks(v, eb):
    E, D = v.shape
    return pl.pallas_call(
        _block_transpose_kernel,
        grid=(E // eb,),
        in_specs=[pl.BlockSpec((eb, D), lambda i: (i, 0))],
        out_specs=pl.BlockSpec((None, D, eb), lambda i: (i, 0, 0)),
        out_shape=jax.ShapeDtypeStruct((E // eb, D, eb), BF16),
        compiler_params=_cparams(("parallel",)),
        name="peer_value_blocks",
    )(v)


def _token_tile(T, pref):
    t = min(pref, T)
    while T % t:
        t //= 2
    return t


def _ffn(mix, x2d, wo, wnf, wq, k1, k2, u, vt_bf, wn_final):
    T = x2d.shape[0]
    x1, h2, s1, s2 = _out_query(mix, x2d, wo, wnf, wq, k1, k2, _token_tile(T, ROUTE_TOKEN_TILE))
    g = _route_weights(s1, s2, _token_tile(T, ROUTE_TOKEN_TILE))
    return _peer(h2, u, vt_bf, g, x1, wn_final, _token_tile(T, PEER_TOKEN_TILE))


def kernel(x_prompt, x_sample, state_ret, state_gla, w_norm_mix, w_in, w_gla_a2, b_gla_a, gn_ret_w, gn_ret_b, gn_gla_w, w_out, w_norm_ffn, w_pq, sub_keys1, sub_keys2, u_tab, v_tab, w_norm_final):
    Bp, Lp, D = x_prompt.shape
    Bs, Ls, _ = x_sample.shape
    depth = w_in.shape[0]
    assert depth == 1 and D == D_MODEL
    l = 0
    xp = x_prompt.reshape(Bp * Lp, D)
    xs = x_sample.reshape(Bs * Ls, D)

    w_main = w_in[l, :, :D_IN_MAIN].astype(BF16)
    w_gate = jnp.pad(w_in[l, :, D_IN_MAIN:], ((0, 0), (0, LANES - GLA_RANK))).astype(BF16)
    w2 = jnp.pad(w_gla_a2[l], ((0, LANES - GLA_RANK), (0, 0))).astype(BF16)
    b2 = b_gla_a[l].reshape(1, -1)
    gnrw = gn_ret_w[l].reshape(1, -1)
    gnrb = gn_ret_b[l].reshape(1, -1)
    gngw = gn_gla_w[l].reshape(1, -1)
    wnm = w_norm_mix[l].reshape(1, D)
    wnf = w_norm_ffn[l].reshape(1, D)
    wn_final = w_norm_final.reshape(1, D)
    wo = w_out[l].astype(BF16)
    wq = w_pq[l].astype(BF16)
    k1 = sub_keys1[l].astype(BF16)
    k2 = sub_keys2[l].astype(BF16)
    u = u_tab[l]
    vt_bf = _value_blocks(v_tab[l], PEER_EXPERT_BLOCK)

    cos_p, sin_p = _rope_tables(jnp.arange(Lp, dtype=F32))
    pos_s = PAST_LEN + jnp.arange(Ls, dtype=F32)
    cos_s, sin_s = _rope_tables(jnp.tile(pos_s, SAMPLE_GROUP))
    sums_p, pm_p = _gla_consts(PROMPT_CHUNK, PROMPT_CHUNK)
    sums_s, pm_s = _gla_consts(SAMPLE_GROUP * Ls, Ls)
    consts_p = (cos_p, sin_p) + _ret_consts(PROMPT_CHUNK, PROMPT_CHUNK) + (
        w2, b2, jnp.asarray(sums_p, BF16), jnp.asarray(pm_p), gnrw, gnrb, gngw)
    consts_s = (cos_s, sin_s) + _ret_consts(SAMPLE_GROUP * Ls, Ls) + (
        w2, b2, jnp.asarray(sums_s, BF16), jnp.asarray(pm_s), gnrw, gnrb, gngw)

    proj_p, a_p = _norm_matmul(xp, wnm, w_main, w_gate, _token_tile(Bp * Lp, IN_PROJ_TOKEN_TILE), IN_PROJ_COL_TILE)
    proj_s, a_s = _norm_matmul(xs, wnm, w_main, w_gate, _token_tile(Bs * Ls, IN_PROJ_TOKEN_TILE), IN_PROJ_COL_TILE)
    mix_p, rp, gp = _mixer_prompt(proj_p, a_p, Bp, Lp, consts_p)
    mix_s, rs, gs = _mixer_sample(proj_s, a_s, state_ret, state_gla, Bs, Ls, consts_s)

    yp = _ffn(mix_p, xp, wo, wnf, wq, k1, k2, u, vt_bf, wn_final)
    ys = _ffn(mix_s, xs, wo, wnf, wq, k1, k2, u, vt_bf, wn_final)
    return (yp.reshape(Bp, Lp, D), ys.reshape(Bs, Ls, D), rp[None], gp[None], rs, gs)
```

```python
import functools
import math

import numpy as np
import jax
import jax.numpy as jnp
from jax import lax
from jax.experimental import pallas as pl
from jax.experimental.pallas import tpu as pltpu

F32 = jnp.float32
BF16 = jnp.bfloat16
ROUTE_DTYPE = F32

D_MODEL = 2048
H_RET, DK_RET, DV_RET = 4, 128, 256
H_GLA, DK_GLA, DV_GLA = 4, 128, 256
GLA_RANK = 16
GLA_TEMP = 16.0
ROPE_BASE = 10000.0
PAST_LEN = 16384
PEER_HEADS, PEER_NKEYS, PEER_DQ, PEER_TOPK = 8, 128, 256, 16
N_EXPERTS = PEER_NKEYS * PEER_NKEYS
EPS = 1e-6

LANES = 128
D_MIX = H_RET * DV_RET + H_GLA * DV_GLA
D_IN_MAIN = 2 * H_RET * DK_RET + 2 * H_RET * DV_RET + 2 * H_GLA * DK_GLA + 2 * H_GLA * DV_GLA
OFF_QR = 0
OFF_KR = OFF_QR + H_RET * DK_RET
OFF_VR = OFF_KR + H_RET * DK_RET
OFF_GR = OFF_VR + H_RET * DV_RET
OFF_QG = OFF_GR + H_RET * DV_RET
OFF_KG = OFF_QG + H_GLA * DK_GLA
OFF_VG = OFF_KG + H_GLA * DK_GLA
OFF_GG = OFF_VG + H_GLA * DV_GLA

IN_PROJ_TOKEN_TILE = 1024
IN_PROJ_COL_TILE = 1536
ROUTE_TOKEN_TILE = 256
PEER_TOKEN_TILE = 512
PEER_EXPERT_BLOCK = 1024
PROMPT_CHUNK = 128
SAMPLE_GROUP = 8
VMEM_LIMIT = 56 * 1024 * 1024


def _cparams(sem):
    return pltpu.CompilerParams(dimension_semantics=sem, vmem_limit_bytes=VMEM_LIMIT)


def _norm_matmul_kernel(x_ref, wn_ref, w_ref, wa_ref, o_ref, oa_ref, h_ref):
    @pl.when(pl.program_id(1) == 0)
    def _():
        x = x_ref[...]
        ms = jnp.mean(x * x, axis=-1, keepdims=True)
        h = (x * lax.rsqrt(ms + EPS) * wn_ref[...]).astype(BF16)
        h_ref[...] = h
        oa_ref[...] = jnp.dot(h, wa_ref[...], preferred_element_type=F32)

    o_ref[...] = jnp.dot(h_ref[...], w_ref[...], preferred_element_type=F32)


def _norm_matmul(x, wn, w, wa, tm, tn):
    T, D = x.shape
    N = w.shape[1]
    Na = wa.shape[1]
    return pl.pallas_call(
        _norm_matmul_kernel,
        grid=(T // tm, N // tn),
        in_specs=[
            pl.BlockSpec((tm, D), lambda i, j: (i, 0)),
            pl.BlockSpec((1, D), lambda i, j: (0, 0)),
            pl.BlockSpec((D, tn), lambda i, j: (0, j)),
            pl.BlockSpec((D, Na), lambda i, j: (0, 0)),
        ],
        out_specs=[
            pl.BlockSpec((tm, tn), lambda i, j: (i, j)),
            pl.BlockSpec((tm, Na), lambda i, j: (i, 0)),
        ],
        out_shape=[jax.ShapeDtypeStruct((T, N), F32), jax.ShapeDtypeStruct((T, Na), F32)],
        scratch_shapes=[pltpu.VMEM((tm, D), BF16)],
        compiler_params=_cparams(("parallel", "arbitrary")),
        name="norm_in_proj",
    )(x, wn, w, wa)


def _gla_consts(rows, blk):
    idx = np.arange(rows)
    seq = idx // blk
    t = idx[None, :]
    i = idx[:, None]
    same = seq[:, None] == seq[None, :]
    mats = [same & (t <= i), same & (t > i)]
    pmasks = []
    s = blk
    while s >= 2:
        bstart = (idx // s) * s
        mid = bstart + s // 2
        upper = idx >= mid
        mats.append((upper[:, None] & (t >= mid[:, None]) & (t <= i))
                    | ((~upper)[:, None] & (t > i) & (t <= mid[:, None] - 1)))
        pmasks.append((bstart[:, None] == bstart[None, :]) & upper[:, None] & (~upper)[None, :])
        s //= 2
    pmasks.append(np.eye(rows, dtype=bool))
    return (np.concatenate(mats, 0).astype(np.float32), np.stack(pmasks).astype(np.float32))


def _ret_consts(rows, blk):
    log_g = jnp.log1p(-jnp.exp2(-5.0 - jnp.arange(H_RET, dtype=F32)))
    idx = np.arange(rows)
    p = jnp.asarray(idx % blk, F32)
    same = jnp.asarray((idx[:, None] // blk) == (idx[None, :] // blk))
    dist = p[:, None] - p[None, :]
    causal = same & (dist >= 0)
    dmask = jnp.where(causal[None], jnp.exp(jnp.maximum(dist, 0.0)[None] * log_g[:, None, None]), 0.0)
    qdec = jnp.exp((p + 1.0)[None, :] * log_g[:, None])
    kdec = jnp.exp((blk - 1.0 - p)[None, :] * log_g[:, None])
    cdec = jnp.exp(blk * log_g)
    qdec_full = jnp.broadcast_to(qdec[:, :, None], (H_RET, rows, DV_RET))
    kdec_full = jnp.broadcast_to(kdec[:, :, None], (H_RET, rows, DK_RET))
    cdec_full = jnp.broadcast_to(cdec[:, None, None], (H_RET, DK_RET, DV_RET))
    return dmask.astype(F32), qdec_full.astype(F32), kdec_full.astype(F32), cdec_full.astype(F32)


def _rope_tables(pos):
    d = DK_RET
    freqs = ROPE_BASE ** (-jnp.arange(0, d, 2, dtype=F32) / d)
    ang = pos[:, None] * freqs[None, :]
    cos = jnp.cos(ang)
    sin = jnp.sin(ang)
    return jnp.concatenate([cos, cos], axis=-1), jnp.concatenate([-sin, sin], axis=-1)


def _dot_nt(a, b):
    return lax.dot_general(a, b, (((1,), (1,)), ((), ())), preferred_element_type=F32)


def _rotary(x, cosf, sinf):
    return x * cosf + pltpu.roll(x, DK_RET // 2, axis=1) * sinf


def _sigmoid(x):
    return 1.0 / (1.0 + jnp.exp(-x))


def _ret_heads(p_ref, cosf, sinf, dmask_ref, kdec_ref):
    heads = range(H_RET)
    q = [_rotary(p_ref[:, OFF_QR + h * DK_RET:OFF_QR + (h + 1) * DK_RET], cosf, sinf) for h in heads]
    k = [_rotary(p_ref[:, OFF_KR + h * DK_RET:OFF_KR + (h + 1) * DK_RET], cosf, sinf) * (DK_RET ** -0.5)
         for h in heads]
    v = [p_ref[:, OFF_VR + h * DV_RET:OFF_VR + (h + 1) * DV_RET].astype(BF16) for h in heads]
    qb = [q[h].astype(BF16) for h in heads]
    scores = [_dot_nt(qb[h], k[h].astype(BF16)) * dmask_ref[h] for h in heads]
    inner = [jnp.dot(scores[h].astype(BF16), v[h], preferred_element_type=F32) for h in heads]
    kt = [(k[h] * kdec_ref[h]).T for h in heads]
    return qb, inner, kt, v


def _ret_finish(o, p_ref, h, gnw_ref, gnb_ref):
    mu = jnp.mean(o, axis=-1, keepdims=True)
    d = o - mu
    var = jnp.mean(d * d, axis=-1, keepdims=True)
    sl = slice(h * DV_RET, (h + 1) * DV_RET)
    y = d * lax.rsqrt(var + EPS) * gnw_ref[:, sl] + gnb_ref[:, sl]
    g = p_ref[:, OFF_GR + h * DV_RET:OFF_GR + (h + 1) * DV_RET]
    return y * (g * _sigmoid(g))


def _gla_log_decay(a_ref, w2_ref, b2_ref):
    a = a_ref[...].astype(BF16)
    z = jnp.dot(a, w2_ref[...], preferred_element_type=F32) + b2_ref[...]
    return (jnp.minimum(z, 0.0) - jnp.log1p(jnp.exp(-jnp.abs(z)))) * (1.0 / GLA_TEMP)


def _gla_heads(p_ref, la, sums_ref, pm_ref, rows, n_levels):
    heads = range(H_GLA)
    hilo = []
    for h in heads:
        la_h = la[:, h * DK_GLA:(h + 1) * DK_GLA]
        hi = la_h.astype(BF16)
        hilo += [hi, (la_h - hi.astype(F32)).astype(BF16)]
    e2 = jnp.dot(sums_ref[...], jnp.concatenate(hilo, axis=1), preferred_element_type=F32)
    ex = [e2[:, 2 * h * DK_GLA:(2 * h + 1) * DK_GLA] + e2[:, (2 * h + 1) * DK_GLA:(2 * h + 2) * DK_GLA]
          for h in heads]
    q = [p_ref[:, OFF_QG + h * DK_GLA:OFF_QG + (h + 1) * DK_GLA] * (DK_GLA ** -0.5) for h in heads]
    k = [p_ref[:, OFF_KG + h * DK_GLA:OFF_KG + (h + 1) * DK_GLA] for h in heads]
    v = [p_ref[:, OFF_VG + h * DV_GLA:OFF_VG + (h + 1) * DV_GLA].astype(BF16) for h in heads]
    amat = [_dot_nt(q[h].astype(BF16), k[h].astype(BF16)) * pm_ref[n_levels] for h in heads]
    for l in range(n_levels):
        for h in heads:
            dec = jnp.exp(ex[h][(2 + l) * rows:(3 + l) * rows])
            amat[h] = amat[h] + _dot_nt((q[h] * dec).astype(BF16), (k[h] * dec).astype(BF16)) * pm_ref[l]
    inner = [jnp.dot(amat[h].astype(BF16), v[h], preferred_element_type=F32) for h in heads]
    bcum = [ex[h][0:rows] for h in heads]
    qd = [(q[h] * jnp.exp(bcum[h])).astype(BF16) for h in heads]
    kt = [(k[h] * jnp.exp(ex[h][rows:2 * rows])).T for h in heads]
    return qd, inner, kt, v, bcum


def _gla_finish(o, p_ref, h, gw_ref):
    sl = slice(h * DV_GLA, (h + 1) * DV_GLA)
    y = o * lax.rsqrt(jnp.mean(o * o, axis=-1, keepdims=True) + EPS) * gw_ref[:, sl]
    g = p_ref[:, OFF_GG + h * DV_GLA:OFF_GG + (h + 1) * DV_GLA]
    return y * (g * _sigmoid(g))


def _rows_to_cols(row, n):
    return jnp.broadcast_to(row, (n, n)).T


def _mixer_prompt_kernel(p_ref, a_ref, cos_ref, sin_ref, dmask_ref, qdec_ref, kdec_ref, cdec_ref,
                         w2_ref, b2_ref, sums_ref, pm_ref, gnrw_ref, gnrb_ref, gngw_ref,
                         mix_ref, sret_ref, sgla_ref, *, rows, n_levels):
    @pl.when(pl.program_id(1) == 0)
    def _():
        sret_ref[...] = jnp.zeros_like(sret_ref)
        sgla_ref[...] = jnp.zeros_like(sgla_ref)

    cosf = cos_ref[...]
    sinf = sin_ref[...]
    rh = range(H_RET)
    qb, inner, kt, v = _ret_heads(p_ref, cosf, sinf, dmask_ref, kdec_ref)
    s_old = [sret_ref[0, h] for h in rh]
    cross = [jnp.dot(qb[h], s_old[h].astype(BF16), preferred_element_type=F32) * qdec_ref[h] for h in rh]
    upd = [jnp.dot(kt[h].astype(BF16), v[h], preferred_element_type=F32) for h in rh]
    for h in rh:
        sret_ref[0, h] = s_old[h] * cdec_ref[h] + upd[h]
    for h in rh:
        y = _ret_finish(inner[h] + cross[h], p_ref, h, gnrw_ref, gnrb_ref)
        mix_ref[:, h * DV_RET:(h + 1) * DV_RET] = y.astype(mix_ref.dtype)

    la = _gla_log_decay(a_ref, w2_ref, b2_ref)
    gh = range(H_GLA)
    qd, inner, kt, v, bcum = _gla_heads(p_ref, la, sums_ref, pm_ref, rows, n_levels)
    g_old = [sgla_ref[0, h] for h in gh]
    cross = [jnp.dot(qd[h], g_old[h].astype(BF16), preferred_element_type=F32) for h in gh]
    upd = [jnp.dot(kt[h].astype(BF16), v[h], preferred_element_type=F32) for h in gh]
    for h in gh:
        dec = jnp.exp(_rows_to_cols(bcum[h][rows - 1:rows, :], DK_GLA))
        sgla_ref[0, h] = g_old[h] * jnp.concatenate([dec, dec], axis=1) + upd[h]
    for h in gh:
        y = _gla_finish(inner[h] + cross[h], p_ref, h, gngw_ref)
        off = H_RET * DV_RET + h * DV_GLA
        mix_ref[:, off:off + DV_GLA] = y.astype(mix_ref.dtype)


def _mixer_prompt(proj, a_lr, B, L, consts):
    rows = PROMPT_CHUNK
    nc = L // rows
    cosf, sinf, dmask, qdec, kdec, cdec, w2, b2, sums, pm, gnrw, gnrb, gngw = consts
    n_levels = pm.shape[0] - 1
    const2 = lambda b, c: (0, 0)
    const3 = lambda b, c: (0, 0, 0)
    kern = functools.partial(_mixer_prompt_kernel, rows=rows, n_levels=n_levels)
    return pl.pallas_call(
        kern,
        grid=(B, nc),
        in_specs=[
            pl.BlockSpec((rows, D_IN_MAIN), lambda b, c: (b * nc + c, 0)),
            pl.BlockSpec((rows, LANES), lambda b, c: (b * nc + c, 0)),
            pl.BlockSpec((rows, DK_RET), lambda b, c: (c, 0)),
            pl.BlockSpec((rows, DK_RET), lambda b, c: (c, 0)),
            pl.BlockSpec(dmask.shape, const3),
            pl.BlockSpec(qdec.shape, const3),
            pl.BlockSpec(kdec.shape, const3),
            pl.BlockSpec(cdec.shape, const3),
            pl.BlockSpec(w2.shape, const2),
            pl.BlockSpec(b2.shape, const2),
            pl.BlockSpec(sums.shape, const2),
            pl.BlockSpec(pm.shape, const3),
            pl.BlockSpec(gnrw.shape, const2),
            pl.BlockSpec(gnrb.shape, const2),
            pl.BlockSpec(gngw.shape, const2),
        ],
        out_specs=[
            pl.BlockSpec((rows, D_MIX), lambda b, c: (b * nc + c, 0)),
            pl.BlockSpec((1, H_RET, DK_RET, DV_RET), lambda b, c: (b, 0, 0, 0)),
            pl.BlockSpec((1, H_GLA, DK_GLA, DV_GLA), lambda b, c: (b, 0, 0, 0)),
        ],
        out_shape=[
            jax.ShapeDtypeStruct((B * L, D_MIX), BF16),
            jax.ShapeDtypeStruct((B, H_RET, DK_RET, DV_RET), F32),
            jax.ShapeDtypeStruct((B, H_GLA, DK_GLA, DV_GLA), F32),
        ],
        compiler_params=_cparams(("parallel", "arbitrary")),
        name="mixer_prompt",
    )(proj, a_lr, cosf, sinf, dmask, qdec, kdec, cdec, w2, b2, sums, pm, gnrw, gnrb, gngw)


def _mixer_sample_kernel(p_ref, a_ref, cos_ref, sin_ref, dmask_ref, qdec_ref, kdec_ref, cdec_ref,
                         w2_ref, b2_ref, sums_ref, pm_ref, gnrw_ref, gnrb_ref, gngw_ref,
                         sret0_ref, sgla0_ref,
                         mix_ref, sret_ref, sgla_ref, *, rows, seq_len, n_levels):
    n_seq = rows // seq_len
    row_seq = lax.broadcasted_iota(jnp.int32, (rows, 1), 0) // seq_len
    col_seq = lax.broadcasted_iota(jnp.int32, (1, rows), 1) // seq_len
    cosf = cos_ref[...]
    sinf = sin_ref[...]
    qb, inner, kt, v = _ret_heads(p_ref, cosf, sinf, dmask_ref, kdec_ref)
    for h in range(H_RET):
        cross = jnp.zeros((rows, DV_RET), F32)
        for s in range(n_seq):
            s_old = sret0_ref[s, h]
            c_s = jnp.dot(qb[h], s_old.astype(BF16), preferred_element_type=F32)
            cross = cross + jnp.where(row_seq == s, c_s, 0.0)
            kt_s = jnp.where(col_seq == s, kt[h], 0.0).astype(BF16)
            sret_ref[s, h] = s_old * cdec_ref[h] + jnp.dot(kt_s, v[h], preferred_element_type=F32)
        y = _ret_finish(inner[h] + cross * qdec_ref[h], p_ref, h, gnrw_ref, gnrb_ref)
        mix_ref[:, h * DV_RET:(h + 1) * DV_RET] = y.astype(mix_ref.dtype)

    la = _gla_log_decay(a_ref, w2_ref, b2_ref)
    qd, inner, kt, v, bcum = _gla_heads(p_ref, la, sums_ref, pm_ref, rows, n_levels)
    for h in range(H_GLA):
        cross = jnp.zeros((rows, DV_GLA), F32)
        for s in range(n_seq):
            s_old = sgla0_ref[s, h]
            c_s = jnp.dot(qd[h], s_old.astype(BF16), preferred_element_type=F32)
            cross = cross + jnp.where(row_seq == s, c_s, 0.0)
            kt_s = jnp.where(col_seq == s, kt[h], 0.0).astype(BF16)
            last = (s + 1) * seq_len - 1
            dec = jnp.exp(_rows_to_cols(bcum[h][last:last + 1, :], DK_GLA))
            dec = jnp.concatenate([dec, dec], axis=1)
            sgla_ref[s, h] = s_old * dec + jnp.dot(kt_s, v[h], preferred_element_type=F32)
        y = _gla_finish(inner[h] + cross, p_ref, h, gngw_ref)
        off = H_RET * DV_RET + h * DV_GLA
        mix_ref[:, off:off + DV_GLA] = y.astype(mix_ref.dtype)


def _mixer_sample(proj, a_lr, sret0, sgla0, Bs, Ls, consts):
    n_seq = SAMPLE_GROUP
    rows = n_seq * Ls
    cosf, sinf, dmask, qdec, kdec, cdec, w2, b2, sums, pm, gnrw, gnrb, gngw = consts
    n_levels = pm.shape[0] - 1
    const2 = lambda g: (0, 0)
    const3 = lambda g: (0, 0, 0)
    kern = functools.partial(_mixer_sample_kernel, rows=rows, seq_len=Ls, n_levels=n_levels)
    st_spec = pl.BlockSpec((None, n_seq, H_RET, DK_RET, DV_RET), lambda g: (0, g, 0, 0, 0))
    return pl.pallas_call(
        kern,
        grid=(Bs // n_seq,),
        in_specs=[
            pl.BlockSpec((rows, D_IN_MAIN), lambda g: (g, 0)),
            pl.BlockSpec((rows, LANES), lambda g: (g, 0)),
            pl.BlockSpec(cosf.shape, const2),
            pl.BlockSpec(sinf.shape, const2),
            pl.BlockSpec(dmask.shape, const3),
            pl.BlockSpec(qdec.shape, const3),
            pl.BlockSpec(kdec.shape, const3),
            pl.BlockSpec(cdec.shape, const3),
            pl.BlockSpec(w2.shape, const2),
            pl.BlockSpec(b2.shape, const2),
            pl.BlockSpec(sums.shape, const2),
            pl.BlockSpec(pm.shape, const3),
            pl.BlockSpec(gnrw.shape, const2),
            pl.BlockSpec(gnrb.shape, const2),
            pl.BlockSpec(gngw.shape, const2),
            st_spec,
            st_spec,
        ],
        out_specs=[
            pl.BlockSpec((rows, D_MIX), lambda g: (g, 0)),
            st_spec,
            st_spec,
        ],
        out_shape=[
            jax.ShapeDtypeStruct((Bs * Ls, D_MIX), BF16),
            jax.ShapeDtypeStruct((1, Bs, H_RET, DK_RET, DV_RET), F32),
            jax.ShapeDtypeStruct((1, Bs, H_GLA, DK_GLA, DV_GLA), F32),
        ],
        compiler_params=_cparams(("parallel",)),
        name="mixer_sample",
    )(proj, a_lr, cosf, sinf, dmask, qdec, kdec, cdec, w2, b2, sums, pm, gnrw, gnrb, gngw, sret0, sgla0)


def _out_query_kernel(mix_ref, x_ref, wo_ref, wn_ref, wq_ref, k1_ref, k2_ref,
                      x1_ref, h2_ref, s1_ref, s2_ref):
    x1 = x_ref[...] + jnp.dot(mix_ref[...], wo_ref[...], preferred_element_type=F32)
    x1_ref[...] = x1
    ms = jnp.mean(x1 * x1, axis=-1, keepdims=True)
    h2 = (x1 * lax.rsqrt(ms + EPS) * wn_ref[...]).astype(BF16)
    h2_ref[...] = h2
    q = jnp.dot(h2, wq_ref[...], preferred_element_type=F32).astype(BF16)
    half = PEER_DQ // 2
    for h in range(PEER_HEADS):
        q1 = q[:, h * PEER_DQ:h * PEER_DQ + half]
        q2 = q[:, h * PEER_DQ + half:(h + 1) * PEER_DQ]
        s1_ref[h] = _dot_nt(k1_ref[h], q1)
        s2_ref[h] = _dot_nt(k2_ref[h], q2)


def _out_query(mix, x, wo, wn, wq, k1, k2, tm):
    T, D = x.shape
    resident = dict(pipeline_mode=pl.Buffered(1))
    return pl.pallas_call(
        _out_query_kernel,
        grid=(T // tm,),
        in_specs=[
            pl.BlockSpec((tm, D_MIX), lambda i: (i, 0)),
            pl.BlockSpec((tm, D), lambda i: (i, 0)),
            pl.BlockSpec(wo.shape, lambda i: (0, 0), **resident),
            pl.BlockSpec((1, D), lambda i: (0, 0)),
            pl.BlockSpec(wq.shape, lambda i: (0, 0), **resident),
            pl.BlockSpec(k1.shape, lambda i: (0, 0, 0)),
            pl.BlockSpec(k2.shape, lambda i: (0, 0, 0)),
        ],
        out_specs=[
            pl.BlockSpec((tm, D), lambda i: (i, 0)),
            pl.BlockSpec((tm, D), lambda i: (i, 0)),
            pl.BlockSpec((PEER_HEADS, PEER_NKEYS, tm), lambda i: (0, 0, i)),
            pl.BlockSpec((PEER_HEADS, PEER_NKEYS, tm), lambda i: (0, 0, i)),
        ],
        out_shape=[
            jax.ShapeDtypeStruct((T, D), F32),
            jax.ShapeDtypeStruct((T, D), BF16),
            jax.ShapeDtypeStruct((PEER_HEADS, PEER_NKEYS, T), F32),
            jax.ShapeDtypeStruct((PEER_HEADS, PEER_NKEYS, T), F32),
        ],
        compiler_params=_cparams(("parallel",)),
        name="out_proj_peer_query",
    )(mix, x, wo, wn, wq, k1, k2)


def _candidate_pairs():
    return [(a, b) for a in range(PEER_TOPK) for b in range(PEER_TOPK) if (a + 1) * (b + 1) <= PEER_TOPK]


def _top_values(s, k):
    vals = []
    rank = jnp.full(s.shape, float(k), F32)
    for i in range(k):
        m = jnp.max(s, axis=0, keepdims=True)
        vals.append(m)
        hit = s == m
        rank = jnp.where(hit, float(i), rank)
        s = jnp.where(hit, -jnp.inf, s)
    return vals, rank


def _route_weights_kernel(s1_ref, s2_ref, g_ref, n1_ref, e1_ref, r2_ref, e2_ref, *, lane_chunk):
    pairs = _candidate_pairs()
    n_cand = -(-len(pairs) // 8) * 8
    tt = s1_ref.shape[-1]
    rows = lax.broadcasted_iota(jnp.int32, (n_cand, tt), 0)
    for h in range(PEER_HEADS):
        s1 = s1_ref[h]
        s2 = s2_ref[h]
        v1, _ = _top_values(s1, PEER_TOPK)
        v2, rank2 = _top_values(s2, PEER_TOPK)
        cand = jnp.full((n_cand, tt), -jnp.inf, F32)
        for r, (a, b) in enumerate(pairs):
            cand = jnp.where(rows == r, v1[a] + v2[b], cand)
        tau = _top_values(cand, PEER_TOPK)[0][-1]
        z = jnp.sum(jnp.where(cand >= tau, jnp.exp(cand - (v1[0] + v2[0])), 0.0), axis=0, keepdims=True)
        n1 = jnp.zeros(s1.shape, F32)
        for b in range(PEER_TOPK):
            n1 = n1 + jnp.where(s1 + v2[b] >= tau, 1.0, 0.0)
        n1_ref[h] = n1
        e1_ref[h] = jnp.exp(s1 - v1[0]) / z
        r2_ref[h] = rank2.astype(ROUTE_DTYPE)
        e2_ref[h] = jnp.exp(s2 - v2[0]).astype(ROUTE_DTYPE)

    def row_block(i1, carry):
        n1_rows = [n1_ref[h, pl.ds(i1, 1), :] for h in range(PEER_HEADS)]
        e1_rows = [e1_ref[h, pl.ds(i1, 1), :] for h in range(PEER_HEADS)]
        row0 = pl.multiple_of(i1 * PEER_NKEYS, PEER_NKEYS)
        for c0 in range(0, tt, lane_chunk):
            cs = slice(c0, c0 + lane_chunk)
            g = None
            for h in range(PEER_HEADS):
                r2 = r2_ref[h, :, cs]
                e2 = e2_ref[h, :, cs]
                n1c = jnp.broadcast_to(n1_rows[h][:, cs], e2.shape).astype(ROUTE_DTYPE)
                e1c = jnp.broadcast_to(e1_rows[h][:, cs], e2.shape).astype(ROUTE_DTYPE)
                term = jnp.where(r2 < n1c, e2, jnp.zeros_like(e2)) * e1c
                g = term if g is None else g + term
            g_ref[pl.ds(row0, PEER_NKEYS), cs] = g.astype(g_ref.dtype)
        return carry

    lax.fori_loop(0, PEER_NKEYS, row_block, 0)


def _route_weights(s1, s2, tt):
    H, K, T = s1.shape
    blk = pl.BlockSpec((H, K, tt), lambda i: (0, 0, i))
    kern = functools.partial(_route_weights_kernel, lane_chunk=min(tt, LANES))
    return pl.pallas_call(
        kern,
        grid=(T // tt,),
        in_specs=[blk, blk],
        out_specs=pl.BlockSpec((None, N_EXPERTS, tt), lambda i: (i, 0, 0)),
        out_shape=jax.ShapeDtypeStruct((T // tt, N_EXPERTS, tt), BF16),
        scratch_shapes=[
            pltpu.VMEM((H, K, tt), F32),
            pltpu.VMEM((H, K, tt), F32),
            pltpu.VMEM((H, K, tt), ROUTE_DTYPE),
            pltpu.VMEM((H, K, tt), ROUTE_DTYPE),
        ],
        compiler_params=_cparams(("parallel",)),
        name="peer_route_weights",
    )(s1, s2)


def _peer_kernel(h2_ref, u_ref, vt_ref, g_ref, x1_ref, wn_ref, y_ref, acc_ref, w_ref, h2s_ref):
    e = pl.program_id(1)

    @pl.when(e == 0)
    def _():
        acc_ref[...] = jnp.zeros_like(acc_ref)
        h2s_ref[...] = h2_ref[...].astype(F32).T.astype(BF16)

    n_sub, _, sub = g_ref.shape
    for t in range(n_sub):
        cols = slice(t * sub, (t + 1) * sub)
        s = jnp.dot(u_ref[...].astype(BF16), h2s_ref[:, cols], preferred_element_type=F32)
        act = 0.5 * s * (1.0 + lax.erf(s * (2.0 ** -0.5)))
        w_ref[:, cols] = g_ref[t] * act.astype(BF16)
    acc_ref[...] += jnp.dot(vt_ref[...], w_ref[...], preferred_element_type=F32)

    @pl.when(e == pl.num_programs(1) - 1)
    def _():
        out = x1_ref[...] + acc_ref[...].T
        ms = jnp.mean(out * out, axis=-1, keepdims=True)
        y_ref[...] = out * lax.rsqrt(ms + EPS) * wn_ref[...]


def _peer(h2, u, vt_bf, g, x1, wn, tt):
    T, D = x1.shape
    nb, _, eb = vt_bf.shape
    sub = g.shape[2]
    return pl.pallas_call(
        _peer_kernel,
        grid=(T // tt, nb),
        in_specs=[
            pl.BlockSpec((tt, D), lambda i, e: (i, 0)),
            pl.BlockSpec((eb, D), lambda i, e: (e, 0)),
            pl.BlockSpec((None, D, eb), lambda i, e: (e, 0, 0)),
            pl.BlockSpec((tt // sub, eb, sub), lambda i, e: (i, e, 0)),
            pl.BlockSpec((tt, D), lambda i, e: (i, 0), pipeline_mode=pl.Buffered(1)),
            pl.BlockSpec((1, D), lambda i, e: (0, 0)),
        ],
        out_specs=pl.BlockSpec((tt, D), lambda i, e: (i, 0), pipeline_mode=pl.Buffered(1)),
        out_shape=jax.ShapeDtypeStruct((T, D), F32),
        scratch_shapes=[
            pltpu.VMEM((D, tt), F32),
            pltpu.VMEM((eb, tt), BF16),
            pltpu.VMEM((D, tt), BF16),
        ],
        compiler_params=_cparams(("parallel", "arbitrary")),
        name="peer_dense",
    )(h2, u, vt_bf, g, x1, wn)


def _block_transpose_kernel(v_ref, o_ref):
    o_ref[...] = v_ref[...].T.astype(o_ref.dtype)


def _value_blocks(v, eb):
    E, D = v.shape
    return pl.pallas_call(
        _block_transpose_kernel,
        grid=(E // eb,),
        in_specs=[pl.BlockSpec((eb, D), lambda i: (i, 0))],
        out_specs=pl.BlockSpec((None, D, eb), lambda i: (i, 0, 0)),
        out_shape=jax.ShapeDtypeStruct((E // eb, D, eb), BF16),
        compiler_params=_cparams(("parallel",)),
        name="peer_value_blocks",
    )(v)


def _token_tile(T, pref):
    t = min(pref, T)
    while T % t:
        t //= 2
    return t


def _ffn(mix, x2d, wo, wnf, wq, k1, k2, u, vt_bf, wn_final):
    T = x2d.shape[0]
    x1, h2, s1, s2 = _out_query(mix, x2d, wo, wnf, wq, k1, k2, _token_tile(T, ROUTE_TOKEN_TILE))
    g = _route_weights(s1, s2, _token_tile(T, ROUTE_TOKEN_TILE))
    return _peer(h2, u, vt_bf, g, x1, wn_final, _token_tile(T, PEER_TOKEN_TILE))


def kernel(x_prompt, x_sample, state_ret, state_gla, w_norm_mix, w_in, w_gla_a2, b_gla_a, gn_ret_w, gn_ret_b, gn_gla_w, w_out, w_norm_ffn, w_pq, sub_keys1, sub_keys2, u_tab, v_tab, w_norm_final):
    Bp, Lp, D = x_prompt.shape
    Bs, Ls, _ = x_sample.shape
    depth = w_in.shape[0]
    assert depth == 1 and D == D_MODEL
    l = 0
    xp = x_prompt.reshape(Bp * Lp, D)
    xs = x_sample.reshape(Bs * Ls, D)

    w_main = w_in[l, :, :D_IN_MAIN].astype(BF16)
    w_gate = jnp.pad(w_in[l, :, D_IN_MAIN:], ((0, 0), (0, LANES - GLA_RANK))).astype(BF16)
    w2 = jnp.pad(w_gla_a2[l], ((0, LANES - GLA_RANK), (0, 0))).astype(BF16)
    b2 = b_gla_a[l].reshape(1, -1)
    gnrw = gn_ret_w[l].reshape(1, -1)
    gnrb = gn_ret_b[l].reshape(1, -1)
    gngw = gn_gla_w[l].reshape(1, -1)
    wnm = w_norm_mix[l].reshape(1, D)
    wnf = w_norm_ffn[l].reshape(1, D)
    wn_final = w_norm_final.reshape(1, D)
    wo = w_out[l].astype(BF16)
    wq = w_pq[l].astype(BF16)
    k1 = sub_keys1[l].astype(BF16)
    k2 = sub_keys2[l].astype(BF16)
    u = u_tab[l]
    vt_bf = _value_blocks(v_tab[l], PEER_EXPERT_BLOCK)

    cos_p, sin_p = _rope_tables(jnp.arange(Lp, dtype=F32))
    pos_s = PAST_LEN + jnp.arange(Ls, dtype=F32)
    cos_s, sin_s = _rope_tables(jnp.tile(pos_s, SAMPLE_GROUP))
    sums_p, pm_p = _gla_consts(PROMPT_CHUNK, PROMPT_CHUNK)
    sums_s, pm_s = _gla_consts(SAMPLE_GROUP * Ls, Ls)
    consts_p = (cos_p, sin_p) + _ret_consts(PROMPT_CHUNK, PROMPT_CHUNK) + (
        w2, b2, jnp.asarray(sums_p, BF16), jnp.asarray(pm_p), gnrw, gnrb, gngw)
    consts_s = (cos_s, sin_s) + _ret_consts(SAMPLE_GROUP * Ls, Ls) + (
        w2, b2, jnp.asarray(sums_s, BF16), jnp.asarray(pm_s), gnrw, gnrb, gngw)

    proj_p, a_p = _norm_matmul(xp, wnm, w_main, w_gate, _token_tile(Bp * Lp, IN_PROJ_TOKEN_TILE), IN_PROJ_COL_TILE)
    proj_s, a_s = _norm_matmul(xs, wnm, w_main, w_gate, _token_tile(Bs * Ls, IN_PROJ_TOKEN_TILE), IN_PROJ_COL_TILE)
    mix_p, rp, gp = _mixer_prompt(proj_p, a_p, Bp, Lp, consts_p)
    mix_s, rs, gs = _mixer_sample(proj_s, a_s, state_ret, state_gla, Bs, Ls, consts_s)

    yp = _ffn(mix_p, xp, wo, wnf, wq, k1, k2, u, vt_bf, wn_final)
    ys = _ffn(mix_s, xs, wo, wnf, wq, k1, k2, u, vt_bf, wn_final)
    return (yp.reshape(Bp, Lp, D), ys.reshape(Bs, Ls, D), rp[None], gp[None], rs, gs)
```

```python
import functools
import math

import numpy as np
import jax
import jax.numpy as jnp
from jax import lax
from jax.experimental import pallas as pl
from jax.experimental.pallas import tpu as pltpu

F32 = jnp.float32
BF16 = jnp.bfloat16
ROUTE_DTYPE = F32

D_MODEL = 2048
H_RET, DK_RET, DV_RET = 4, 128, 256
H_GLA, DK_GLA, DV_GLA = 4, 128, 256
GLA_RANK = 16
GLA_TEMP = 16.0
ROPE_BASE = 10000.0
PAST_LEN = 16384
PEER_HEADS, PEER_NKEYS, PEER_DQ, PEER_TOPK = 8, 128, 256, 16
N_EXPERTS = PEER_NKEYS * PEER_NKEYS
EPS = 1e-6

LANES = 128
D_MIX = H_RET * DV_RET + H_GLA * DV_GLA
D_IN_MAIN = 2 * H_RET * DK_RET + 2 * H_RET * DV_RET + 2 * H_GLA * DK_GLA + 2 * H_GLA * DV_GLA
OFF_QR = 0
OFF_KR = OFF_QR + H_RET * DK_RET
OFF_VR = OFF_KR + H_RET * DK_RET
OFF_GR = OFF_VR + H_RET * DV_RET
OFF_QG = OFF_GR + H_RET * DV_RET
OFF_KG = OFF_QG + H_GLA * DK_GLA
OFF_VG = OFF_KG + H_GLA * DK_GLA
OFF_GG = OFF_VG + H_GLA * DV_GLA

IN_PROJ_TOKEN_TILE = 1024
IN_PROJ_COL_TILE = 1536
QUERY_TOKEN_TILE = 256
ROUTE_TOKEN_TILE = 256
PEER_TOKEN_TILE = 512
PEER_EXPERT_BLOCK = 1024
PROMPT_CHUNK = 128
SAMPLE_GROUP = 8
VMEM_LIMIT = 56 * 1024 * 1024


def _cparams(sem):
    return pltpu.CompilerParams(dimension_semantics=sem, vmem_limit_bytes=VMEM_LIMIT)


def _norm_matmul_kernel(x_ref, wn_ref, w_ref, wa_ref, o_ref, oa_ref, h_ref):
    @pl.when(pl.program_id(1) == 0)
    def _():
        x = x_ref[...]
        ms = jnp.mean(x * x, axis=-1, keepdims=True)
        h = (x * lax.rsqrt(ms + EPS) * wn_ref[...]).astype(BF16)
        h_ref[...] = h
        oa_ref[...] = jnp.dot(h, wa_ref[...], preferred_element_type=F32)

    o_ref[...] = jnp.dot(h_ref[...], w_ref[...], preferred_element_type=F32)


def _norm_matmul(x, wn, w, wa, tm, tn):
    T, D = x.shape
    N = w.shape[1]
    Na = wa.shape[1]
    return pl.pallas_call(
        _norm_matmul_kernel,
        grid=(T // tm, N // tn),
        in_specs=[
            pl.BlockSpec((tm, D), lambda i, j: (i, 0)),
            pl.BlockSpec((1, D), lambda i, j: (0, 0)),
            pl.BlockSpec((D, tn), lambda i, j: (0, j)),
            pl.BlockSpec((D, Na), lambda i, j: (0, 0)),
        ],
        out_specs=[
            pl.BlockSpec((tm, tn), lambda i, j: (i, j)),
            pl.BlockSpec((tm, Na), lambda i, j: (i, 0)),
        ],
        out_shape=[jax.ShapeDtypeStruct((T, N), F32), jax.ShapeDtypeStruct((T, Na), F32)],
        scratch_shapes=[pltpu.VMEM((tm, D), BF16)],
        compiler_params=_cparams(("parallel", "arbitrary")),
        name="norm_in_proj",
    )(x, wn, w, wa)


def _gla_consts(rows, blk):
    idx = np.arange(rows)
    seq = idx // blk
    t = idx[None, :]
    i = idx[:, None]
    same = seq[:, None] == seq[None, :]
    mats = [same & (t <= i), same & (t > i)]
    pmasks = []
    s = blk
    while s >= 2:
        bstart = (idx // s) * s
        mid = bstart + s // 2
        upper = idx >= mid
        mats.append((upper[:, None] & (t >= mid[:, None]) & (t <= i))
                    | ((~upper)[:, None] & (t > i) & (t <= mid[:, None] - 1)))
        pmasks.append((bstart[:, None] == bstart[None, :]) & upper[:, None] & (~upper)[None, :])
        s //= 2
    pmasks.append(np.eye(rows, dtype=bool))
    return (np.concatenate(mats, 0).astype(np.float32), np.stack(pmasks).astype(np.float32))


def _ret_consts(rows, blk):
    log_g = jnp.log1p(-jnp.exp2(-5.0 - jnp.arange(H_RET, dtype=F32)))
    idx = np.arange(rows)
    p = jnp.asarray(idx % blk, F32)
    same = jnp.asarray((idx[:, None] // blk) == (idx[None, :] // blk))
    dist = p[:, None] - p[None, :]
    causal = same & (dist >= 0)
    dmask = jnp.where(causal[None], jnp.exp(jnp.maximum(dist, 0.0)[None] * log_g[:, None, None]), 0.0)
    qdec = jnp.exp((p + 1.0)[None, :] * log_g[:, None])
    kdec = jnp.exp((blk - 1.0 - p)[None, :] * log_g[:, None])
    cdec = jnp.exp(blk * log_g)
    qdec_full = jnp.broadcast_to(qdec[:, :, None], (H_RET, rows, DV_RET))
    kdec_full = jnp.broadcast_to(kdec[:, :, None], (H_RET, rows, DK_RET))
    cdec_full = jnp.broadcast_to(cdec[:, None, None], (H_RET, DK_RET, DV_RET))
    return dmask.astype(F32), qdec_full.astype(F32), kdec_full.astype(F32), cdec_full.astype(F32)


def _rope_tables(pos):
    d = DK_RET
    freqs = ROPE_BASE ** (-jnp.arange(0, d, 2, dtype=F32) / d)
    ang = pos[:, None] * freqs[None, :]
    cos = jnp.cos(ang)
    sin = jnp.sin(ang)
    return jnp.concatenate([cos, cos], axis=-1), jnp.concatenate([-sin, sin], axis=-1)


def _dot_nt(a, b):
    return lax.dot_general(a, b, (((1,), (1,)), ((), ())), preferred_element_type=F32)


def _rotary(x, cosf, sinf):
    return x * cosf + pltpu.roll(x, DK_RET // 2, axis=1) * sinf


def _sigmoid(x):
    return 1.0 / (1.0 + jnp.exp(-x))


def _ret_heads(p_ref, cosf, sinf, dmask_ref, kdec_ref):
    heads = range(H_RET)
    q = [_rotary(p_ref[:, OFF_QR + h * DK_RET:OFF_QR + (h + 1) * DK_RET], cosf, sinf) for h in heads]
    k = [_rotary(p_ref[:, OFF_KR + h * DK_RET:OFF_KR + (h + 1) * DK_RET], cosf, sinf) * (DK_RET ** -0.5)
         for h in heads]
    v = [p_ref[:, OFF_VR + h * DV_RET:OFF_VR + (h + 1) * DV_RET].astype(BF16) for h in heads]
    qb = [q[h].astype(BF16) for h in heads]
    scores = [_dot_nt(qb[h], k[h].astype(BF16)) * dmask_ref[h] for h in heads]
    inner = [jnp.dot(scores[h].astype(BF16), v[h], preferred_element_type=F32) for h in heads]
    kt = [(k[h] * kdec_ref[h]).T for h in heads]
    return qb, inner, kt, v


def _ret_finish(o, p_ref, h, gnw_ref, gnb_ref):
    mu = jnp.mean(o, axis=-1, keepdims=True)
    d = o - mu
    var = jnp.mean(d * d, axis=-1, keepdims=True)
    sl = slice(h * DV_RET, (h + 1) * DV_RET)
    y = d * lax.rsqrt(var + EPS) * gnw_ref[:, sl] + gnb_ref[:, sl]
    g = p_ref[:, OFF_GR + h * DV_RET:OFF_GR + (h + 1) * DV_RET]
    return y * (g * _sigmoid(g))


def _gla_log_decay(a_ref, w2_ref, b2_ref):
    a = a_ref[...].astype(BF16)
    z = jnp.dot(a, w2_ref[...], preferred_element_type=F32) + b2_ref[...]
    return (jnp.minimum(z, 0.0) - jnp.log1p(jnp.exp(-jnp.abs(z)))) * (1.0 / GLA_TEMP)


def _gla_heads(p_ref, la, sums_ref, pm_ref, rows, n_levels):
    heads = range(H_GLA)
    hilo = []
    for h in heads:
        la_h = la[:, h * DK_GLA:(h + 1) * DK_GLA]
        hi = la_h.astype(BF16)
        hilo += [hi, (la_h - hi.astype(F32)).astype(BF16)]
    e2 = jnp.dot(sums_ref[...], jnp.concatenate(hilo, axis=1), preferred_element_type=F32)
    ex = [e2[:, 2 * h * DK_GLA:(2 * h + 1) * DK_GLA] + e2[:, (2 * h + 1) * DK_GLA:(2 * h + 2) * DK_GLA]
          for h in heads]
    q = [p_ref[:, OFF_QG + h * DK_GLA:OFF_QG + (h + 1) * DK_GLA] * (DK_GLA ** -0.5) for h in heads]
    k = [p_ref[:, OFF_KG + h * DK_GLA:OFF_KG + (h + 1) * DK_GLA] for h in heads]
    v = [p_ref[:, OFF_VG + h * DV_GLA:OFF_VG + (h + 1) * DV_GLA].astype(BF16) for h in heads]
    amat = [_dot_nt(q[h].astype(BF16), k[h].astype(BF16)) * pm_ref[n_levels] for h in heads]
    for l in range(n_levels):
        for h in heads:
            dec = jnp.exp(ex[h][(2 + l) * rows:(3 + l) * rows])
            amat[h] = amat[h] + _dot_nt((q[h] * dec).astype(BF16), (k[h] * dec).astype(BF16)) * pm_ref[l]
    inner = [jnp.dot(amat[h].astype(BF16), v[h], preferred_element_type=F32) for h in heads]
    bcum = [ex[h][0:rows] for h in heads]
    qd = [(q[h] * jnp.exp(bcum[h])).astype(BF16) for h in heads]
    kt = [(k[h] * jnp.exp(ex[h][rows:2 * rows])).T for h in heads]
    return qd, inner, kt, v, bcum


def _gla_finish(o, p_ref, h, gw_ref):
    sl = slice(h * DV_GLA, (h + 1) * DV_GLA)
    y = o * lax.rsqrt(jnp.mean(o * o, axis=-1, keepdims=True) + EPS) * gw_ref[:, sl]
    g = p_ref[:, OFF_GG + h * DV_GLA:OFF_GG + (h + 1) * DV_GLA]
    return y * (g * _sigmoid(g))


def _rows_to_cols(row, n):
    return jnp.broadcast_to(row, (n, n)).T


def _mixer_prompt_kernel(p_ref, a_ref, cos_ref, sin_ref, dmask_ref, qdec_ref, kdec_ref, cdec_ref,
                         w2_ref, b2_ref, sums_ref, pm_ref, gnrw_ref, gnrb_ref, gngw_ref,
                         mix_ref, sret_ref, sgla_ref, *, rows, n_levels):
    @pl.when(pl.program_id(1) == 0)
    def _():
        sret_ref[...] = jnp.zeros_like(sret_ref)
        sgla_ref[...] = jnp.zeros_like(sgla_ref)

    cosf = cos_ref[...]
    sinf = sin_ref[...]
    rh = range(H_RET)
    qb, inner, kt, v = _ret_heads(p_ref, cosf, sinf, dmask_ref, kdec_ref)
    s_old = [sret_ref[0, h] for h in rh]
    cross = [jnp.dot(qb[h], s_old[h].astype(BF16), preferred_element_type=F32) * qdec_ref[h] for h in rh]
    upd = [jnp.dot(kt[h].astype(BF16), v[h], preferred_element_type=F32) for h in rh]
    for h in rh:
        sret_ref[0, h] = s_old[h] * cdec_ref[h] + upd[h]
    for h in rh:
        y = _ret_finish(inner[h] + cross[h], p_ref, h, gnrw_ref, gnrb_ref)
        mix_ref[:, h * DV_RET:(h + 1) * DV_RET] = y.astype(mix_ref.dtype)

    la = _gla_log_decay(a_ref, w2_ref, b2_ref)
    gh = range(H_GLA)
    qd, inner, kt, v, bcum = _gla_heads(p_ref, la, sums_ref, pm_ref, rows, n_levels)
    g_old = [sgla_ref[0, h] for h in gh]
    cross = [jnp.dot(qd[h], g_old[h].astype(BF16), preferred_element_type=F32) for h in gh]
    upd = [jnp.dot(kt[h].astype(BF16), v[h], preferred_element_type=F32) for h in gh]
    for h in gh:
        dec = jnp.exp(_rows_to_cols(bcum[h][rows - 1:rows, :], DK_GLA))
        sgla_ref[0, h] = g_old[h] * jnp.concatenate([dec, dec], axis=1) + upd[h]
    for h in gh:
        y = _gla_finish(inner[h] + cross[h], p_ref, h, gngw_ref)
        off = H_RET * DV_RET + h * DV_GLA
        mix_ref[:, off:off + DV_GLA] = y.astype(mix_ref.dtype)


def _mixer_prompt(proj, a_lr, B, L, consts):
    rows = PROMPT_CHUNK
    nc = L // rows
    cosf, sinf, dmask, qdec, kdec, cdec, w2, b2, sums, pm, gnrw, gnrb, gngw = consts
    n_levels = pm.shape[0] - 1
    const2 = lambda b, c: (0, 0)
    const3 = lambda b, c: (0, 0, 0)
    kern = functools.partial(_mixer_prompt_kernel, rows=rows, n_levels=n_levels)
    return pl.pallas_call(
        kern,
        grid=(B, nc),
        in_specs=[
            pl.BlockSpec((rows, D_IN_MAIN), lambda b, c: (b * nc + c, 0)),
            pl.BlockSpec((rows, LANES), lambda b, c: (b * nc + c, 0)),
            pl.BlockSpec((rows, DK_RET), lambda b, c: (c, 0)),
            pl.BlockSpec((rows, DK_RET), lambda b, c: (c, 0)),
            pl.BlockSpec(dmask.shape, const3),
            pl.BlockSpec(qdec.shape, const3),
            pl.BlockSpec(kdec.shape, const3),
            pl.BlockSpec(cdec.shape, const3),
            pl.BlockSpec(w2.shape, const2),
            pl.BlockSpec(b2.shape, const2),
            pl.BlockSpec(sums.shape, const2),
            pl.BlockSpec(pm.shape, const3),
            pl.BlockSpec(gnrw.shape, const2),
            pl.BlockSpec(gnrb.shape, const2),
            pl.BlockSpec(gngw.shape, const2),
        ],
        out_specs=[
            pl.BlockSpec((rows, D_MIX), lambda b, c: (b * nc + c, 0)),
            pl.BlockSpec((1, H_RET, DK_RET, DV_RET), lambda b, c: (b, 0, 0, 0)),
            pl.BlockSpec((1, H_GLA, DK_GLA, DV_GLA), lambda b, c: (b, 0, 0, 0)),
        ],
        out_shape=[
            jax.ShapeDtypeStruct((B * L, D_MIX), BF16),
            jax.ShapeDtypeStruct((B, H_RET, DK_RET, DV_RET), F32),
            jax.ShapeDtypeStruct((B, H_GLA, DK_GLA, DV_GLA), F32),
        ],
        compiler_params=_cparams(("parallel", "arbitrary")),
        name="mixer_prompt",
    )(proj, a_lr, cosf, sinf, dmask, qdec, kdec, cdec, w2, b2, sums, pm, gnrw, gnrb, gngw)


def _mixer_sample_kernel(p_ref, a_ref, cos_ref, sin_ref, dmask_ref, qdec_ref, kdec_ref, cdec_ref,
                         w2_ref, b2_ref, sums_ref, pm_ref, gnrw_ref, gnrb_ref, gngw_ref,
                         sret0_ref, sgla0_ref,
                         mix_ref, sret_ref, sgla_ref, *, rows, seq_len, n_levels):
    n_seq = rows // seq_len
    row_seq = lax.broadcasted_iota(jnp.int32, (rows, 1), 0) // seq_len
    col_seq = lax.broadcasted_iota(jnp.int32, (1, rows), 1) // seq_len
    cosf = cos_ref[...]
    sinf = sin_ref[...]
    qb, inner, kt, v = _ret_heads(p_ref, cosf, sinf, dmask_ref, kdec_ref)
    for h in range(H_RET):
        cross = jnp.zeros((rows, DV_RET), F32)
        for s in range(n_seq):
            s_old = sret0_ref[s, h]
            c_s = jnp.dot(qb[h], s_old.astype(BF16), preferred_element_type=F32)
            cross = cross + jnp.where(row_seq == s, c_s, 0.0)
            kt_s = jnp.where(col_seq == s, kt[h], 0.0).astype(BF16)
            sret_ref[s, h] = s_old * cdec_ref[h] + jnp.dot(kt_s, v[h], preferred_element_type=F32)
        y = _ret_finish(inner[h] + cross * qdec_ref[h], p_ref, h, gnrw_ref, gnrb_ref)
        mix_ref[:, h * DV_RET:(h + 1) * DV_RET] = y.astype(mix_ref.dtype)

    la = _gla_log_decay(a_ref, w2_ref, b2_ref)
    qd, inner, kt, v, bcum = _gla_heads(p_ref, la, sums_ref, pm_ref, rows, n_levels)
    for h in range(H_GLA):
        cross = jnp.zeros((rows, DV_GLA), F32)
        for s in range(n_seq):
            s_old = sgla0_ref[s, h]
            c_s = jnp.dot(qd[h], s_old.astype(BF16), preferred_element_type=F32)
            cross = cross + jnp.where(row_seq == s, c_s, 0.0)
            kt_s = jnp.where(col_seq == s, kt[h], 0.0).astype(BF16)
            last = (s + 1) * seq_len - 1
            dec = jnp.exp(_rows_to_cols(bcum[h][last:last + 1, :], DK_GLA))
            dec = jnp.concatenate([dec, dec], axis=1)
            sgla_ref[s, h] = s_old * dec + jnp.dot(kt_s, v[h], preferred_element_type=F32)
        y = _gla_finish(inner[h] + cross, p_ref, h, gngw_ref)
        off = H_RET * DV_RET + h * DV_GLA
        mix_ref[:, off:off + DV_GLA] = y.astype(mix_ref.dtype)


def _mixer_sample(proj, a_lr, sret0, sgla0, Bs, Ls, consts):
    n_seq = SAMPLE_GROUP
    rows = n_seq * Ls
    cosf, sinf, dmask, qdec, kdec, cdec, w2, b2, sums, pm, gnrw, gnrb, gngw = consts
    n_levels = pm.shape[0] - 1
    const2 = lambda g: (0, 0)
    const3 = lambda g: (0, 0, 0)
    kern = functools.partial(_mixer_sample_kernel, rows=rows, seq_len=Ls, n_levels=n_levels)
    st_spec = pl.BlockSpec((None, n_seq, H_RET, DK_RET, DV_RET), lambda g: (0, g, 0, 0, 0))
    return pl.pallas_call(
        kern,
        grid=(Bs // n_seq,),
        in_specs=[
            pl.BlockSpec((rows, D_IN_MAIN), lambda g: (g, 0)),
            pl.BlockSpec((rows, LANES), lambda g: (g, 0)),
            pl.BlockSpec(cosf.shape, const2),
            pl.BlockSpec(sinf.shape, const2),
            pl.BlockSpec(dmask.shape, const3),
            pl.BlockSpec(qdec.shape, const3),
            pl.BlockSpec(kdec.shape, const3),
            pl.BlockSpec(cdec.shape, const3),
            pl.BlockSpec(w2.shape, const2),
            pl.BlockSpec(b2.shape, const2),
            pl.BlockSpec(sums.shape, const2),
            pl.BlockSpec(pm.shape, const3),
            pl.BlockSpec(gnrw.shape, const2),
            pl.BlockSpec(gnrb.shape, const2),
            pl.BlockSpec(gngw.shape, const2),
            st_spec,
            st_spec,
        ],
        out_specs=[
            pl.BlockSpec((rows, D_MIX), lambda g: (g, 0)),
            st_spec,
            st_spec,
        ],
        out_shape=[
            jax.ShapeDtypeStruct((Bs * Ls, D_MIX), BF16),
            jax.ShapeDtypeStruct((1, Bs, H_RET, DK_RET, DV_RET), F32),
            jax.ShapeDtypeStruct((1, Bs, H_GLA, DK_GLA, DV_GLA), F32),
        ],
        compiler_params=_cparams(("parallel",)),
        name="mixer_sample",
    )(proj, a_lr, cosf, sinf, dmask, qdec, kdec, cdec, w2, b2, sums, pm, gnrw, gnrb, gngw, sret0, sgla0)


def _out_query_kernel(mix_ref, x_ref, wo_ref, wn_ref, wq_ref, k1_ref, k2_ref,
                      x1_ref, h2_ref, s1_ref, s2_ref):
    x1 = x_ref[...] + jnp.dot(mix_ref[...], wo_ref[...], preferred_element_type=F32)
    x1_ref[...] = x1
    ms = jnp.mean(x1 * x1, axis=-1, keepdims=True)
    h2 = (x1 * lax.rsqrt(ms + EPS) * wn_ref[...]).astype(BF16)
    h2_ref[...] = h2
    q = jnp.dot(h2, wq_ref[...], preferred_element_type=F32).astype(BF16)
    half = PEER_DQ // 2
    for h in range(PEER_HEADS):
        q1 = q[:, h * PEER_DQ:h * PEER_DQ + half]
        q2 = q[:, h * PEER_DQ + half:(h + 1) * PEER_DQ]
        s1_ref[h] = _dot_nt(k1_ref[h], q1)
        s2_ref[h] = _dot_nt(k2_ref[h], q2)


def _out_query(mix, x, wo, wn, wq, k1, k2, tm):
    T, D = x.shape
    resident = dict(pipeline_mode=pl.Buffered(1))
    return pl.pallas_call(
        _out_query_kernel,
        grid=(T // tm,),
        in_specs=[
            pl.BlockSpec((tm, D_MIX), lambda i: (i, 0)),
            pl.BlockSpec((tm, D), lambda i: (i, 0)),
            pl.BlockSpec(wo.shape, lambda i: (0, 0), **resident),
            pl.BlockSpec((1, D), lambda i: (0, 0)),
            pl.BlockSpec(wq.shape, lambda i: (0, 0), **resident),
            pl.BlockSpec(k1.shape, lambda i: (0, 0, 0)),
            pl.BlockSpec(k2.shape, lambda i: (0, 0, 0)),
        ],
        out_specs=[
            pl.BlockSpec((tm, D), lambda i: (i, 0)),
            pl.BlockSpec((tm, D), lambda i: (i, 0)),
            pl.BlockSpec((PEER_HEADS, PEER_NKEYS, tm), lambda i: (0, 0, i)),
            pl.BlockSpec((PEER_HEADS, PEER_NKEYS, tm), lambda i: (0, 0, i)),
        ],
        out_shape=[
            jax.ShapeDtypeStruct((T, D), F32),
            jax.ShapeDtypeStruct((T, D), BF16),
            jax.ShapeDtypeStruct((PEER_HEADS, PEER_NKEYS, T), F32),
            jax.ShapeDtypeStruct((PEER_HEADS, PEER_NKEYS, T), F32),
        ],
        compiler_params=_cparams(("parallel",)),
        name="out_proj_peer_query",
    )(mix, x, wo, wn, wq, k1, k2)


def _top_values(s, k, with_rank=False):
    vals = []
    rank = jnp.full(s.shape, float(k), F32) if with_rank else None
    for i in range(k):
        m = jnp.max(s, axis=0, keepdims=True)
        vals.append(m)
        hit = s == m
        if with_rank:
            rank = jnp.where(hit, float(i), rank)
        s = jnp.where(hit, -jnp.inf, s)
    return vals, rank


def _candidate_sums(v1, v2, stack_ref):
    k, half = PEER_TOPK, PEER_TOPK // 2
    for i in range(k):
        stack_ref[0, i:i + 1, :] = v1[i]
        stack_ref[1, i:i + 1, :] = v2[i]
    v1s = stack_ref[0]
    v2s = stack_ref[1]
    pieces = [v1[0] + v2s] + [v1[a] + v2s[:half] for a in range(1, half)] + [v1s[half:] + v2[0]]
    return jnp.concatenate(pieces, axis=0)


def _selected_count(s1, v2, tau):
    k = len(v2)
    steps = [k >> (i + 1) for i in range(k.bit_length() - 1)]
    preds = []

    def threshold(step, i=0, base=0):
        if i == len(preds):
            return v2[base + step - 1]
        return jnp.where(preds[i], threshold(step, i + 1, base + steps[i]), threshold(step, i + 1, base))

    for step in steps:
        preds.append(s1 + threshold(step) >= tau)
    n = jnp.where(preds[0], float(steps[0]), 0.0)
    for p, step in zip(preds[1:], steps[1:]):
        n = n + jnp.where(p, float(step), 0.0)
    return jnp.where(s1 + v2[k - 1] >= tau, float(k), n)


def _route_weights_kernel(s1_ref, s2_ref, g_ref, n1_ref, e1_ref, r2_ref, e2_ref, stack_ref, *, lane_chunk):
    tt = s1_ref.shape[-1]
    for h in range(PEER_HEADS):
        s1 = s1_ref[h]
        s2 = s2_ref[h]
        v1, _ = _top_values(s1, PEER_TOPK)
        v2, rank2 = _top_values(s2, PEER_TOPK, with_rank=True)
        cand = _candidate_sums(v1, v2, stack_ref)
        tau = _top_values(cand, PEER_TOPK)[0][-1]
        z = jnp.sum(jnp.where(cand >= tau, jnp.exp(cand - (v1[0] + v2[0])), 0.0), axis=0, keepdims=True)
        n1_ref[h] = _selected_count(s1, v2, tau)
        e1_ref[h] = jnp.exp(s1 - v1[0]) * (0.5 / z)
        r2_ref[h] = rank2.astype(ROUTE_DTYPE)
        e2_ref[h] = jnp.exp(s2 - v2[0]).astype(ROUTE_DTYPE)

    def row_block(i1, carry):
        n1_rows = [n1_ref[h, pl.ds(i1, 1), :] for h in range(PEER_HEADS)]
        e1_rows = [e1_ref[h, pl.ds(i1, 1), :] for h in range(PEER_HEADS)]
        row0 = pl.multiple_of(i1 * PEER_NKEYS, PEER_NKEYS)
        for c0 in range(0, tt, lane_chunk):
            cs = slice(c0, c0 + lane_chunk)
            g = None
            for h in range(PEER_HEADS):
                r2 = r2_ref[h, :, cs]
                e2 = e2_ref[h, :, cs]
                n1c = jnp.broadcast_to(n1_rows[h][:, cs], e2.shape).astype(ROUTE_DTYPE)
                e1c = jnp.broadcast_to(e1_rows[h][:, cs], e2.shape).astype(ROUTE_DTYPE)
                term = jnp.where(r2 < n1c, e2, jnp.zeros_like(e2)) * e1c
                g = term if g is None else g + term
            g_ref[pl.ds(row0, PEER_NKEYS), cs] = g.astype(g_ref.dtype)
        return carry

    lax.fori_loop(0, PEER_NKEYS, row_block, 0)


def _route_weights(s1, s2, tt):
    H, K, T = s1.shape
    blk = pl.BlockSpec((H, K, tt), lambda i: (0, 0, i))
    kern = functools.partial(_route_weights_kernel, lane_chunk=min(tt, LANES))
    return pl.pallas_call(
        kern,
        grid=(T // tt,),
        in_specs=[blk, blk],
        out_specs=pl.BlockSpec((None, N_EXPERTS, tt), lambda i: (i, 0, 0)),
        out_shape=jax.ShapeDtypeStruct((T // tt, N_EXPERTS, tt), BF16),
        scratch_shapes=[
            pltpu.VMEM((H, K, tt), F32),
            pltpu.VMEM((H, K, tt), F32),
            pltpu.VMEM((H, K, tt), ROUTE_DTYPE),
            pltpu.VMEM((H, K, tt), ROUTE_DTYPE),
            pltpu.VMEM((2, PEER_TOPK, tt), F32),
        ],
        compiler_params=_cparams(("parallel",)),
        name="peer_route_weights",
    )(s1, s2)


def _peer_kernel(h2_ref, u_ref, vt_ref, g_ref, x1_ref, wn_ref, y_ref, acc_ref, w_ref, h2s_ref):
    e = pl.program_id(1)

    @pl.when(e == 0)
    def _():
        acc_ref[...] = jnp.zeros_like(acc_ref)
        h2s_ref[...] = h2_ref[...].astype(F32).T.astype(BF16)

    n_sub, _, sub = g_ref.shape
    for t in range(n_sub):
        cols = slice(t * sub, (t + 1) * sub)
        s = jnp.dot(u_ref[...].astype(BF16), h2s_ref[:, cols], preferred_element_type=F32)
        act = s * (1.0 + lax.erf(s * (2.0 ** -0.5)))
        w_ref[:, cols] = g_ref[t] * act.astype(BF16)
    acc_ref[...] += jnp.dot(vt_ref[...], w_ref[...], preferred_element_type=F32)

    @pl.when(e == pl.num_programs(1) - 1)
    def _():
        out = x1_ref[...] + acc_ref[...].T
        ms = jnp.mean(out * out, axis=-1, keepdims=True)
        y_ref[...] = out * lax.rsqrt(ms + EPS) * wn_ref[...]


def _peer(h2, u, vt_bf, g, x1, wn, tt):
    T, D = x1.shape
    nb, _, eb = vt_bf.shape
    sub = g.shape[2]
    return pl.pallas_call(
        _peer_kernel,
        grid=(T // tt, nb),
        in_specs=[
            pl.BlockSpec((tt, D), lambda i, e: (i, 0)),
            pl.BlockSpec((eb, D), lambda i, e: (e, 0)),
            pl.BlockSpec((None, D, eb), lambda i, e: (e, 0, 0)),
            pl.BlockSpec((tt // sub, eb, sub), lambda i, e: (i, e, 0)),
            pl.BlockSpec((tt, D), lambda i, e: (i, 0), pipeline_mode=pl.Buffered(1)),
            pl.BlockSpec((1, D), lambda i, e: (0, 0)),
        ],
        out_specs=pl.BlockSpec((tt, D), lambda i, e: (i, 0), pipeline_mode=pl.Buffered(1)),
        out_shape=jax.ShapeDtypeStruct((T, D), F32),
        scratch_shapes=[
            pltpu.VMEM((D, tt), F32),
            pltpu.VMEM((eb, tt), BF16),
            pltpu.VMEM((D, tt), BF16),
        ],
        compiler_params=_cparams(("parallel", "arbitrary")),
        name="peer_dense",
    )(h2, u, vt_bf, g, x1, wn)


def _block_transpose_kernel(v_ref, o_ref):
    o_ref[...] = v_ref[...].T.astype(o_ref.dtype)


def _value_blocks(v, eb):
    E, D = v.shape
    return pl.pallas_call(
        _block_transpose_kernel,
        grid=(E // eb,),
        in_specs=[pl.BlockSpec((eb, D), lambda i: (i, 0))],
        out_specs=pl.BlockSpec((None, D, eb), lambda i: (i, 0, 0)),
        out_shape=jax.ShapeDtypeStruct((E // eb, D, eb), BF16),
        compiler_params=_cparams(("parallel",)),
        name="peer_value_blocks",
    )(v)


def _token_tile(T, pref):
    t = min(pref, T)
    while T % t:
        t //= 2
    return t


def _ffn(mix, x2d, wo, wnf, wq, k1, k2, u, vt_bf, wn_final):
    T = x2d.shape[0]
    x1, h2, s1, s2 = _out_query(mix, x2d, wo, wnf, wq, k1, k2, _token_tile(T, QUERY_TOKEN_TILE))
    g = _route_weights(s1, s2, _token_tile(T, ROUTE_TOKEN_TILE))
    return _peer(h2, u, vt_bf, g, x1, wn_final, _token_tile(T, PEER_TOKEN_TILE))


def kernel(x_prompt, x_sample, state_ret, state_gla, w_norm_mix, w_in, w_gla_a2, b_gla_a, gn_ret_w, gn_ret_b, gn_gla_w, w_out, w_norm_ffn, w_pq, sub_keys1, sub_keys2, u_tab, v_tab, w_norm_final):
    Bp, Lp, D = x_prompt.shape
    Bs, Ls, _ = x_sample.shape
    depth = w_in.shape[0]
    assert depth == 1 and D == D_MODEL
    l = 0
    xp = x_prompt.reshape(Bp * Lp, D)
    xs = x_sample.reshape(Bs * Ls, D)

    w_main = w_in[l, :, :D_IN_MAIN].astype(BF16)
    w_gate = jnp.pad(w_in[l, :, D_IN_MAIN:], ((0, 0), (0, LANES - GLA_RANK))).astype(BF16)
    w2 = jnp.pad(w_gla_a2[l], ((0, LANES - GLA_RANK), (0, 0))).astype(BF16)
    b2 = b_gla_a[l].reshape(1, -1)
    gnrw = gn_ret_w[l].reshape(1, -1)
    gnrb = gn_ret_b[l].reshape(1, -1)
    gngw = gn_gla_w[l].reshape(1, -1)
    wnm = w_norm_mix[l].reshape(1, D)
    wnf = w_norm_ffn[l].reshape(1, D)
    wn_final = w_norm_final.reshape(1, D)
    wo = w_out[l].astype(BF16)
    wq = w_pq[l].astype(BF16)
    k1 = sub_keys1[l].astype(BF16)
    k2 = sub_keys2[l].astype(BF16)
    u = u_tab[l]
    vt_bf = _value_blocks(v_tab[l], PEER_EXPERT_BLOCK)

    cos_p, sin_p = _rope_tables(jnp.arange(Lp, dtype=F32))
    pos_s = PAST_LEN + jnp.arange(Ls, dtype=F32)
    cos_s, sin_s = _rope_tables(jnp.tile(pos_s, SAMPLE_GROUP))
    sums_p, pm_p = _gla_consts(PROMPT_CHUNK, PROMPT_CHUNK)
    sums_s, pm_s = _gla_consts(SAMPLE_GROUP * Ls, Ls)
    consts_p = (cos_p, sin_p) + _ret_consts(PROMPT_CHUNK, PROMPT_CHUNK) + (
        w2, b2, jnp.asarray(sums_p, BF16), jnp.asarray(pm_p), gnrw, gnrb, gngw)
    consts_s = (cos_s, sin_s) + _ret_consts(SAMPLE_GROUP * Ls, Ls) + (
        w2, b2, jnp.asarray(sums_s, BF16), jnp.asarray(pm_s), gnrw, gnrb, gngw)

    proj_p, a_p = _norm_matmul(xp, wnm, w_main, w_gate, _token_tile(Bp * Lp, IN_PROJ_TOKEN_TILE), IN_PROJ_COL_TILE)
    proj_s, a_s = _norm_matmul(xs, wnm, w_main, w_gate, _token_tile(Bs * Ls, IN_PROJ_TOKEN_TILE), IN_PROJ_COL_TILE)
    mix_p, rp, gp = _mixer_prompt(proj_p, a_p, Bp, Lp, consts_p)
    mix_s, rs, gs = _mixer_sample(proj_s, a_s, state_ret, state_gla, Bs, Ls, consts_s)

    yp = _ffn(mix_p, xp, wo, wnf, wq, k1, k2, u, vt_bf, wn_final)
    ys = _ffn(mix_s, xs, wo, wnf, wq, k1, k2, u, vt_bf, wn_final)
    return (yp.reshape(Bp, Lp, D), ys.reshape(Bs, Ls, D), rp[None], gp[None], rs, gs)
```

```python
import functools
import math

import numpy as np
import jax
import jax.numpy as jnp
from jax import lax
from jax.experimental import pallas as pl
from jax.experimental.pallas import tpu as pltpu

F32 = jnp.float32
BF16 = jnp.bfloat16

D_MODEL = 2048
H_RET, DK_RET, DV_RET = 4, 128, 256
H_GLA, DK_GLA, DV_GLA = 4, 128, 256
GLA_RANK = 16
GLA_TEMP = 16.0
ROPE_BASE = 10000.0
PAST_LEN = 16384
PEER_HEADS, PEER_NKEYS, PEER_DQ, PEER_TOPK = 8, 128, 256, 16
N_EXPERTS = PEER_NKEYS * PEER_NKEYS
EPS = 1e-6

LANES = 128
D_MIX = H_RET * DV_RET + H_GLA * DV_GLA
D_IN_MAIN = 2 * H_RET * DK_RET + 2 * H_RET * DV_RET + 2 * H_GLA * DK_GLA + 2 * H_GLA * DV_GLA
OFF_QR = 0
OFF_KR = OFF_QR + H_RET * DK_RET
OFF_VR = OFF_KR + H_RET * DK_RET
OFF_GR = OFF_VR + H_RET * DV_RET
OFF_QG = OFF_GR + H_RET * DV_RET
OFF_KG = OFF_QG + H_GLA * DK_GLA
OFF_VG = OFF_KG + H_GLA * DK_GLA
OFF_GG = OFF_VG + H_GLA * DV_GLA

IN_PROJ_TOKEN_TILE = 1024
IN_PROJ_COL_TILE = 1536
QUERY_TOKEN_TILE = 256
ROUTE_TOKEN_TILE = 256
PEER_TOKEN_TILE = 512
PEER_EXPERT_BLOCK = 1024
PROMPT_CHUNK = 128
SAMPLE_GROUP = 8
VMEM_LIMIT = 56 * 1024 * 1024


def _cparams(sem):
    return pltpu.CompilerParams(dimension_semantics=sem, vmem_limit_bytes=VMEM_LIMIT)


def _norm_matmul_kernel(x_ref, wn_ref, w_ref, wa_ref, o_ref, oa_ref, h_ref):
    @pl.when(pl.program_id(1) == 0)
    def _():
        x = x_ref[...]
        ms = jnp.mean(x * x, axis=-1, keepdims=True)
        h = (x * lax.rsqrt(ms + EPS) * wn_ref[...]).astype(BF16)
        h_ref[...] = h
        oa_ref[...] = jnp.dot(h, wa_ref[...], preferred_element_type=F32)

    o_ref[...] = jnp.dot(h_ref[...], w_ref[...], preferred_element_type=F32)


def _norm_matmul(x, wn, w, wa, tm, tn):
    T, D = x.shape
    N = w.shape[1]
    Na = wa.shape[1]
    return pl.pallas_call(
        _norm_matmul_kernel,
        grid=(T // tm, N // tn),
        in_specs=[
            pl.BlockSpec((tm, D), lambda i, j: (i, 0)),
            pl.BlockSpec((1, D), lambda i, j: (0, 0)),
            pl.BlockSpec((D, tn), lambda i, j: (0, j)),
            pl.BlockSpec((D, Na), lambda i, j: (0, 0)),
        ],
        out_specs=[
            pl.BlockSpec((tm, tn), lambda i, j: (i, j)),
            pl.BlockSpec((tm, Na), lambda i, j: (i, 0)),
        ],
        out_shape=[jax.ShapeDtypeStruct((T, N), F32), jax.ShapeDtypeStruct((T, Na), F32)],
        scratch_shapes=[pltpu.VMEM((tm, D), BF16)],
        compiler_params=_cparams(("parallel", "arbitrary")),
        name="norm_in_proj",
    )(x, wn, w, wa)


def _gla_consts(rows, blk):
    idx = np.arange(rows)
    seq = idx // blk
    t = idx[None, :]
    i = idx[:, None]
    same = seq[:, None] == seq[None, :]
    mats = [same & (t <= i), same & (t > i)]
    pmasks = []
    s = blk
    while s >= 2:
        bstart = (idx // s) * s
        mid = bstart + s // 2
        upper = idx >= mid
        mats.append((upper[:, None] & (t >= mid[:, None]) & (t <= i))
                    | ((~upper)[:, None] & (t > i) & (t <= mid[:, None] - 1)))
        pmasks.append((bstart[:, None] == bstart[None, :]) & upper[:, None] & (~upper)[None, :])
        s //= 2
    pmasks.append(np.eye(rows, dtype=bool))
    return (np.concatenate(mats, 0).astype(np.float32), np.stack(pmasks).astype(np.float32))


def _ret_consts(rows, blk):
    log_g = jnp.log1p(-jnp.exp2(-5.0 - jnp.arange(H_RET, dtype=F32)))
    idx = np.arange(rows)
    p = jnp.asarray(idx % blk, F32)
    same = jnp.asarray((idx[:, None] // blk) == (idx[None, :] // blk))
    dist = p[:, None] - p[None, :]
    causal = same & (dist >= 0)
    dmask = jnp.where(causal[None], jnp.exp(jnp.maximum(dist, 0.0)[None] * log_g[:, None, None]), 0.0)
    qdec = jnp.exp((p + 1.0)[None, :] * log_g[:, None])
    kdec = jnp.exp((blk - 1.0 - p)[None, :] * log_g[:, None])
    cdec = jnp.exp(blk * log_g)
    qdec_full = jnp.broadcast_to(qdec[:, :, None], (H_RET, rows, DV_RET))
    kdec_full = jnp.broadcast_to(kdec[:, :, None], (H_RET, rows, DK_RET))
    cdec_full = jnp.broadcast_to(cdec[:, None, None], (H_RET, DK_RET, DV_RET))
    return dmask.astype(F32), qdec_full.astype(F32), kdec_full.astype(F32), cdec_full.astype(F32)


def _rope_tables(pos):
    d = DK_RET
    freqs = ROPE_BASE ** (-jnp.arange(0, d, 2, dtype=F32) / d)
    ang = pos[:, None] * freqs[None, :]
    cos = jnp.cos(ang)
    sin = jnp.sin(ang)
    return jnp.concatenate([cos, cos], axis=-1), jnp.concatenate([-sin, sin], axis=-1)


def _dot_nt(a, b):
    return lax.dot_general(a, b, (((1,), (1,)), ((), ())), preferred_element_type=F32)


def _rotary(x, cosf, sinf):
    return x * cosf + pltpu.roll(x, DK_RET // 2, axis=1) * sinf


def _sigmoid(x):
    return 1.0 / (1.0 + jnp.exp(-x))


def _ret_heads(p_ref, cosf, sinf, dmask_ref, kdec_ref):
    heads = range(H_RET)
    q = [_rotary(p_ref[:, OFF_QR + h * DK_RET:OFF_QR + (h + 1) * DK_RET], cosf, sinf) for h in heads]
    k = [_rotary(p_ref[:, OFF_KR + h * DK_RET:OFF_KR + (h + 1) * DK_RET], cosf, sinf) * (DK_RET ** -0.5)
         for h in heads]
    v = [p_ref[:, OFF_VR + h * DV_RET:OFF_VR + (h + 1) * DV_RET].astype(BF16) for h in heads]
    qb = [q[h].astype(BF16) for h in heads]
    scores = [_dot_nt(qb[h], k[h].astype(BF16)) * dmask_ref[h] for h in heads]
    inner = [jnp.dot(scores[h].astype(BF16), v[h], preferred_element_type=F32) for h in heads]
    kt = [(k[h] * kdec_ref[h]).T for h in heads]
    return qb, inner, kt, v


def _ret_finish(o, p_ref, h, gnw_ref, gnb_ref):
    mu = jnp.mean(o, axis=-1, keepdims=True)
    d = o - mu
    var = jnp.mean(d * d, axis=-1, keepdims=True)
    sl = slice(h * DV_RET, (h + 1) * DV_RET)
    y = d * lax.rsqrt(var + EPS) * gnw_ref[:, sl] + gnb_ref[:, sl]
    g = p_ref[:, OFF_GR + h * DV_RET:OFF_GR + (h + 1) * DV_RET]
    return y * (g * _sigmoid(g))


def _gla_log_decay(a_ref, w2_ref, b2_ref):
    a = a_ref[...].astype(BF16)
    z = jnp.dot(a, w2_ref[...], preferred_element_type=F32) + b2_ref[...]
    return (jnp.minimum(z, 0.0) - jnp.log1p(jnp.exp(-jnp.abs(z)))) * (1.0 / GLA_TEMP)


def _gla_heads(p_ref, la, sums_ref, pm_ref, rows, n_levels):
    heads = range(H_GLA)
    hilo = []
    for h in heads:
        la_h = la[:, h * DK_GLA:(h + 1) * DK_GLA]
        hi = la_h.astype(BF16)
        hilo += [hi, (la_h - hi.astype(F32)).astype(BF16)]
    e2 = jnp.dot(sums_ref[...], jnp.concatenate(hilo, axis=1), preferred_element_type=F32)
    ex = [e2[:, 2 * h * DK_GLA:(2 * h + 1) * DK_GLA] + e2[:, (2 * h + 1) * DK_GLA:(2 * h + 2) * DK_GLA]
          for h in heads]
    q = [p_ref[:, OFF_QG + h * DK_GLA:OFF_QG + (h + 1) * DK_GLA] * (DK_GLA ** -0.5) for h in heads]
    k = [p_ref[:, OFF_KG + h * DK_GLA:OFF_KG + (h + 1) * DK_GLA] for h in heads]
    v = [p_ref[:, OFF_VG + h * DV_GLA:OFF_VG + (h + 1) * DV_GLA].astype(BF16) for h in heads]
    amat = [_dot_nt(q[h].astype(BF16), k[h].astype(BF16)) * pm_ref[n_levels] for h in heads]
    for l in range(n_levels):
        for h in heads:
            dec = jnp.exp(ex[h][(2 + l) * rows:(3 + l) * rows])
            amat[h] = amat[h] + _dot_nt((q[h] * dec).astype(BF16), (k[h] * dec).astype(BF16)) * pm_ref[l]
    inner = [jnp.dot(amat[h].astype(BF16), v[h], preferred_element_type=F32) for h in heads]
    bcum = [ex[h][0:rows] for h in heads]
    qd = [(q[h] * jnp.exp(bcum[h])).astype(BF16) for h in heads]
    kt = [(k[h] * jnp.exp(ex[h][rows:2 * rows])).T for h in heads]
    return qd, inner, kt, v, bcum


def _gla_finish(o, p_ref, h, gw_ref):
    sl = slice(h * DV_GLA, (h + 1) * DV_GLA)
    y = o * lax.rsqrt(jnp.mean(o * o, axis=-1, keepdims=True) + EPS) * gw_ref[:, sl]
    g = p_ref[:, OFF_GG + h * DV_GLA:OFF_GG + (h + 1) * DV_GLA]
    return y * (g * _sigmoid(g))


def _rows_to_cols(row, n):
    return jnp.broadcast_to(row, (n, n)).T


def _mixer_prompt_kernel(p_ref, a_ref, cos_ref, sin_ref, dmask_ref, qdec_ref, kdec_ref, cdec_ref,
                         w2_ref, b2_ref, sums_ref, pm_ref, gnrw_ref, gnrb_ref, gngw_ref,
                         mix_ref, sret_ref, sgla_ref, *, rows, n_levels):
    @pl.when(pl.program_id(1) == 0)
    def _():
        sret_ref[...] = jnp.zeros_like(sret_ref)
        sgla_ref[...] = jnp.zeros_like(sgla_ref)

    cosf = cos_ref[...]
    sinf = sin_ref[...]
    rh = range(H_RET)
    qb, inner, kt, v = _ret_heads(p_ref, cosf, sinf, dmask_ref, kdec_ref)
    s_old = [sret_ref[0, h] for h in rh]
    cross = [jnp.dot(qb[h], s_old[h].astype(BF16), preferred_element_type=F32) * qdec_ref[h] for h in rh]
    upd = [jnp.dot(kt[h].astype(BF16), v[h], preferred_element_type=F32) for h in rh]
    for h in rh:
        sret_ref[0, h] = s_old[h] * cdec_ref[h] + upd[h]
    for h in rh:
        y = _ret_finish(inner[h] + cross[h], p_ref, h, gnrw_ref, gnrb_ref)
        mix_ref[:, h * DV_RET:(h + 1) * DV_RET] = y.astype(mix_ref.dtype)

    la = _gla_log_decay(a_ref, w2_ref, b2_ref)
    gh = range(H_GLA)
    qd, inner, kt, v, bcum = _gla_heads(p_ref, la, sums_ref, pm_ref, rows, n_levels)
    g_old = [sgla_ref[0, h] for h in gh]
    cross = [jnp.dot(qd[h], g_old[h].astype(BF16), preferred_element_type=F32) for h in gh]
    upd = [jnp.dot(kt[h].astype(BF16), v[h], preferred_element_type=F32) for h in gh]
    for h in gh:
        dec = jnp.exp(_rows_to_cols(bcum[h][rows - 1:rows, :], DK_GLA))
        sgla_ref[0, h] = g_old[h] * jnp.concatenate([dec, dec], axis=1) + upd[h]
    for h in gh:
        y = _gla_finish(inner[h] + cross[h], p_ref, h, gngw_ref)
        off = H_RET * DV_RET + h * DV_GLA
        mix_ref[:, off:off + DV_GLA] = y.astype(mix_ref.dtype)


def _mixer_prompt(proj, a_lr, B, L, consts):
    rows = PROMPT_CHUNK
    nc = L // rows
    cosf, sinf, dmask, qdec, kdec, cdec, w2, b2, sums, pm, gnrw, gnrb, gngw = consts
    n_levels = pm.shape[0] - 1
    const2 = lambda b, c: (0, 0)
    const3 = lambda b, c: (0, 0, 0)
    kern = functools.partial(_mixer_prompt_kernel, rows=rows, n_levels=n_levels)
    return pl.pallas_call(
        kern,
        grid=(B, nc),
        in_specs=[
            pl.BlockSpec((rows, D_IN_MAIN), lambda b, c: (b * nc + c, 0)),
            pl.BlockSpec((rows, LANES), lambda b, c: (b * nc + c, 0)),
            pl.BlockSpec((rows, DK_RET), lambda b, c: (c, 0)),
            pl.BlockSpec((rows, DK_RET), lambda b, c: (c, 0)),
            pl.BlockSpec(dmask.shape, const3),
            pl.BlockSpec(qdec.shape, const3),
            pl.BlockSpec(kdec.shape, const3),
            pl.BlockSpec(cdec.shape, const3),
            pl.BlockSpec(w2.shape, const2),
            pl.BlockSpec(b2.shape, const2),
            pl.BlockSpec(sums.shape, const2),
            pl.BlockSpec(pm.shape, const3),
            pl.BlockSpec(gnrw.shape, const2),
            pl.BlockSpec(gnrb.shape, const2),
            pl.BlockSpec(gngw.shape, const2),
        ],
        out_specs=[
            pl.BlockSpec((rows, D_MIX), lambda b, c: (b * nc + c, 0)),
            pl.BlockSpec((1, H_RET, DK_RET, DV_RET), lambda b, c: (b, 0, 0, 0)),
            pl.BlockSpec((1, H_GLA, DK_GLA, DV_GLA), lambda b, c: (b, 0, 0, 0)),
        ],
        out_shape=[
            jax.ShapeDtypeStruct((B * L, D_MIX), BF16),
            jax.ShapeDtypeStruct((B, H_RET, DK_RET, DV_RET), F32),
            jax.ShapeDtypeStruct((B, H_GLA, DK_GLA, DV_GLA), F32),
        ],
        compiler_params=_cparams(("parallel", "arbitrary")),
        name="mixer_prompt",
    )(proj, a_lr, cosf, sinf, dmask, qdec, kdec, cdec, w2, b2, sums, pm, gnrw, gnrb, gngw)


def _mixer_sample_kernel(p_ref, a_ref, cos_ref, sin_ref, dmask_ref, qdec_ref, kdec_ref, cdec_ref,
                         w2_ref, b2_ref, sums_ref, pm_ref, gnrw_ref, gnrb_ref, gngw_ref,
                         sret0_ref, sgla0_ref,
                         mix_ref, sret_ref, sgla_ref, *, rows, seq_len, n_levels):
    n_seq = rows // seq_len
    row_seq = lax.broadcasted_iota(jnp.int32, (rows, 1), 0) // seq_len
    col_seq = lax.broadcasted_iota(jnp.int32, (1, rows), 1) // seq_len
    cosf = cos_ref[...]
    sinf = sin_ref[...]
    qb, inner, kt, v = _ret_heads(p_ref, cosf, sinf, dmask_ref, kdec_ref)
    for h in range(H_RET):
        cross = jnp.zeros((rows, DV_RET), F32)
        for s in range(n_seq):
            s_old = sret0_ref[s, h]
            c_s = jnp.dot(qb[h], s_old.astype(BF16), preferred_element_type=F32)
            cross = cross + jnp.where(row_seq == s, c_s, 0.0)
            kt_s = jnp.where(col_seq == s, kt[h], 0.0).astype(BF16)
            sret_ref[s, h] = s_old * cdec_ref[h] + jnp.dot(kt_s, v[h], preferred_element_type=F32)
        y = _ret_finish(inner[h] + cross * qdec_ref[h], p_ref, h, gnrw_ref, gnrb_ref)
        mix_ref[:, h * DV_RET:(h + 1) * DV_RET] = y.astype(mix_ref.dtype)

    la = _gla_log_decay(a_ref, w2_ref, b2_ref)
    qd, inner, kt, v, bcum = _gla_heads(p_ref, la, sums_ref, pm_ref, rows, n_levels)
    for h in range(H_GLA):
        cross = jnp.zeros((rows, DV_GLA), F32)
        for s in range(n_seq):
            s_old = sgla0_ref[s, h]
            c_s = jnp.dot(qd[h], s_old.astype(BF16), preferred_element_type=F32)
            cross = cross + jnp.where(row_seq == s, c_s, 0.0)
            kt_s = jnp.where(col_seq == s, kt[h], 0.0).astype(BF16)
            last = (s + 1) * seq_len - 1
            dec = jnp.exp(_rows_to_cols(bcum[h][last:last + 1, :], DK_GLA))
            dec = jnp.concatenate([dec, dec], axis=1)
            sgla_ref[s, h] = s_old * dec + jnp.dot(kt_s, v[h], preferred_element_type=F32)
        y = _gla_finish(inner[h] + cross, p_ref, h, gngw_ref)
        off = H_RET * DV_RET + h * DV_GLA
        mix_ref[:, off:off + DV_GLA] = y.astype(mix_ref.dtype)


def _mixer_sample(proj, a_lr, sret0, sgla0, Bs, Ls, consts):
    n_seq = SAMPLE_GROUP
    rows = n_seq * Ls
    cosf, sinf, dmask, qdec, kdec, cdec, w2, b2, sums, pm, gnrw, gnrb, gngw = consts
    n_levels = pm.shape[0] - 1
    const2 = lambda g: (0, 0)
    const3 = lambda g: (0, 0, 0)
    kern = functools.partial(_mixer_sample_kernel, rows=rows, seq_len=Ls, n_levels=n_levels)
    st_spec = pl.BlockSpec((None, n_seq, H_RET, DK_RET, DV_RET), lambda g: (0, g, 0, 0, 0))
    return pl.pallas_call(
        kern,
        grid=(Bs // n_seq,),
        in_specs=[
            pl.BlockSpec((rows, D_IN_MAIN), lambda g: (g, 0)),
            pl.BlockSpec((rows, LANES), lambda g: (g, 0)),
            pl.BlockSpec(cosf.shape, const2),
            pl.BlockSpec(sinf.shape, const2),
            pl.BlockSpec(dmask.shape, const3),
            pl.BlockSpec(qdec.shape, const3),
            pl.BlockSpec(kdec.shape, const3),
            pl.BlockSpec(cdec.shape, const3),
            pl.BlockSpec(w2.shape, const2),
            pl.BlockSpec(b2.shape, const2),
            pl.BlockSpec(sums.shape, const2),
            pl.BlockSpec(pm.shape, const3),
            pl.BlockSpec(gnrw.shape, const2),
            pl.BlockSpec(gnrb.shape, const2),
            pl.BlockSpec(gngw.shape, const2),
            st_spec,
            st_spec,
        ],
        out_specs=[
            pl.BlockSpec((rows, D_MIX), lambda g: (g, 0)),
            st_spec,
            st_spec,
        ],
        out_shape=[
            jax.ShapeDtypeStruct((Bs * Ls, D_MIX), BF16),
            jax.ShapeDtypeStruct((1, Bs, H_RET, DK_RET, DV_RET), F32),
            jax.ShapeDtypeStruct((1, Bs, H_GLA, DK_GLA, DV_GLA), F32),
        ],
        compiler_params=_cparams(("parallel",)),
        name="mixer_sample",
    )(proj, a_lr, cosf, sinf, dmask, qdec, kdec, cdec, w2, b2, sums, pm, gnrw, gnrb, gngw, sret0, sgla0)


def _out_query_kernel(mix_ref, x_ref, wo_ref, wn_ref, wq_ref, k1_ref, k2_ref,
                      x1_ref, h2_ref, s1_ref, s2_ref):
    x1 = x_ref[...] + jnp.dot(mix_ref[...], wo_ref[...], preferred_element_type=F32)
    x1_ref[...] = x1
    ms = jnp.mean(x1 * x1, axis=-1, keepdims=True)
    h2 = (x1 * lax.rsqrt(ms + EPS) * wn_ref[...]).astype(BF16)
    h2_ref[...] = h2
    q = jnp.dot(h2, wq_ref[...], preferred_element_type=F32).astype(BF16)
    half = PEER_DQ // 2
    for h in range(PEER_HEADS):
        q1 = q[:, h * PEER_DQ:h * PEER_DQ + half]
        q2 = q[:, h * PEER_DQ + half:(h + 1) * PEER_DQ]
        s1_ref[h] = _dot_nt(k1_ref[h], q1)
        s2_ref[h] = _dot_nt(k2_ref[h], q2)


def _out_query(mix, x, wo, wn, wq, k1, k2, tm):
    T, D = x.shape
    resident = dict(pipeline_mode=pl.Buffered(1))
    return pl.pallas_call(
        _out_query_kernel,
        grid=(T // tm,),
        in_specs=[
            pl.BlockSpec((tm, D_MIX), lambda i: (i, 0)),
            pl.BlockSpec((tm, D), lambda i: (i, 0)),
            pl.BlockSpec(wo.shape, lambda i: (0, 0), **resident),
            pl.BlockSpec((1, D), lambda i: (0, 0)),
            pl.BlockSpec(wq.shape, lambda i: (0, 0), **resident),
            pl.BlockSpec(k1.shape, lambda i: (0, 0, 0)),
            pl.BlockSpec(k2.shape, lambda i: (0, 0, 0)),
        ],
        out_specs=[
            pl.BlockSpec((tm, D), lambda i: (i, 0)),
            pl.BlockSpec((tm, D), lambda i: (i, 0)),
            pl.BlockSpec((PEER_HEADS, PEER_NKEYS, tm), lambda i: (0, 0, i)),
            pl.BlockSpec((PEER_HEADS, PEER_NKEYS, tm), lambda i: (0, 0, i)),
        ],
        out_shape=[
            jax.ShapeDtypeStruct((T, D), F32),
            jax.ShapeDtypeStruct((T, D), BF16),
            jax.ShapeDtypeStruct((PEER_HEADS, PEER_NKEYS, T), F32),
            jax.ShapeDtypeStruct((PEER_HEADS, PEER_NKEYS, T), F32),
        ],
        compiler_params=_cparams(("parallel",)),
        name="out_proj_peer_query",
    )(mix, x, wo, wn, wq, k1, k2)


SUBLANES = 8


def _compare_exchange(a, i, l, descending=True):
    hi, lo = jnp.maximum(a[i], a[l]), jnp.minimum(a[i], a[l])
    a[i], a[l] = (hi, lo) if descending else (lo, hi)


def _bitonic_merge(a):
    n = len(a)
    j = n // 2
    while j >= 1:
        for i in range(n):
            if i ^ j > i:
                _compare_exchange(a, i, i ^ j)
        j //= 2


def _sorted_top(s):
    n = s.shape[0] // SUBLANES
    a = [s[SUBLANES * j:SUBLANES * (j + 1), :] for j in range(n)]
    k = 2
    while k <= n:
        j = k // 2
        while j >= 1:
            for i in range(n):
                if i ^ j > i:
                    _compare_exchange(a, i, i ^ j, descending=(i & k) == 0)
            j //= 2
        k *= 2
    shift = SUBLANES // 2
    while shift >= 1:
        other = [pltpu.roll(x, shift, axis=0) for x in a]
        a = [jnp.maximum(a[i], other[n - 1 - i]) for i in range(n)]
        _bitonic_merge(a)
        shift //= 2
    return [x[0:1, :] for x in a]


def _kth_largest(x, k):
    for _ in range(k - 1):
        x = jnp.where(x == jnp.max(x, axis=0, keepdims=True), -jnp.inf, x)
    return jnp.max(x, axis=0, keepdims=True)


def _candidate_sums(v1, v2, stack_ref):
    k, half = PEER_TOPK, PEER_TOPK // 2
    for i in range(k):
        stack_ref[0, i:i + 1, :] = v1[i]
        stack_ref[1, i:i + 1, :] = v2[i]
    v1s = stack_ref[0]
    v2s = stack_ref[1]
    pieces = [v1[0] + v2s] + [v1[a] + v2s[:half] for a in range(1, half)] + [v1s[half:] + v2[0]]
    return jnp.concatenate(pieces, axis=0)


def _selection_threshold(s1, v2, tau):
    k = len(v2)
    steps = [k >> (i + 1) for i in range(k.bit_length() - 1)]
    preds = []

    def candidate(step, i=0, base=0):
        if i == len(preds):
            return v2[base + step - 1]
        return jnp.where(preds[i], candidate(step, i + 1, base + steps[i]), candidate(step, i + 1, base))

    last = jnp.full(s1.shape, jnp.inf, F32)
    for step in steps + [None]:
        thr = v2[k - 1] if step is None else candidate(step)
        passed = s1 + thr >= tau
        last = jnp.where(passed, thr, last)
        preds.append(passed)
    return last


def _route_weights_kernel(s1_ref, s2_ref, g_ref, th_ref, e1_ref, e2_ref, stack_ref, *, lane_chunk):
    tt = s1_ref.shape[-1]
    for h in range(PEER_HEADS):
        for c0 in range(0, tt, lane_chunk):
            cs = slice(c0, c0 + lane_chunk)
            s1 = s1_ref[h, :, cs]
            s2 = s2_ref[h, :, cs]
            v1 = _sorted_top(s1)
            v2 = _sorted_top(s2)
            cand = _candidate_sums(v1, v2, stack_ref.at[:, :, cs])
            tau = _kth_largest(cand, PEER_TOPK)
            z = jnp.sum(jnp.where(cand >= tau, jnp.exp(cand - (v1[0] + v2[0])), 0.0), axis=0, keepdims=True)
            th_ref[h, :, cs] = _selection_threshold(s1, v2, tau)
            e1_ref[h, :, cs] = jnp.exp(s1 - v1[0]) * (0.5 / z)
            e2_ref[h, :, cs] = jnp.exp(s2 - v2[0])

    def row_block(i1, carry):
        th_rows = [th_ref[h, pl.ds(i1, 1), :] for h in range(PEER_HEADS)]
        e1_rows = [e1_ref[h, pl.ds(i1, 1), :] for h in range(PEER_HEADS)]
        row0 = pl.multiple_of(i1 * PEER_NKEYS, PEER_NKEYS)
        for c0 in range(0, tt, lane_chunk):
            cs = slice(c0, c0 + lane_chunk)
            g = None
            for h in range(PEER_HEADS):
                e2 = e2_ref[h, :, cs]
                term = jnp.where(s2_ref[h, :, cs] >= th_rows[h][:, cs], e2, jnp.zeros_like(e2)) * e1_rows[h][:, cs]
                g = term if g is None else g + term
            g_ref[pl.ds(row0, PEER_NKEYS), cs] = g.astype(g_ref.dtype)
        return carry

    lax.fori_loop(0, PEER_NKEYS, row_block, 0)


def _route_weights(s1, s2, tt):
    H, K, T = s1.shape
    blk = pl.BlockSpec((H, K, tt), lambda i: (0, 0, i))
    kern = functools.partial(_route_weights_kernel, lane_chunk=min(tt, LANES))
    return pl.pallas_call(
        kern,
        grid=(T // tt,),
        in_specs=[blk, blk],
        out_specs=pl.BlockSpec((None, N_EXPERTS, tt), lambda i: (i, 0, 0)),
        out_shape=jax.ShapeDtypeStruct((T // tt, N_EXPERTS, tt), BF16),
        scratch_shapes=[
            pltpu.VMEM((H, K, tt), F32),
            pltpu.VMEM((H, K, tt), F32),
            pltpu.VMEM((H, K, tt), F32),
            pltpu.VMEM((2, PEER_TOPK, tt), F32),
        ],
        compiler_params=_cparams(("parallel",)),
        name="peer_route_weights",
    )(s1, s2)


def _peer_kernel(h2_ref, u_ref, vt_ref, g_ref, x1_ref, wn_ref, y_ref, acc_ref, w_ref, h2s_ref):
    e = pl.program_id(1)

    @pl.when(e == 0)
    def _():
        acc_ref[...] = jnp.zeros_like(acc_ref)
        h2s_ref[...] = h2_ref[...].astype(F32).T.astype(BF16)

    n_sub, _, sub = g_ref.shape
    for t in range(n_sub):
        cols = slice(t * sub, (t + 1) * sub)
        s = jnp.dot(u_ref[...].astype(BF16), h2s_ref[:, cols], preferred_element_type=F32)
        act = s * (1.0 + lax.erf(s * (2.0 ** -0.5)))
        w_ref[:, cols] = g_ref[t] * act.astype(BF16)
    acc_ref[...] += jnp.dot(vt_ref[...], w_ref[...], preferred_element_type=F32)

    @pl.when(e == pl.num_programs(1) - 1)
    def _():
        out = x1_ref[...] + acc_ref[...].T
        ms = jnp.mean(out * out, axis=-1, keepdims=True)
        y_ref[...] = out * lax.rsqrt(ms + EPS) * wn_ref[...]


def _peer(h2, u, vt_bf, g, x1, wn, tt):
    T, D = x1.shape
    nb, _, eb = vt_bf.shape
    sub = g.shape[2]
    return pl.pallas_call(
        _peer_kernel,
        grid=(T // tt, nb),
        in_specs=[
            pl.BlockSpec((tt, D), lambda i, e: (i, 0)),
            pl.BlockSpec((eb, D), lambda i, e: (e, 0)),
            pl.BlockSpec((None, D, eb), lambda i, e: (e, 0, 0)),
            pl.BlockSpec((tt // sub, eb, sub), lambda i, e: (i, e, 0)),
            pl.BlockSpec((tt, D), lambda i, e: (i, 0), pipeline_mode=pl.Buffered(1)),
            pl.BlockSpec((1, D), lambda i, e: (0, 0)),
        ],
        out_specs=pl.BlockSpec((tt, D), lambda i, e: (i, 0), pipeline_mode=pl.Buffered(1)),
        out_shape=jax.ShapeDtypeStruct((T, D), F32),
        scratch_shapes=[
            pltpu.VMEM((D, tt), F32),
            pltpu.VMEM((eb, tt), BF16),
            pltpu.VMEM((D, tt), BF16),
        ],
        compiler_params=_cparams(("parallel", "arbitrary")),
        name="peer_dense",
    )(h2, u, vt_bf, g, x1, wn)


def _block_transpose_kernel(v_ref, o_ref):
    o_ref[...] = v_ref[...].T.astype(o_ref.dtype)


def _value_blocks(v, eb):
    E, D = v.shape
    return pl.pallas_call(
        _block_transpose_kernel,
        grid=(E // eb,),
        in_specs=[pl.BlockSpec((eb, D), lambda i: (i, 0))],
        out_specs=pl.BlockSpec((None, D, eb), lambda i: (i, 0, 0)),
        out_shape=jax.ShapeDtypeStruct((E // eb, D, eb), BF16),
        compiler_params=_cparams(("parallel",)),
        name="peer_value_blocks",
    )(v)


def _token_tile(T, pref):
    t = min(pref, T)
    while T % t:
        t //= 2
    return t


def _ffn(mix, x2d, wo, wnf, wq, k1, k2, u, vt_bf, wn_final):
    T = x2d.shape[0]
    x1, h2, s1, s2 = _out_query(mix, x2d, wo, wnf, wq, k1, k2, _token_tile(T, QUERY_TOKEN_TILE))
    g = _route_weights(s1, s2, _token_tile(T, ROUTE_TOKEN_TILE))
    return _peer(h2, u, vt_bf, g, x1, wn_final, _token_tile(T, PEER_TOKEN_TILE))


def kernel(x_prompt, x_sample, state_ret, state_gla, w_norm_mix, w_in, w_gla_a2, b_gla_a, gn_ret_w, gn_ret_b, gn_gla_w, w_out, w_norm_ffn, w_pq, sub_keys1, sub_keys2, u_tab, v_tab, w_norm_final):
    Bp, Lp, D = x_prompt.shape
    Bs, Ls, _ = x_sample.shape
    depth = w_in.shape[0]
    assert depth == 1 and D == D_MODEL
    l = 0
    xp = x_prompt.reshape(Bp * Lp, D)
    xs = x_sample.reshape(Bs * Ls, D)

    w_main = w_in[l, :, :D_IN_MAIN].astype(BF16)
    w_gate = jnp.pad(w_in[l, :, D_IN_MAIN:], ((0, 0), (0, LANES - GLA_RANK))).astype(BF16)
    w2 = jnp.pad(w_gla_a2[l], ((0, LANES - GLA_RANK), (0, 0))).astype(BF16)
    b2 = b_gla_a[l].reshape(1, -1)
    gnrw = gn_ret_w[l].reshape(1, -1)
    gnrb = gn_ret_b[l].reshape(1, -1)
    gngw = gn_gla_w[l].reshape(1, -1)
    wnm = w_norm_mix[l].reshape(1, D)
    wnf = w_norm_ffn[l].reshape(1, D)
    wn_final = w_norm_final.reshape(1, D)
    wo = w_out[l].astype(BF16)
    wq = w_pq[l].astype(BF16)
    k1 = sub_keys1[l].astype(BF16)
    k2 = sub_keys2[l].astype(BF16)
    u = u_tab[l]
    vt_bf = _value_blocks(v_tab[l], PEER_EXPERT_BLOCK)

    cos_p, sin_p = _rope_tables(jnp.arange(Lp, dtype=F32))
    pos_s = PAST_LEN + jnp.arange(Ls, dtype=F32)
    cos_s, sin_s = _rope_tables(jnp.tile(pos_s, SAMPLE_GROUP))
    sums_p, pm_p = _gla_consts(PROMPT_CHUNK, PROMPT_CHUNK)
    sums_s, pm_s = _gla_consts(SAMPLE_GROUP * Ls, Ls)
    consts_p = (cos_p, sin_p) + _ret_consts(PROMPT_CHUNK, PROMPT_CHUNK) + (
        w2, b2, jnp.asarray(sums_p, BF16), jnp.asarray(pm_p), gnrw, gnrb, gngw)
    consts_s = (cos_s, sin_s) + _ret_consts(SAMPLE_GROUP * Ls, Ls) + (
        w2, b2, jnp.asarray(sums_s, BF16), jnp.asarray(pm_s), gnrw, gnrb, gngw)

    proj_p, a_p = _norm_matmul(xp, wnm, w_main, w_gate, _token_tile(Bp * Lp, IN_PROJ_TOKEN_TILE), IN_PROJ_COL_TILE)
    proj_s, a_s = _norm_matmul(xs, wnm, w_main, w_gate, _token_tile(Bs * Ls, IN_PROJ_TOKEN_TILE), IN_PROJ_COL_TILE)
    mix_p, rp, gp = _mixer_prompt(proj_p, a_p, Bp, Lp, consts_p)
    mix_s, rs, gs = _mixer_sample(proj_s, a_s, state_ret, state_gla, Bs, Ls, consts_s)

    yp = _ffn(mix_p, xp, wo, wnf, wq, k1, k2, u, vt_bf, wn_final)
    ys = _ffn(mix_s, xs, wo, wnf, wq, k1, k2, u, vt_bf, wn_final)
    return (yp.reshape(Bp, Lp, D), ys.reshape(Bs, Ls, D), rp[None], gp[None], rs, gs)
```

```python
import functools
import math

import numpy as np
import jax
import jax.numpy as jnp
from jax import lax
from jax.experimental import pallas as pl
from jax.experimental.pallas import tpu as pltpu

F32 = jnp.float32
BF16 = jnp.bfloat16

D_MODEL = 2048
H_RET, DK_RET, DV_RET = 4, 128, 256
H_GLA, DK_GLA, DV_GLA = 4, 128, 256
GLA_RANK = 16
GLA_TEMP = 16.0
ROPE_BASE = 10000.0
PAST_LEN = 16384
PEER_HEADS, PEER_NKEYS, PEER_DQ, PEER_TOPK = 8, 128, 256, 16
N_EXPERTS = PEER_NKEYS * PEER_NKEYS
EPS = 1e-6

LANES = 128
D_MIX = H_RET * DV_RET + H_GLA * DV_GLA
D_IN_MAIN = 2 * H_RET * DK_RET + 2 * H_RET * DV_RET + 2 * H_GLA * DK_GLA + 2 * H_GLA * DV_GLA
OFF_QR = 0
OFF_KR = OFF_QR + H_RET * DK_RET
OFF_VR = OFF_KR + H_RET * DK_RET
OFF_GR = OFF_VR + H_RET * DV_RET
OFF_QG = OFF_GR + H_RET * DV_RET
OFF_KG = OFF_QG + H_GLA * DK_GLA
OFF_VG = OFF_KG + H_GLA * DK_GLA
OFF_GG = OFF_VG + H_GLA * DV_GLA

IN_PROJ_TOKEN_TILE = 1024
IN_PROJ_COL_TILE = 1536
QUERY_TOKEN_TILE = 512
ROUTE_TOKEN_TILE = 256
PEER_TOKEN_TILE = 512
PEER_EXPERT_BLOCK = 1024
PROMPT_CHUNK = 128
SAMPLE_GROUP = 8
VMEM_LIMIT = 56 * 1024 * 1024


def _cparams(sem):
    return pltpu.CompilerParams(dimension_semantics=sem, vmem_limit_bytes=VMEM_LIMIT)


def _norm_matmul_kernel(x_ref, wn_ref, w_ref, wa_ref, o_ref, oa_ref, h_ref):
    @pl.when(pl.program_id(1) == 0)
    def _():
        x = x_ref[...]
        ms = jnp.mean(x * x, axis=-1, keepdims=True)
        h = (x * lax.rsqrt(ms + EPS) * wn_ref[...]).astype(BF16)
        h_ref[...] = h
        oa_ref[...] = jnp.dot(h, wa_ref[...], preferred_element_type=F32)

    o_ref[...] = jnp.dot(h_ref[...], w_ref[...], preferred_element_type=F32)


def _norm_matmul(x, wn, w, wa, tm, tn):
    T, D = x.shape
    N = D_IN_MAIN
    Na = wa.shape[1]
    return pl.pallas_call(
        _norm_matmul_kernel,
        grid=(T // tm, N // tn),
        in_specs=[
            pl.BlockSpec((tm, D), lambda i, j: (i, 0)),
            pl.BlockSpec((1, D), lambda i, j: (0, 0)),
            pl.BlockSpec((D, tn), lambda i, j: (0, j)),
            pl.BlockSpec((D, Na), lambda i, j: (0, 0)),
        ],
        out_specs=[
            pl.BlockSpec((tm, tn), lambda i, j: (i, j)),
            pl.BlockSpec((tm, Na), lambda i, j: (i, 0)),
        ],
        out_shape=[jax.ShapeDtypeStruct((T, N), F32), jax.ShapeDtypeStruct((T, Na), F32)],
        scratch_shapes=[pltpu.VMEM((tm, D), BF16)],
        compiler_params=_cparams(("parallel", "arbitrary")),
        name="norm_in_proj",
    )(x, wn, w, wa)


def _gla_consts(rows, blk):
    idx = np.arange(rows)
    seq = idx // blk
    t = idx[None, :]
    i = idx[:, None]
    same = seq[:, None] == seq[None, :]
    mats = [same & (t <= i), same & (t > i)]
    pmasks = []
    s = blk
    while s >= 2:
        bstart = (idx // s) * s
        mid = bstart + s // 2
        upper = idx >= mid
        mats.append((upper[:, None] & (t >= mid[:, None]) & (t <= i))
                    | ((~upper)[:, None] & (t > i) & (t <= mid[:, None] - 1)))
        pmasks.append((bstart[:, None] == bstart[None, :]) & upper[:, None] & (~upper)[None, :])
        s //= 2
    pmasks.append(np.eye(rows, dtype=bool))
    return (np.concatenate(mats, 0).astype(np.float32), np.stack(pmasks).astype(np.float32))


def _ret_consts(rows, blk):
    log_g = jnp.log1p(-jnp.exp2(-5.0 - jnp.arange(H_RET, dtype=F32)))
    idx = np.arange(rows)
    p = jnp.asarray(idx % blk, F32)
    same = jnp.asarray((idx[:, None] // blk) == (idx[None, :] // blk))
    dist = p[:, None] - p[None, :]
    causal = same & (dist >= 0)
    dmask = jnp.where(causal[None], jnp.exp(jnp.maximum(dist, 0.0)[None] * log_g[:, None, None]), 0.0)
    qdec = jnp.exp((p + 1.0)[None, :] * log_g[:, None])
    kdec = jnp.exp((blk - 1.0 - p)[None, :] * log_g[:, None])
    cdec = jnp.exp(blk * log_g)
    qdec_full = jnp.broadcast_to(qdec[:, :, None], (H_RET, rows, DV_RET))
    kdec_full = jnp.broadcast_to(kdec[:, :, None], (H_RET, rows, DK_RET))
    cdec_full = jnp.broadcast_to(cdec[:, None, None], (H_RET, DK_RET, DV_RET))
    return dmask.astype(F32), qdec_full.astype(F32), kdec_full.astype(F32), cdec_full.astype(F32)


def _rope_tables(pos):
    d = DK_RET
    freqs = ROPE_BASE ** (-jnp.arange(0, d, 2, dtype=F32) / d)
    ang = pos[:, None] * freqs[None, :]
    cos = jnp.cos(ang)
    sin = jnp.sin(ang)
    return jnp.concatenate([cos, cos], axis=-1), jnp.concatenate([-sin, sin], axis=-1)


def _dot_nt(a, b):
    return lax.dot_general(a, b, (((1,), (1,)), ((), ())), preferred_element_type=F32)


def _rotary(x, cosf, sinf):
    return x * cosf + pltpu.roll(x, DK_RET // 2, axis=1) * sinf


def _sigmoid(x):
    return 1.0 / (1.0 + jnp.exp(-x))


def _ret_heads(p_ref, cosf, sinf, dmask_ref, kdec_ref):
    heads = range(H_RET)
    q = [_rotary(p_ref[:, OFF_QR + h * DK_RET:OFF_QR + (h + 1) * DK_RET], cosf, sinf) for h in heads]
    k = [_rotary(p_ref[:, OFF_KR + h * DK_RET:OFF_KR + (h + 1) * DK_RET], cosf, sinf) * (DK_RET ** -0.5)
         for h in heads]
    v = [p_ref[:, OFF_VR + h * DV_RET:OFF_VR + (h + 1) * DV_RET].astype(BF16) for h in heads]
    qb = [q[h].astype(BF16) for h in heads]
    scores = [_dot_nt(qb[h], k[h].astype(BF16)) * dmask_ref[h] for h in heads]
    inner = [jnp.dot(scores[h].astype(BF16), v[h], preferred_element_type=F32) for h in heads]
    kt = [(k[h] * kdec_ref[h]).T for h in heads]
    return qb, inner, kt, v


def _ret_finish(o, p_ref, h, gnw_ref, gnb_ref):
    mu = jnp.mean(o, axis=-1, keepdims=True)
    d = o - mu
    var = jnp.mean(d * d, axis=-1, keepdims=True)
    sl = slice(h * DV_RET, (h + 1) * DV_RET)
    y = d * lax.rsqrt(var + EPS) * gnw_ref[:, sl] + gnb_ref[:, sl]
    g = p_ref[:, OFF_GR + h * DV_RET:OFF_GR + (h + 1) * DV_RET]
    return y * (g * _sigmoid(g))


def _gla_log_decay(a_ref, w2_ref, b2_ref):
    a = a_ref[...].astype(BF16)
    z = jnp.dot(a, w2_ref[...], preferred_element_type=F32) + b2_ref[...]
    return (jnp.minimum(z, 0.0) - jnp.log1p(jnp.exp(-jnp.abs(z)))) * (1.0 / GLA_TEMP)


def _gla_heads(p_ref, la, sums_ref, pm_ref, rows, n_levels):
    heads = range(H_GLA)
    hilo = []
    for h in heads:
        la_h = la[:, h * DK_GLA:(h + 1) * DK_GLA]
        hi = la_h.astype(BF16)
        hilo += [hi, (la_h - hi.astype(F32)).astype(BF16)]
    e2 = jnp.dot(sums_ref[...], jnp.concatenate(hilo, axis=1), preferred_element_type=F32)
    ex = [e2[:, 2 * h * DK_GLA:(2 * h + 1) * DK_GLA] + e2[:, (2 * h + 1) * DK_GLA:(2 * h + 2) * DK_GLA]
          for h in heads]
    q = [p_ref[:, OFF_QG + h * DK_GLA:OFF_QG + (h + 1) * DK_GLA] * (DK_GLA ** -0.5) for h in heads]
    k = [p_ref[:, OFF_KG + h * DK_GLA:OFF_KG + (h + 1) * DK_GLA] for h in heads]
    v = [p_ref[:, OFF_VG + h * DV_GLA:OFF_VG + (h + 1) * DV_GLA].astype(BF16) for h in heads]
    amat = [_dot_nt(q[h].astype(BF16), k[h].astype(BF16)) * pm_ref[n_levels] for h in heads]
    for l in range(n_levels):
        for h in heads:
            dec = jnp.exp(ex[h][(2 + l) * rows:(3 + l) * rows])
            amat[h] = amat[h] + _dot_nt((q[h] * dec).astype(BF16), (k[h] * dec).astype(BF16)) * pm_ref[l]
    inner = [jnp.dot(amat[h].astype(BF16), v[h], preferred_element_type=F32) for h in heads]
    bcum = [ex[h][0:rows] for h in heads]
    qd = [(q[h] * jnp.exp(bcum[h])).astype(BF16) for h in heads]
    kt = [(k[h] * jnp.exp(ex[h][rows:2 * rows])).T for h in heads]
    return qd, inner, kt, v, bcum


def _gla_finish(o, p_ref, h, gw_ref):
    sl = slice(h * DV_GLA, (h + 1) * DV_GLA)
    y = o * lax.rsqrt(jnp.mean(o * o, axis=-1, keepdims=True) + EPS) * gw_ref[:, sl]
    g = p_ref[:, OFF_GG + h * DV_GLA:OFF_GG + (h + 1) * DV_GLA]
    return y * (g * _sigmoid(g))


def _rows_to_cols(row, n):
    return jnp.broadcast_to(row, (n, n)).T


def _mixer_prompt_kernel(p_ref, a_ref, cos_ref, sin_ref, dmask_ref, qdec_ref, kdec_ref, cdec_ref,
                         w2_ref, b2_ref, sums_ref, pm_ref, gnrw_ref, gnrb_ref, gngw_ref,
                         mix_ref, sret_ref, sgla_ref, *, rows, n_levels):
    @pl.when(pl.program_id(1) == 0)
    def _():
        sret_ref[...] = jnp.zeros_like(sret_ref)
        sgla_ref[...] = jnp.zeros_like(sgla_ref)

    cosf = cos_ref[...]
    sinf = sin_ref[...]
    rh = range(H_RET)
    qb, inner, kt, v = _ret_heads(p_ref, cosf, sinf, dmask_ref, kdec_ref)
    s_old = [sret_ref[0, h] for h in rh]
    cross = [jnp.dot(qb[h], s_old[h].astype(BF16), preferred_element_type=F32) * qdec_ref[h] for h in rh]
    upd = [jnp.dot(kt[h].astype(BF16), v[h], preferred_element_type=F32) for h in rh]
    for h in rh:
        sret_ref[0, h] = s_old[h] * cdec_ref[h] + upd[h]
    for h in rh:
        y = _ret_finish(inner[h] + cross[h], p_ref, h, gnrw_ref, gnrb_ref)
        mix_ref[:, h * DV_RET:(h + 1) * DV_RET] = y.astype(mix_ref.dtype)

    la = _gla_log_decay(a_ref, w2_ref, b2_ref)
    gh = range(H_GLA)
    qd, inner, kt, v, bcum = _gla_heads(p_ref, la, sums_ref, pm_ref, rows, n_levels)
    g_old = [sgla_ref[0, h] for h in gh]
    cross = [jnp.dot(qd[h], g_old[h].astype(BF16), preferred_element_type=F32) for h in gh]
    upd = [jnp.dot(kt[h].astype(BF16), v[h], preferred_element_type=F32) for h in gh]
    for h in gh:
        dec = jnp.exp(_rows_to_cols(bcum[h][rows - 1:rows, :], DK_GLA))
        sgla_ref[0, h] = g_old[h] * jnp.concatenate([dec, dec], axis=1) + upd[h]
    for h in gh:
        y = _gla_finish(inner[h] + cross[h], p_ref, h, gngw_ref)
        off = H_RET * DV_RET + h * DV_GLA
        mix_ref[:, off:off + DV_GLA] = y.astype(mix_ref.dtype)


def _mixer_prompt(proj, a_lr, B, L, consts):
    rows = PROMPT_CHUNK
    nc = L // rows
    cosf, sinf, dmask, qdec, kdec, cdec, w2, b2, sums, pm, gnrw, gnrb, gngw = consts
    n_levels = pm.shape[0] - 1
    const2 = lambda b, c: (0, 0)
    const3 = lambda b, c: (0, 0, 0)
    kern = functools.partial(_mixer_prompt_kernel, rows=rows, n_levels=n_levels)
    return pl.pallas_call(
        kern,
        grid=(B, nc),
        in_specs=[
            pl.BlockSpec((rows, D_IN_MAIN), lambda b, c: (b * nc + c, 0)),
            pl.BlockSpec((rows, LANES), lambda b, c: (b * nc + c, 0)),
            pl.BlockSpec((rows, DK_RET), lambda b, c: (c, 0)),
            pl.BlockSpec((rows, DK_RET), lambda b, c: (c, 0)),
            pl.BlockSpec(dmask.shape, const3),
            pl.BlockSpec(qdec.shape, const3),
            pl.BlockSpec(kdec.shape, const3),
            pl.BlockSpec(cdec.shape, const3),
            pl.BlockSpec(w2.shape, const2),
            pl.BlockSpec(b2.shape, const2),
            pl.BlockSpec(sums.shape, const2),
            pl.BlockSpec(pm.shape, const3),
            pl.BlockSpec(gnrw.shape, const2),
            pl.BlockSpec(gnrb.shape, const2),
            pl.BlockSpec(gngw.shape, const2),
        ],
        out_specs=[
            pl.BlockSpec((rows, D_MIX), lambda b, c: (b * nc + c, 0)),
            pl.BlockSpec((1, H_RET, DK_RET, DV_RET), lambda b, c: (b, 0, 0, 0)),
            pl.BlockSpec((1, H_GLA, DK_GLA, DV_GLA), lambda b, c: (b, 0, 0, 0)),
        ],
        out_shape=[
            jax.ShapeDtypeStruct((B * L, D_MIX), BF16),
            jax.ShapeDtypeStruct((B, H_RET, DK_RET, DV_RET), F32),
            jax.ShapeDtypeStruct((B, H_GLA, DK_GLA, DV_GLA), F32),
        ],
        compiler_params=_cparams(("parallel", "arbitrary")),
        name="mixer_prompt",
    )(proj, a_lr, cosf, sinf, dmask, qdec, kdec, cdec, w2, b2, sums, pm, gnrw, gnrb, gngw)


def _mixer_sample_kernel(p_ref, a_ref, cos_ref, sin_ref, dmask_ref, qdec_ref, kdec_ref, cdec_ref,
                         w2_ref, b2_ref, sums_ref, pm_ref, gnrw_ref, gnrb_ref, gngw_ref,
                         sret0_ref, sgla0_ref,
                         mix_ref, sret_ref, sgla_ref, *, rows, seq_len, n_levels):
    n_seq = rows // seq_len
    row_seq = lax.broadcasted_iota(jnp.int32, (rows, 1), 0) // seq_len
    col_seq = lax.broadcasted_iota(jnp.int32, (1, rows), 1) // seq_len
    cosf = cos_ref[...]
    sinf = sin_ref[...]
    qb, inner, kt, v = _ret_heads(p_ref, cosf, sinf, dmask_ref, kdec_ref)
    for h in range(H_RET):
        cross = jnp.zeros((rows, DV_RET), F32)
        for s in range(n_seq):
            s_old = sret0_ref[s, h]
            c_s = jnp.dot(qb[h], s_old.astype(BF16), preferred_element_type=F32)
            cross = cross + jnp.where(row_seq == s, c_s, 0.0)
            kt_s = jnp.where(col_seq == s, kt[h], 0.0).astype(BF16)
            sret_ref[s, h] = s_old * cdec_ref[h] + jnp.dot(kt_s, v[h], preferred_element_type=F32)
        y = _ret_finish(inner[h] + cross * qdec_ref[h], p_ref, h, gnrw_ref, gnrb_ref)
        mix_ref[:, h * DV_RET:(h + 1) * DV_RET] = y.astype(mix_ref.dtype)

    la = _gla_log_decay(a_ref, w2_ref, b2_ref)
    qd, inner, kt, v, bcum = _gla_heads(p_ref, la, sums_ref, pm_ref, rows, n_levels)
    for h in range(H_GLA):
        cross = jnp.zeros((rows, DV_GLA), F32)
        for s in range(n_seq):
            s_old = sgla0_ref[s, h]
            c_s = jnp.dot(qd[h], s_old.astype(BF16), preferred_element_type=F32)
            cross = cross + jnp.where(row_seq == s, c_s, 0.0)
            kt_s = jnp.where(col_seq == s, kt[h], 0.0).astype(BF16)
            last = (s + 1) * seq_len - 1
            dec = jnp.exp(_rows_to_cols(bcum[h][last:last + 1, :], DK_GLA))
            dec = jnp.concatenate([dec, dec], axis=1)
            sgla_ref[s, h] = s_old * dec + jnp.dot(kt_s, v[h], preferred_element_type=F32)
        y = _gla_finish(inner[h] + cross, p_ref, h, gngw_ref)
        off = H_RET * DV_RET + h * DV_GLA
        mix_ref[:, off:off + DV_GLA] = y.astype(mix_ref.dtype)


def _mixer_sample(proj, a_lr, sret0, sgla0, Bs, Ls, consts):
    n_seq = SAMPLE_GROUP
    rows = n_seq * Ls
    cosf, sinf, dmask, qdec, kdec, cdec, w2, b2, sums, pm, gnrw, gnrb, gngw = consts
    n_levels = pm.shape[0] - 1
    const2 = lambda g: (0, 0)
    const3 = lambda g: (0, 0, 0)
    kern = functools.partial(_mixer_sample_kernel, rows=rows, seq_len=Ls, n_levels=n_levels)
    st_spec = pl.BlockSpec((None, n_seq, H_RET, DK_RET, DV_RET), lambda g: (0, g, 0, 0, 0))
    return pl.pallas_call(
        kern,
        grid=(Bs // n_seq,),
        in_specs=[
            pl.BlockSpec((rows, D_IN_MAIN), lambda g: (g, 0)),
            pl.BlockSpec((rows, LANES), lambda g: (g, 0)),
            pl.BlockSpec(cosf.shape, const2),
            pl.BlockSpec(sinf.shape, const2),
            pl.BlockSpec(dmask.shape, const3),
            pl.BlockSpec(qdec.shape, const3),
            pl.BlockSpec(kdec.shape, const3),
            pl.BlockSpec(cdec.shape, const3),
            pl.BlockSpec(w2.shape, const2),
            pl.BlockSpec(b2.shape, const2),
            pl.BlockSpec(sums.shape, const2),
            pl.BlockSpec(pm.shape, const3),
            pl.BlockSpec(gnrw.shape, const2),
            pl.BlockSpec(gnrb.shape, const2),
            pl.BlockSpec(gngw.shape, const2),
            st_spec,
            st_spec,
        ],
        out_specs=[
            pl.BlockSpec((rows, D_MIX), lambda g: (g, 0)),
            st_spec,
            st_spec,
        ],
        out_shape=[
            jax.ShapeDtypeStruct((Bs * Ls, D_MIX), BF16),
            jax.ShapeDtypeStruct((1, Bs, H_RET, DK_RET, DV_RET), F32),
            jax.ShapeDtypeStruct((1, Bs, H_GLA, DK_GLA, DV_GLA), F32),
        ],
        compiler_params=_cparams(("parallel",)),
        name="mixer_sample",
    )(proj, a_lr, cosf, sinf, dmask, qdec, kdec, cdec, w2, b2, sums, pm, gnrw, gnrb, gngw, sret0, sgla0)


def _out_query_kernel(mix_ref, x_ref, wo_ref, wn_ref, wq_ref, k1_ref, k2_ref,
                      x1_ref, h2_ref, s1_ref, s2_ref):
    x1 = x_ref[...] + jnp.dot(mix_ref[...], wo_ref[...], preferred_element_type=F32)
    x1_ref[...] = x1
    ms = jnp.mean(x1 * x1, axis=-1, keepdims=True)
    h2 = (x1 * lax.rsqrt(ms + EPS) * wn_ref[...]).astype(BF16)
    h2_ref[...] = h2
    q = jnp.dot(h2, wq_ref[...], preferred_element_type=F32).astype(BF16)
    half = PEER_DQ // 2
    for h in range(PEER_HEADS):
        q1 = q[:, h * PEER_DQ:h * PEER_DQ + half]
        q2 = q[:, h * PEER_DQ + half:(h + 1) * PEER_DQ]
        s1_ref[h] = _dot_nt(k1_ref[h], q1)
        s2_ref[h] = _dot_nt(k2_ref[h], q2)


def _out_query(mix, x, wo, wn, wq, k1, k2, tm):
    T, D = x.shape
    resident = dict(pipeline_mode=pl.Buffered(1))
    return pl.pallas_call(
        _out_query_kernel,
        grid=(T // tm,),
        in_specs=[
            pl.BlockSpec((tm, D_MIX), lambda i: (i, 0)),
            pl.BlockSpec((tm, D), lambda i: (i, 0)),
            pl.BlockSpec(wo.shape, lambda i: (0, 0), **resident),
            pl.BlockSpec((1, D), lambda i: (0, 0)),
            pl.BlockSpec(wq.shape, lambda i: (0, 0), **resident),
            pl.BlockSpec(k1.shape, lambda i: (0, 0, 0)),
            pl.BlockSpec(k2.shape, lambda i: (0, 0, 0)),
        ],
        out_specs=[
            pl.BlockSpec((tm, D), lambda i: (i, 0)),
            pl.BlockSpec((tm, D), lambda i: (i, 0)),
            pl.BlockSpec((PEER_HEADS, PEER_NKEYS, tm), lambda i: (0, 0, i)),
            pl.BlockSpec((PEER_HEADS, PEER_NKEYS, tm), lambda i: (0, 0, i)),
        ],
        out_shape=[
            jax.ShapeDtypeStruct((T, D), F32),
            jax.ShapeDtypeStruct((T, D), BF16),
            jax.ShapeDtypeStruct((PEER_HEADS, PEER_NKEYS, T), F32),
            jax.ShapeDtypeStruct((PEER_HEADS, PEER_NKEYS, T), F32),
        ],
        compiler_params=_cparams(("parallel",)),
        name="out_proj_peer_query",
    )(mix, x, wo, wn, wq, k1, k2)


SUBLANES = 8


def _compare_exchange(a, i, l, descending=True):
    hi, lo = jnp.maximum(a[i], a[l]), jnp.minimum(a[i], a[l])
    a[i], a[l] = (hi, lo) if descending else (lo, hi)


def _bitonic_merge(a):
    n = len(a)
    j = n // 2
    while j >= 1:
        for i in range(n):
            if i ^ j > i:
                _compare_exchange(a, i, i ^ j)
        j //= 2


def _sorted_top(s):
    n = s.shape[0] // SUBLANES
    a = [s[SUBLANES * j:SUBLANES * (j + 1), :] for j in range(n)]
    k = 2
    while k <= n:
        j = k // 2
        while j >= 1:
            for i in range(n):
                if i ^ j > i:
                    _compare_exchange(a, i, i ^ j, descending=(i & k) == 0)
            j //= 2
        k *= 2
    shift = SUBLANES // 2
    while shift >= 1:
        other = [pltpu.roll(x, shift, axis=0) for x in a]
        a = [jnp.maximum(a[i], other[n - 1 - i]) for i in range(n)]
        _bitonic_merge(a)
        shift //= 2
    return [x[0:1, :] for x in a]


def _kth_largest(x, k):
    for _ in range(k - 1):
        x = jnp.where(x == jnp.max(x, axis=0, keepdims=True), -jnp.inf, x)
    return jnp.max(x, axis=0, keepdims=True)


def _candidate_sums(v1, v2, stack_ref):
    k, half = PEER_TOPK, PEER_TOPK // 2
    for i in range(k):
        stack_ref[0, i:i + 1, :] = v1[i]
        stack_ref[1, i:i + 1, :] = v2[i]
    v1s = stack_ref[0]
    v2s = stack_ref[1]
    pieces = [v1[0] + v2s] + [v1[a] + v2s[:half] for a in range(1, half)] + [v1s[half:] + v2[0]]
    return jnp.concatenate(pieces, axis=0)


def _selection_threshold(s1, v2, tau):
    k = len(v2)
    steps = [k >> (i + 1) for i in range(k.bit_length() - 1)]
    preds = []

    def candidate(step, i=0, base=0):
        if i == len(preds):
            return v2[base + step - 1]
        return jnp.where(preds[i], candidate(step, i + 1, base + steps[i]), candidate(step, i + 1, base))

    last = jnp.full(s1.shape, jnp.inf, F32)
    for step in steps + [None]:
        thr = v2[k - 1] if step is None else candidate(step)
        passed = s1 + thr >= tau
        last = jnp.where(passed, thr, last)
        preds.append(passed)
    return last


def _route_weights_kernel(s1_ref, s2_ref, g_ref, th_ref, e1_ref, e2_ref, stack_ref, *, lane_chunk):
    tt = s1_ref.shape[-1]
    for h in range(PEER_HEADS):
        for c0 in range(0, tt, lane_chunk):
            cs = slice(c0, c0 + lane_chunk)
            s1 = s1_ref[h, :, cs]
            s2 = s2_ref[h, :, cs]
            v1 = _sorted_top(s1)
            v2 = _sorted_top(s2)
            cand = _candidate_sums(v1, v2, stack_ref.at[:, :, cs])
            tau = _kth_largest(cand, PEER_TOPK)
            z = jnp.sum(jnp.where(cand >= tau, jnp.exp(cand - (v1[0] + v2[0])), 0.0), axis=0, keepdims=True)
            th_ref[h, :, cs] = _selection_threshold(s1, v2, tau)
            e1_ref[h, :, cs] = jnp.exp(s1 - v1[0]) * (0.5 / z)
            e2_ref[h, :, cs] = jnp.exp(s2 - v2[0])

    def row_block(i1, carry):
        th_rows = [th_ref[h, pl.ds(i1, 1), :] for h in range(PEER_HEADS)]
        e1_rows = [e1_ref[h, pl.ds(i1, 1), :] for h in range(PEER_HEADS)]
        row0 = pl.multiple_of(i1 * PEER_NKEYS, PEER_NKEYS)
        for c0 in range(0, tt, lane_chunk):
            cs = slice(c0, c0 + lane_chunk)
            g = None
            for h in range(PEER_HEADS):
                e2 = e2_ref[h, :, cs]
                term = jnp.where(s2_ref[h, :, cs] >= th_rows[h][:, cs], e2, jnp.zeros_like(e2)) * e1_rows[h][:, cs]
                g = term if g is None else g + term
            g_ref[pl.ds(row0, PEER_NKEYS), cs] = g.astype(g_ref.dtype)
        return carry

    lax.fori_loop(0, PEER_NKEYS, row_block, 0)


def _route_weights(s1, s2, tt):
    H, K, T = s1.shape
    blk = pl.BlockSpec((H, K, tt), lambda i: (0, 0, i))
    kern = functools.partial(_route_weights_kernel, lane_chunk=min(tt, LANES))
    return pl.pallas_call(
        kern,
        grid=(T // tt,),
        in_specs=[blk, blk],
        out_specs=pl.BlockSpec((None, N_EXPERTS, tt), lambda i: (i, 0, 0)),
        out_shape=jax.ShapeDtypeStruct((T // tt, N_EXPERTS, tt), BF16),
        scratch_shapes=[
            pltpu.VMEM((H, K, tt), F32),
            pltpu.VMEM((H, K, tt), F32),
            pltpu.VMEM((H, K, tt), F32),
            pltpu.VMEM((2, PEER_TOPK, tt), F32),
        ],
        compiler_params=_cparams(("parallel",)),
        name="peer_route_weights",
    )(s1, s2)


def _peer_kernel(h2_ref, u_ref, vt_ref, g_ref, x1_ref, wn_ref, y_ref, acc_ref, w_ref, h2s_ref):
    e = pl.program_id(1)

    @pl.when(e == 0)
    def _():
        acc_ref[...] = jnp.zeros_like(acc_ref)
        h2s_ref[...] = h2_ref[...].astype(F32).T.astype(BF16)

    n_sub, _, sub = g_ref.shape
    for t in range(n_sub):
        cols = slice(t * sub, (t + 1) * sub)
        s = jnp.dot(u_ref[...].astype(BF16), h2s_ref[:, cols], preferred_element_type=F32)
        act = s * (1.0 + lax.erf(s * (2.0 ** -0.5)))
        w_ref[:, cols] = g_ref[t] * act.astype(BF16)
    acc_ref[...] += jnp.dot(vt_ref[...], w_ref[...], preferred_element_type=F32)

    @pl.when(e == pl.num_programs(1) - 1)
    def _():
        out = x1_ref[...] + acc_ref[...].T
        ms = jnp.mean(out * out, axis=-1, keepdims=True)
        y_ref[...] = out * lax.rsqrt(ms + EPS) * wn_ref[...]


def _peer(h2, u, vt_bf, g, x1, wn, tt):
    T, D = x1.shape
    nb, _, eb = vt_bf.shape
    sub = g.shape[2]
    return pl.pallas_call(
        _peer_kernel,
        grid=(T // tt, nb),
        in_specs=[
            pl.BlockSpec((tt, D), lambda i, e: (i, 0), pipeline_mode=pl.Buffered(1)),
            pl.BlockSpec((eb, D), lambda i, e: (e, 0)),
            pl.BlockSpec((None, D, eb), lambda i, e: (e, 0, 0)),
            pl.BlockSpec((tt // sub, eb, sub), lambda i, e: (i, e, 0)),
            pl.BlockSpec((tt, D), lambda i, e: (i, 0)),
            pl.BlockSpec((1, D), lambda i, e: (0, 0)),
        ],
        out_specs=pl.BlockSpec((tt, D), lambda i, e: (i, 0), pipeline_mode=pl.Buffered(1)),
        out_shape=jax.ShapeDtypeStruct((T, D), F32),
        scratch_shapes=[
            pltpu.VMEM((D, tt), F32),
            pltpu.VMEM((eb, tt), BF16),
            pltpu.VMEM((D, tt), BF16),
        ],
        compiler_params=_cparams(("parallel", "arbitrary")),
        name="peer_dense",
    )(h2, u, vt_bf, g, x1, wn)


def _block_transpose_kernel(v_ref, o_ref):
    o_ref[...] = v_ref[...].T.astype(o_ref.dtype)


def _value_blocks(v, eb):
    E, D = v.shape
    return pl.pallas_call(
        _block_transpose_kernel,
        grid=(E // eb,),
        in_specs=[pl.BlockSpec((eb, D), lambda i: (i, 0))],
        out_specs=pl.BlockSpec((None, D, eb), lambda i: (i, 0, 0)),
        out_shape=jax.ShapeDtypeStruct((E // eb, D, eb), BF16),
        compiler_params=_cparams(("parallel",)),
        name="peer_value_blocks",
    )(v)


def _token_tile(T, pref):
    t = min(pref, T)
    while T % t:
        t //= 2
    return t


def _ffn(mix, x2d, wo, wnf, wq, k1, k2, u, vt_bf, wn_final):
    T = x2d.shape[0]
    x1, h2, s1, s2 = _out_query(mix, x2d, wo, wnf, wq, k1, k2, _token_tile(T, QUERY_TOKEN_TILE))
    g = _route_weights(s1, s2, _token_tile(T, ROUTE_TOKEN_TILE))
    return _peer(h2, u, vt_bf, g, x1, wn_final, _token_tile(T, PEER_TOKEN_TILE))


def kernel(x_prompt, x_sample, state_ret, state_gla, w_norm_mix, w_in, w_gla_a2, b_gla_a, gn_ret_w, gn_ret_b, gn_gla_w, w_out, w_norm_ffn, w_pq, sub_keys1, sub_keys2, u_tab, v_tab, w_norm_final):
    Bp, Lp, D = x_prompt.shape
    Bs, Ls, _ = x_sample.shape
    depth = w_in.shape[0]
    assert depth == 1 and D == D_MODEL
    l = 0
    xp = x_prompt.reshape(Bp * Lp, D)
    xs = x_sample.reshape(Bs * Ls, D)

    w_main = w_in[l].astype(BF16)
    w_gate = jnp.pad(w_in[l, :, D_IN_MAIN:], ((0, 0), (0, LANES - GLA_RANK))).astype(BF16)
    w2 = jnp.pad(w_gla_a2[l], ((0, LANES - GLA_RANK), (0, 0))).astype(BF16)
    b2 = b_gla_a[l].reshape(1, -1)
    gnrw = gn_ret_w[l].reshape(1, -1)
    gnrb = gn_ret_b[l].reshape(1, -1)
    gngw = gn_gla_w[l].reshape(1, -1)
    wnm = w_norm_mix[l].reshape(1, D)
    wnf = w_norm_ffn[l].reshape(1, D)
    wn_final = w_norm_final.reshape(1, D)
    wo = w_out[l].astype(BF16)
    wq = w_pq[l].astype(BF16)
    k1 = sub_keys1[l].astype(BF16)
    k2 = sub_keys2[l].astype(BF16)
    u = u_tab[l]
    vt_bf = _value_blocks(v_tab[l], PEER_EXPERT_BLOCK)

    cos_p, sin_p = _rope_tables(jnp.arange(Lp, dtype=F32))
    pos_s = PAST_LEN + jnp.arange(Ls, dtype=F32)
    cos_s, sin_s = _rope_tables(jnp.tile(pos_s, SAMPLE_GROUP))
    sums_p, pm_p = _gla_consts(PROMPT_CHUNK, PROMPT_CHUNK)
    sums_s, pm_s = _gla_consts(SAMPLE_GROUP * Ls, Ls)
    consts_p = (cos_p, sin_p) + _ret_consts(PROMPT_CHUNK, PROMPT_CHUNK) + (
        w2, b2, jnp.asarray(sums_p, BF16), jnp.asarray(pm_p), gnrw, gnrb, gngw)
    consts_s = (cos_s, sin_s) + _ret_consts(SAMPLE_GROUP * Ls, Ls) + (
        w2, b2, jnp.asarray(sums_s, BF16), jnp.asarray(pm_s), gnrw, gnrb, gngw)

    proj_p, a_p = _norm_matmul(xp, wnm, w_main, w_gate, _token_tile(Bp * Lp, IN_PROJ_TOKEN_TILE), IN_PROJ_COL_TILE)
    proj_s, a_s = _norm_matmul(xs, wnm, w_main, w_gate, _token_tile(Bs * Ls, IN_PROJ_TOKEN_TILE), IN_PROJ_COL_TILE)
    mix_p, rp, gp = _mixer_prompt(proj_p, a_p, Bp, Lp, consts_p)
    mix_s, rs, gs = _mixer_sample(proj_s, a_s, state_ret, state_gla, Bs, Ls, consts_s)

    yp = _ffn(mix_p, xp, wo, wnf, wq, k1, k2, u, vt_bf, wn_final)
    ys = _ffn(mix_s, xs, wo, wnf, wq, k1, k2, u, vt_bf, wn_final)
    return (yp.reshape(Bp, Lp, D), ys.reshape(Bs, Ls, D), rp[None], gp[None], rs, gs)
```

```python
import functools
import math

import numpy as np
import jax
import jax.numpy as jnp
from jax import lax
from jax.experimental import pallas as pl
from jax.experimental.pallas import tpu as pltpu

F32 = jnp.float32
BF16 = jnp.bfloat16

D_MODEL = 2048
H_RET, DK_RET, DV_RET = 4, 128, 256
H_GLA, DK_GLA, DV_GLA = 4, 128, 256
GLA_RANK = 16
GLA_TEMP = 16.0
ROPE_BASE = 10000.0
PAST_LEN = 16384
PEER_HEADS, PEER_NKEYS, PEER_DQ, PEER_TOPK = 8, 128, 256, 16
N_EXPERTS = PEER_NKEYS * PEER_NKEYS
EPS = 1e-6

LANES = 128
D_MIX = H_RET * DV_RET + H_GLA * DV_GLA
D_IN_MAIN = 2 * H_RET * DK_RET + 2 * H_RET * DV_RET + 2 * H_GLA * DK_GLA + 2 * H_GLA * DV_GLA
OFF_QR = 0
OFF_KR = OFF_QR + H_RET * DK_RET
OFF_VR = OFF_KR + H_RET * DK_RET
OFF_GR = OFF_VR + H_RET * DV_RET
OFF_QG = OFF_GR + H_RET * DV_RET
OFF_KG = OFF_QG + H_GLA * DK_GLA
OFF_VG = OFF_KG + H_GLA * DK_GLA
OFF_GG = OFF_VG + H_GLA * DV_GLA

IN_PROJ_TOKEN_TILE = 1024
IN_PROJ_COL_TILE = 1536
QUERY_TOKEN_TILE = 512
ROUTE_TOKEN_TILE = 256
PEER_TOKEN_TILE = 512
PEER_EXPERT_BLOCK = 1024
PROMPT_CHUNK = 128
SAMPLE_GROUP = 8
VMEM_LIMIT = 56 * 1024 * 1024


def _cparams(sem):
    return pltpu.CompilerParams(dimension_semantics=sem, vmem_limit_bytes=VMEM_LIMIT)


def _norm_matmul_kernel(x_ref, wn_ref, w_ref, wa_ref, o_ref, oa_ref, h_ref):
    @pl.when(pl.program_id(1) == 0)
    def _():
        x = x_ref[...]
        ms = jnp.mean(x * x, axis=-1, keepdims=True)
        h = (x * lax.rsqrt(ms + EPS) * wn_ref[...]).astype(BF16)
        h_ref[...] = h
        oa_ref[...] = jnp.dot(h, wa_ref[...], preferred_element_type=F32)

    o_ref[...] = jnp.dot(h_ref[...], w_ref[...], preferred_element_type=F32)


def _norm_matmul(x, wn, w, wa, tm, tn):
    T, D = x.shape
    N = D_IN_MAIN
    Na = wa.shape[1]
    return pl.pallas_call(
        _norm_matmul_kernel,
        grid=(T // tm, N // tn),
        in_specs=[
            pl.BlockSpec((tm, D), lambda i, j: (i, 0)),
            pl.BlockSpec((1, D), lambda i, j: (0, 0)),
            pl.BlockSpec((D, tn), lambda i, j: (0, j)),
            pl.BlockSpec((D, Na), lambda i, j: (0, 0)),
        ],
        out_specs=[
            pl.BlockSpec((tm, tn), lambda i, j: (i, j)),
            pl.BlockSpec((tm, Na), lambda i, j: (i, 0)),
        ],
        out_shape=[jax.ShapeDtypeStruct((T, N), F32), jax.ShapeDtypeStruct((T, Na), F32)],
        scratch_shapes=[pltpu.VMEM((tm, D), BF16)],
        compiler_params=_cparams(("parallel", "arbitrary")),
        name="norm_in_proj",
    )(x, wn, w, wa)


def _gla_consts(rows, blk):
    idx = np.arange(rows)
    seq = idx // blk
    t = idx[None, :]
    i = idx[:, None]
    same = seq[:, None] == seq[None, :]
    mats = [same & (t <= i), same & (t > i)]
    pmasks = []
    s = blk
    while s >= 2:
        bstart = (idx // s) * s
        mid = bstart + s // 2
        upper = idx >= mid
        mats.append((upper[:, None] & (t >= mid[:, None]) & (t <= i))
                    | ((~upper)[:, None] & (t > i) & (t <= mid[:, None] - 1)))
        pmasks.append((bstart[:, None] == bstart[None, :]) & upper[:, None] & (~upper)[None, :])
        s //= 2
    pmasks.append(np.eye(rows, dtype=bool))
    return (np.concatenate(mats, 0).astype(np.float32), np.stack(pmasks).astype(np.float32))


def _ret_consts(rows, blk):
    log_g = jnp.log1p(-jnp.exp2(-5.0 - jnp.arange(H_RET, dtype=F32)))
    idx = np.arange(rows)
    p = jnp.asarray(idx % blk, F32)
    same = jnp.asarray((idx[:, None] // blk) == (idx[None, :] // blk))
    dist = p[:, None] - p[None, :]
    causal = same & (dist >= 0)
    dmask = jnp.where(causal[None], jnp.exp(jnp.maximum(dist, 0.0)[None] * log_g[:, None, None]), 0.0)
    qdec = jnp.exp((p + 1.0)[None, :] * log_g[:, None])
    kdec = jnp.exp((blk - 1.0 - p)[None, :] * log_g[:, None])
    cdec = jnp.exp(blk * log_g)
    qdec_full = jnp.broadcast_to(qdec[:, :, None], (H_RET, rows, DV_RET))
    kdec_full = jnp.broadcast_to(kdec[:, :, None], (H_RET, rows, DK_RET))
    cdec_full = jnp.broadcast_to(cdec[:, None, None], (H_RET, DK_RET, DV_RET))
    return dmask.astype(F32), qdec_full.astype(F32), kdec_full.astype(F32), cdec_full.astype(F32)


def _rope_tables(pos):
    d = DK_RET
    freqs = ROPE_BASE ** (-jnp.arange(0, d, 2, dtype=F32) / d)
    ang = pos[:, None] * freqs[None, :]
    cos = jnp.cos(ang)
    sin = jnp.sin(ang)
    return jnp.concatenate([cos, cos], axis=-1), jnp.concatenate([-sin, sin], axis=-1)


def _dot_nt(a, b):
    return lax.dot_general(a, b, (((1,), (1,)), ((), ())), preferred_element_type=F32)


def _rotary(x, cosf, sinf):
    return x * cosf + pltpu.roll(x, DK_RET // 2, axis=1) * sinf


def _sigmoid(x):
    return 1.0 / (1.0 + jnp.exp(-x))


def _ret_heads(p_ref, cosf, sinf, dmask_ref, kdec_ref):
    heads = range(H_RET)
    q = [_rotary(p_ref[:, OFF_QR + h * DK_RET:OFF_QR + (h + 1) * DK_RET], cosf, sinf) for h in heads]
    k = [_rotary(p_ref[:, OFF_KR + h * DK_RET:OFF_KR + (h + 1) * DK_RET], cosf, sinf) * (DK_RET ** -0.5)
         for h in heads]
    v = [p_ref[:, OFF_VR + h * DV_RET:OFF_VR + (h + 1) * DV_RET].astype(BF16) for h in heads]
    qb = [q[h].astype(BF16) for h in heads]
    scores = [_dot_nt(qb[h], k[h].astype(BF16)) * dmask_ref[h] for h in heads]
    inner = [jnp.dot(scores[h].astype(BF16), v[h], preferred_element_type=F32) for h in heads]
    kt = [(k[h] * kdec_ref[h]).T for h in heads]
    return qb, inner, kt, v


def _ret_finish(o, p_ref, h, gnw_ref, gnb_ref):
    mu = jnp.mean(o, axis=-1, keepdims=True)
    d = o - mu
    var = jnp.mean(d * d, axis=-1, keepdims=True)
    sl = slice(h * DV_RET, (h + 1) * DV_RET)
    y = d * lax.rsqrt(var + EPS) * gnw_ref[:, sl] + gnb_ref[:, sl]
    g = p_ref[:, OFF_GR + h * DV_RET:OFF_GR + (h + 1) * DV_RET]
    return y * (g * _sigmoid(g))


def _gla_log_decay(a_ref, w2_ref, b2_ref):
    a = a_ref[...].astype(BF16)
    z = jnp.dot(a, w2_ref[...], preferred_element_type=F32) + b2_ref[...]
    return (jnp.minimum(z, 0.0) - jnp.log1p(jnp.exp(-jnp.abs(z)))) * (1.0 / GLA_TEMP)


def _gla_heads(p_ref, la, sums_ref, pm_ref, rows, n_levels):
    heads = range(H_GLA)
    hilo = []
    for h in heads:
        la_h = la[:, h * DK_GLA:(h + 1) * DK_GLA]
        hi = la_h.astype(BF16)
        hilo += [hi, (la_h - hi.astype(F32)).astype(BF16)]
    e2 = jnp.dot(sums_ref[...], jnp.concatenate(hilo, axis=1), preferred_element_type=F32)
    ex = [e2[:, 2 * h * DK_GLA:(2 * h + 1) * DK_GLA] + e2[:, (2 * h + 1) * DK_GLA:(2 * h + 2) * DK_GLA]
          for h in heads]
    q = [p_ref[:, OFF_QG + h * DK_GLA:OFF_QG + (h + 1) * DK_GLA] * (DK_GLA ** -0.5) for h in heads]
    k = [p_ref[:, OFF_KG + h * DK_GLA:OFF_KG + (h + 1) * DK_GLA] for h in heads]
    v = [p_ref[:, OFF_VG + h * DV_GLA:OFF_VG + (h + 1) * DV_GLA].astype(BF16) for h in heads]
    amat = [_dot_nt(q[h].astype(BF16), k[h].astype(BF16)) * pm_ref[n_levels] for h in heads]
    for l in range(n_levels):
        for h in heads:
            dec = jnp.exp(ex[h][(2 + l) * rows:(3 + l) * rows])
            amat[h] = amat[h] + _dot_nt((q[h] * dec).astype(BF16), (k[h] * dec).astype(BF16)) * pm_ref[l]
    inner = [jnp.dot(amat[h].astype(BF16), v[h], preferred_element_type=F32) for h in heads]
    bcum = [ex[h][0:rows] for h in heads]
    qd = [(q[h] * jnp.exp(bcum[h])).astype(BF16) for h in heads]
    kt = [(k[h] * jnp.exp(ex[h][rows:2 * rows])).T for h in heads]
    return qd, inner, kt, v, bcum


def _gla_finish(o, p_ref, h, gw_ref):
    sl = slice(h * DV_GLA, (h + 1) * DV_GLA)
    y = o * lax.rsqrt(jnp.mean(o * o, axis=-1, keepdims=True) + EPS) * gw_ref[:, sl]
    g = p_ref[:, OFF_GG + h * DV_GLA:OFF_GG + (h + 1) * DV_GLA]
    return y * (g * _sigmoid(g))


def _rows_to_cols(row, n):
    return jnp.broadcast_to(row, (n, n)).T


def _mixer_prompt_kernel(p_ref, a_ref, cos_ref, sin_ref, dmask_ref, qdec_ref, kdec_ref, cdec_ref,
                         w2_ref, b2_ref, sums_ref, pm_ref, gnrw_ref, gnrb_ref, gngw_ref,
                         mix_ref, sret_ref, sgla_ref, *, rows, n_levels):
    @pl.when(pl.program_id(1) == 0)
    def _():
        sret_ref[...] = jnp.zeros_like(sret_ref)
        sgla_ref[...] = jnp.zeros_like(sgla_ref)

    cosf = cos_ref[...]
    sinf = sin_ref[...]
    rh = range(H_RET)
    qb, inner, kt, v = _ret_heads(p_ref, cosf, sinf, dmask_ref, kdec_ref)
    s_old = [sret_ref[0, h] for h in rh]
    cross = [jnp.dot(qb[h], s_old[h].astype(BF16), preferred_element_type=F32) * qdec_ref[h] for h in rh]
    upd = [jnp.dot(kt[h].astype(BF16), v[h], preferred_element_type=F32) for h in rh]
    for h in rh:
        sret_ref[0, h] = s_old[h] * cdec_ref[h] + upd[h]
    for h in rh:
        y = _ret_finish(inner[h] + cross[h], p_ref, h, gnrw_ref, gnrb_ref)
        mix_ref[:, h * DV_RET:(h + 1) * DV_RET] = y.astype(mix_ref.dtype)

    la = _gla_log_decay(a_ref, w2_ref, b2_ref)
    gh = range(H_GLA)
    qd, inner, kt, v, bcum = _gla_heads(p_ref, la, sums_ref, pm_ref, rows, n_levels)
    g_old = [sgla_ref[0, h] for h in gh]
    cross = [jnp.dot(qd[h], g_old[h].astype(BF16), preferred_element_type=F32) for h in gh]
    upd = [jnp.dot(kt[h].astype(BF16), v[h], preferred_element_type=F32) for h in gh]
    for h in gh:
        dec = jnp.exp(_rows_to_cols(bcum[h][rows - 1:rows, :], DK_GLA))
        sgla_ref[0, h] = g_old[h] * jnp.concatenate([dec, dec], axis=1) + upd[h]
    for h in gh:
        y = _gla_finish(inner[h] + cross[h], p_ref, h, gngw_ref)
        off = H_RET * DV_RET + h * DV_GLA
        mix_ref[:, off:off + DV_GLA] = y.astype(mix_ref.dtype)


def _mixer_prompt(proj, a_lr, B, L, consts):
    rows = PROMPT_CHUNK
    nc = L // rows
    cosf, sinf, dmask, qdec, kdec, cdec, w2, b2, sums, pm, gnrw, gnrb, gngw = consts
    n_levels = pm.shape[0] - 1
    const2 = lambda b, c: (0, 0)
    const3 = lambda b, c: (0, 0, 0)
    kern = functools.partial(_mixer_prompt_kernel, rows=rows, n_levels=n_levels)
    return pl.pallas_call(
        kern,
        grid=(B, nc),
        in_specs=[
            pl.BlockSpec((rows, D_IN_MAIN), lambda b, c: (b * nc + c, 0)),
            pl.BlockSpec((rows, LANES), lambda b, c: (b * nc + c, 0)),
            pl.BlockSpec((rows, DK_RET), lambda b, c: (c, 0)),
            pl.BlockSpec((rows, DK_RET), lambda b, c: (c, 0)),
            pl.BlockSpec(dmask.shape, const3),
            pl.BlockSpec(qdec.shape, const3),
            pl.BlockSpec(kdec.shape, const3),
            pl.BlockSpec(cdec.shape, const3),
            pl.BlockSpec(w2.shape, const2),
            pl.BlockSpec(b2.shape, const2),
            pl.BlockSpec(sums.shape, const2),
            pl.BlockSpec(pm.shape, const3),
            pl.BlockSpec(gnrw.shape, const2),
            pl.BlockSpec(gnrb.shape, const2),
            pl.BlockSpec(gngw.shape, const2),
        ],
        out_specs=[
            pl.BlockSpec((rows, D_MIX), lambda b, c: (b * nc + c, 0)),
            pl.BlockSpec((1, H_RET, DK_RET, DV_RET), lambda b, c: (b, 0, 0, 0)),
            pl.BlockSpec((1, H_GLA, DK_GLA, DV_GLA), lambda b, c: (b, 0, 0, 0)),
        ],
        out_shape=[
            jax.ShapeDtypeStruct((B * L, D_MIX), BF16),
            jax.ShapeDtypeStruct((B, H_RET, DK_RET, DV_RET), F32),
            jax.ShapeDtypeStruct((B, H_GLA, DK_GLA, DV_GLA), F32),
        ],
        compiler_params=_cparams(("parallel", "arbitrary")),
        name="mixer_prompt",
    )(proj, a_lr, cosf, sinf, dmask, qdec, kdec, cdec, w2, b2, sums, pm, gnrw, gnrb, gngw)


def _mixer_sample_kernel(p_ref, a_ref, cos_ref, sin_ref, dmask_ref, qdec_ref, kdec_ref, cdec_ref,
                         w2_ref, b2_ref, sums_ref, pm_ref, gnrw_ref, gnrb_ref, gngw_ref,
                         sret0_ref, sgla0_ref,
                         mix_ref, sret_ref, sgla_ref, *, rows, seq_len, n_levels):
    n_seq = rows // seq_len
    row_seq = lax.broadcasted_iota(jnp.int32, (rows, 1), 0) // seq_len
    col_seq = lax.broadcasted_iota(jnp.int32, (1, rows), 1) // seq_len
    cosf = cos_ref[...]
    sinf = sin_ref[...]
    qb, inner, kt, v = _ret_heads(p_ref, cosf, sinf, dmask_ref, kdec_ref)
    for h in range(H_RET):
        cross = jnp.zeros((rows, DV_RET), F32)
        for s in range(n_seq):
            s_old = sret0_ref[s, h]
            c_s = jnp.dot(qb[h], s_old.astype(BF16), preferred_element_type=F32)
            cross = cross + jnp.where(row_seq == s, c_s, 0.0)
            kt_s = jnp.where(col_seq == s, kt[h], 0.0).astype(BF16)
            sret_ref[s, h] = s_old * cdec_ref[h] + jnp.dot(kt_s, v[h], preferred_element_type=F32)
        y = _ret_finish(inner[h] + cross * qdec_ref[h], p_ref, h, gnrw_ref, gnrb_ref)
        mix_ref[:, h * DV_RET:(h + 1) * DV_RET] = y.astype(mix_ref.dtype)

    la = _gla_log_decay(a_ref, w2_ref, b2_ref)
    qd, inner, kt, v, bcum = _gla_heads(p_ref, la, sums_ref, pm_ref, rows, n_levels)
    for h in range(H_GLA):
        cross = jnp.zeros((rows, DV_GLA), F32)
        for s in range(n_seq):
            s_old = sgla0_ref[s, h]
            c_s = jnp.dot(qd[h], s_old.astype(BF16), preferred_element_type=F32)
            cross = cross + jnp.where(row_seq == s, c_s, 0.0)
            kt_s = jnp.where(col_seq == s, kt[h], 0.0).astype(BF16)
            last = (s + 1) * seq_len - 1
            dec = jnp.exp(_rows_to_cols(bcum[h][last:last + 1, :], DK_GLA))
            dec = jnp.concatenate([dec, dec], axis=1)
            sgla_ref[s, h] = s_old * dec + jnp.dot(kt_s, v[h], preferred_element_type=F32)
        y = _gla_finish(inner[h] + cross, p_ref, h, gngw_ref)
        off = H_RET * DV_RET + h * DV_GLA
        mix_ref[:, off:off + DV_GLA] = y.astype(mix_ref.dtype)


def _mixer_sample(proj, a_lr, sret0, sgla0, Bs, Ls, consts):
    n_seq = SAMPLE_GROUP
    rows = n_seq * Ls
    cosf, sinf, dmask, qdec, kdec, cdec, w2, b2, sums, pm, gnrw, gnrb, gngw = consts
    n_levels = pm.shape[0] - 1
    const2 = lambda g: (0, 0)
    const3 = lambda g: (0, 0, 0)
    kern = functools.partial(_mixer_sample_kernel, rows=rows, seq_len=Ls, n_levels=n_levels)
    st_spec = pl.BlockSpec((None, n_seq, H_RET, DK_RET, DV_RET), lambda g: (0, g, 0, 0, 0))
    return pl.pallas_call(
        kern,
        grid=(Bs // n_seq,),
        in_specs=[
            pl.BlockSpec((rows, D_IN_MAIN), lambda g: (g, 0)),
            pl.BlockSpec((rows, LANES), lambda g: (g, 0)),
            pl.BlockSpec(cosf.shape, const2),
            pl.BlockSpec(sinf.shape, const2),
            pl.BlockSpec(dmask.shape, const3),
            pl.BlockSpec(qdec.shape, const3),
            pl.BlockSpec(kdec.shape, const3),
            pl.BlockSpec(cdec.shape, const3),
            pl.BlockSpec(w2.shape, const2),
            pl.BlockSpec(b2.shape, const2),
            pl.BlockSpec(sums.shape, const2),
            pl.BlockSpec(pm.shape, const3),
            pl.BlockSpec(gnrw.shape, const2),
            pl.BlockSpec(gnrb.shape, const2),
            pl.BlockSpec(gngw.shape, const2),
            st_spec,
            st_spec,
        ],
        out_specs=[
            pl.BlockSpec((rows, D_MIX), lambda g: (g, 0)),
            st_spec,
            st_spec,
        ],
        out_shape=[
            jax.ShapeDtypeStruct((Bs * Ls, D_MIX), BF16),
            jax.ShapeDtypeStruct((1, Bs, H_RET, DK_RET, DV_RET), F32),
            jax.ShapeDtypeStruct((1, Bs, H_GLA, DK_GLA, DV_GLA), F32),
        ],
        compiler_params=_cparams(("parallel",)),
        name="mixer_sample",
    )(proj, a_lr, cosf, sinf, dmask, qdec, kdec, cdec, w2, b2, sums, pm, gnrw, gnrb, gngw, sret0, sgla0)


def _out_query_kernel(mix_ref, x_ref, wo_ref, wn_ref, wq_ref, k1_ref, k2_ref,
                      x1_ref, h2_ref, s1_ref, s2_ref):
    x1 = x_ref[...] + jnp.dot(mix_ref[...], wo_ref[...], preferred_element_type=F32)
    x1_ref[...] = x1
    ms = jnp.mean(x1 * x1, axis=-1, keepdims=True)
    h2 = (x1 * lax.rsqrt(ms + EPS) * wn_ref[...]).astype(BF16)
    h2_ref[...] = h2
    q = jnp.dot(h2, wq_ref[...], preferred_element_type=F32).astype(BF16)
    half = PEER_DQ // 2
    for h in range(PEER_HEADS):
        q1 = q[:, h * PEER_DQ:h * PEER_DQ + half]
        q2 = q[:, h * PEER_DQ + half:(h + 1) * PEER_DQ]
        s1_ref[h] = _dot_nt(k1_ref[h], q1)
        s2_ref[h] = _dot_nt(k2_ref[h], q2)


def _out_query(mix, x, wo, wn, wq, k1, k2, tm):
    T, D = x.shape
    resident = dict(pipeline_mode=pl.Buffered(1))
    return pl.pallas_call(
        _out_query_kernel,
        grid=(T // tm,),
        in_specs=[
            pl.BlockSpec((tm, D_MIX), lambda i: (i, 0)),
            pl.BlockSpec((tm, D), lambda i: (i, 0)),
            pl.BlockSpec(wo.shape, lambda i: (0, 0), **resident),
            pl.BlockSpec((1, D), lambda i: (0, 0)),
            pl.BlockSpec(wq.shape, lambda i: (0, 0), **resident),
            pl.BlockSpec(k1.shape, lambda i: (0, 0, 0)),
            pl.BlockSpec(k2.shape, lambda i: (0, 0, 0)),
        ],
        out_specs=[
            pl.BlockSpec((tm, D), lambda i: (i, 0)),
            pl.BlockSpec((tm, D), lambda i: (i, 0)),
            pl.BlockSpec((PEER_HEADS, PEER_NKEYS, tm), lambda i: (0, 0, i)),
            pl.BlockSpec((PEER_HEADS, PEER_NKEYS, tm), lambda i: (0, 0, i)),
        ],
        out_shape=[
            jax.ShapeDtypeStruct((T, D), F32),
            jax.ShapeDtypeStruct((T, D), BF16),
            jax.ShapeDtypeStruct((PEER_HEADS, PEER_NKEYS, T), F32),
            jax.ShapeDtypeStruct((PEER_HEADS, PEER_NKEYS, T), F32),
        ],
        compiler_params=_cparams(("parallel",)),
        name="out_proj_peer_query",
    )(mix, x, wo, wn, wq, k1, k2)


SUBLANES = 8


def _compare_exchange(a, i, l, descending=True):
    hi, lo = jnp.maximum(a[i], a[l]), jnp.minimum(a[i], a[l])
    a[i], a[l] = (hi, lo) if descending else (lo, hi)


def _bitonic_merge(a):
    n = len(a)
    j = n // 2
    while j >= 1:
        for i in range(n):
            if i ^ j > i:
                _compare_exchange(a, i, i ^ j)
        j //= 2


def _merge_sublane_lists(a):
    n = len(a)
    shift = SUBLANES // 2
    while shift >= 1:
        other = [pltpu.roll(x, shift, axis=0) for x in a]
        a = [jnp.maximum(a[i], other[n - 1 - i]) for i in range(n)]
        _bitonic_merge(a)
        shift //= 2
    return a


def _sorted_top(s):
    n = s.shape[0] // SUBLANES
    a = [s[SUBLANES * j:SUBLANES * (j + 1), :] for j in range(n)]
    k = 2
    while k <= n:
        j = k // 2
        while j >= 1:
            for i in range(n):
                if i ^ j > i:
                    _compare_exchange(a, i, i ^ j, descending=(i & k) == 0)
            j //= 2
        k *= 2
    return [x[0:1, :] for x in _merge_sublane_lists(a)]


def _top_pair_sums(v1, v2, stack_ref):
    k, half = PEER_TOPK, PEER_TOPK // 2
    for i in range(k):
        stack_ref[0, i:i + 1, :] = v1[i]
        stack_ref[1, i:i + 1, :] = v2[i]
    v1_low = stack_ref[0, 0:half, :]
    v2_low = stack_ref[1, 0:half, :]
    first = [v1_low + v2[b] for b in range(k)]
    second = [v1[half + j] + v2_low for j in range(half)]
    t = [first[i] if i < k - half else jnp.maximum(first[i], second[k - 1 - i]) for i in range(k)]
    _bitonic_merge(t)
    return [x[0:1, :] for x in _merge_sublane_lists(t)]


def _selection_threshold(s1, v2, tau):
    k = len(v2)
    steps = [k >> (i + 1) for i in range(k.bit_length() - 1)]
    preds = []

    def candidate(step, i=0, base=0):
        if i == len(preds):
            return v2[base + step - 1]
        return jnp.where(preds[i], candidate(step, i + 1, base + steps[i]), candidate(step, i + 1, base))

    last = jnp.full(s1.shape, jnp.inf, F32)
    for step in steps + [None]:
        thr = v2[k - 1] if step is None else candidate(step)
        passed = s1 + thr >= tau
        last = jnp.where(passed, thr, last)
        preds.append(passed)
    return last


def _route_weights_kernel(s1_ref, s2_ref, g_ref, th_ref, e1_ref, e2_ref, stack_ref, *, lane_chunk):
    tt = s1_ref.shape[-1]
    for h in range(PEER_HEADS):
        for c0 in range(0, tt, lane_chunk):
            cs = slice(c0, c0 + lane_chunk)
            s1 = s1_ref[h, :, cs]
            s2 = s2_ref[h, :, cs]
            v1 = _sorted_top(s1)
            v2 = _sorted_top(s2)
            top = _top_pair_sums(v1, v2, stack_ref.at[:, :, cs])
            tau = top[-1]
            z = sum(jnp.exp(c - top[0]) for c in top)
            th_ref[h, :, cs] = _selection_threshold(s1, v2, tau)
            e1_ref[h, :, cs] = jnp.exp(s1 - v1[0]) * (0.5 / z)
            e2_ref[h, :, cs] = jnp.exp(s2 - v2[0])

    def row_block(i1, carry):
        th_rows = [th_ref[h, pl.ds(i1, 1), :] for h in range(PEER_HEADS)]
        e1_rows = [e1_ref[h, pl.ds(i1, 1), :] for h in range(PEER_HEADS)]
        row0 = pl.multiple_of(i1 * PEER_NKEYS, PEER_NKEYS)
        for c0 in range(0, tt, lane_chunk):
            cs = slice(c0, c0 + lane_chunk)
            g = None
            for h in range(PEER_HEADS):
                e2 = e2_ref[h, :, cs]
                term = jnp.where(s2_ref[h, :, cs] >= th_rows[h][:, cs], e2, jnp.zeros_like(e2)) * e1_rows[h][:, cs]
                g = term if g is None else g + term
            g_ref[pl.ds(row0, PEER_NKEYS), cs] = g.astype(g_ref.dtype)
        return carry

    lax.fori_loop(0, PEER_NKEYS, row_block, 0)


def _route_weights(s1, s2, tt):
    H, K, T = s1.shape
    blk = pl.BlockSpec((H, K, tt), lambda i: (0, 0, i))
    kern = functools.partial(_route_weights_kernel, lane_chunk=min(tt, LANES))
    return pl.pallas_call(
        kern,
        grid=(T // tt,),
        in_specs=[blk, blk],
        out_specs=pl.BlockSpec((None, N_EXPERTS, tt), lambda i: (i, 0, 0)),
        out_shape=jax.ShapeDtypeStruct((T // tt, N_EXPERTS, tt), BF16),
        scratch_shapes=[
            pltpu.VMEM((H, K, tt), F32),
            pltpu.VMEM((H, K, tt), F32),
            pltpu.VMEM((H, K, tt), F32),
            pltpu.VMEM((2, PEER_TOPK, tt), F32),
        ],
        compiler_params=_cparams(("parallel",)),
        name="peer_route_weights",
    )(s1, s2)


def _peer_kernel(h2_ref, u_ref, vt_ref, g_ref, x1_ref, wn_ref, y_ref, acc_ref, w_ref, h2s_ref):
    e = pl.program_id(1)

    @pl.when(e == 0)
    def _():
        acc_ref[...] = jnp.zeros_like(acc_ref)
        h2s_ref[...] = h2_ref[...].astype(F32).T.astype(BF16)

    n_sub, _, sub = g_ref.shape
    for t in range(n_sub):
        cols = slice(t * sub, (t + 1) * sub)
        s = jnp.dot(u_ref[...].astype(BF16), h2s_ref[:, cols], preferred_element_type=F32)
        act = s * (1.0 + lax.erf(s * (2.0 ** -0.5)))
        w_ref[:, cols] = g_ref[t] * act.astype(BF16)
    acc_ref[...] += jnp.dot(vt_ref[...], w_ref[...], preferred_element_type=F32)

    @pl.when(e == pl.num_programs(1) - 1)
    def _():
        out = x1_ref[...] + acc_ref[...].T
        ms = jnp.mean(out * out, axis=-1, keepdims=True)
        y_ref[...] = out * lax.rsqrt(ms + EPS) * wn_ref[...]


def _peer(h2, u, vt_bf, g, x1, wn, tt):
    T, D = x1.shape
    nb, _, eb = vt_bf.shape
    sub = g.shape[2]
    return pl.pallas_call(
        _peer_kernel,
        grid=(T // tt, nb),
        in_specs=[
            pl.BlockSpec((tt, D), lambda i, e: (i, 0), pipeline_mode=pl.Buffered(1)),
            pl.BlockSpec((eb, D), lambda i, e: (e, 0)),
            pl.BlockSpec((None, D, eb), lambda i, e: (e, 0, 0)),
            pl.BlockSpec((tt // sub, eb, sub), lambda i, e: (i, e, 0)),
            pl.BlockSpec((tt, D), lambda i, e: (i, 0)),
            pl.BlockSpec((1, D), lambda i, e: (0, 0)),
        ],
        out_specs=pl.BlockSpec((tt, D), lambda i, e: (i, 0), pipeline_mode=pl.Buffered(1)),
        out_shape=jax.ShapeDtypeStruct((T, D), F32),
        scratch_shapes=[
            pltpu.VMEM((D, tt), F32),
            pltpu.VMEM((eb, tt), BF16),
            pltpu.VMEM((D, tt), BF16),
        ],
        compiler_params=_cparams(("parallel", "arbitrary")),
        name="peer_dense",
    )(h2, u, vt_bf, g, x1, wn)


def _block_transpose_kernel(v_ref, o_ref):
    o_ref[...] = v_ref[...].T.astype(o_ref.dtype)


def _value_blocks(v, eb):
    E, D = v.shape
    return pl.pallas_call(
        _block_transpose_kernel,
        grid=(E // eb,),
        in_specs=[pl.BlockSpec((eb, D), lambda i: (i, 0))],
        out_specs=pl.BlockSpec((None, D, eb), lambda i: (i, 0, 0)),
        out_shape=jax.ShapeDtypeStruct((E // eb, D, eb), BF16),
        compiler_params=_cparams(("parallel",)),
        name="peer_value_blocks",
    )(v)


def _token_tile(T, pref):
    t = min(pref, T)
    while T % t:
        t //= 2
    return t


def _ffn(mix, x2d, wo, wnf, wq, k1, k2, u, vt_bf, wn_final):
    T = x2d.shape[0]
    x1, h2, s1, s2 = _out_query(mix, x2d, wo, wnf, wq, k1, k2, _token_tile(T, QUERY_TOKEN_TILE))
    g = _route_weights(s1, s2, _token_tile(T, ROUTE_TOKEN_TILE))
    return _peer(h2, u, vt_bf, g, x1, wn_final, _token_tile(T, PEER_TOKEN_TILE))


def kernel(x_prompt, x_sample, state_ret, state_gla, w_norm_mix, w_in, w_gla_a2, b_gla_a, gn_ret_w, gn_ret_b, gn_gla_w, w_out, w_norm_ffn, w_pq, sub_keys1, sub_keys2, u_tab, v_tab, w_norm_final):
    Bp, Lp, D = x_prompt.shape
    Bs, Ls, _ = x_sample.shape
    depth = w_in.shape[0]
    assert depth == 1 and D == D_MODEL
    l = 0
    xp = x_prompt.reshape(Bp * Lp, D)
    xs = x_sample.reshape(Bs * Ls, D)

    w_main = w_in[l].astype(BF16)
    w_gate = jnp.pad(w_in[l, :, D_IN_MAIN:], ((0, 0), (0, LANES - GLA_RANK))).astype(BF16)
    w2 = jnp.pad(w_gla_a2[l], ((0, LANES - GLA_RANK), (0, 0))).astype(BF16)
    b2 = b_gla_a[l].reshape(1, -1)
    gnrw = gn_ret_w[l].reshape(1, -1)
    gnrb = gn_ret_b[l].reshape(1, -1)
    gngw = gn_gla_w[l].reshape(1, -1)
    wnm = w_norm_mix[l].reshape(1, D)
    wnf = w_norm_ffn[l].reshape(1, D)
    wn_final = w_norm_final.reshape(1, D)
    wo = w_out[l].astype(BF16)
    wq = w_pq[l].astype(BF16)
    k1 = sub_keys1[l].astype(BF16)
    k2 = sub_keys2[l].astype(BF16)
    u = u_tab[l]
    vt_bf = _value_blocks(v_tab[l], PEER_EXPERT_BLOCK)

    cos_p, sin_p = _rope_tables(jnp.arange(Lp, dtype=F32))
    pos_s = PAST_LEN + jnp.arange(Ls, dtype=F32)
    cos_s, sin_s = _rope_tables(jnp.tile(pos_s, SAMPLE_GROUP))
    sums_p, pm_p = _gla_consts(PROMPT_CHUNK, PROMPT_CHUNK)
    sums_s, pm_s = _gla_consts(SAMPLE_GROUP * Ls, Ls)
    consts_p = (cos_p, sin_p) + _ret_consts(PROMPT_CHUNK, PROMPT_CHUNK) + (
        w2, b2, jnp.asarray(sums_p, BF16), jnp.asarray(pm_p), gnrw, gnrb, gngw)
    consts_s = (cos_s, sin_s) + _ret_consts(SAMPLE_GROUP * Ls, Ls) + (
        w2, b2, jnp.asarray(sums_s, BF16), jnp.asarray(pm_s), gnrw, gnrb, gngw)

    proj_p, a_p = _norm_matmul(xp, wnm, w_main, w_gate, _token_tile(Bp * Lp, IN_PROJ_TOKEN_TILE), IN_PROJ_COL_TILE)
    proj_s, a_s = _norm_matmul(xs, wnm, w_main, w_gate, _token_tile(Bs * Ls, IN_PROJ_TOKEN_TILE), IN_PROJ_COL_TILE)
    mix_p, rp, gp = _mixer_prompt(proj_p, a_p, Bp, Lp, consts_p)
    mix_s, rs, gs = _mixer_sample(proj_s, a_s, state_ret, state_gla, Bs, Ls, consts_s)

    yp = _ffn(mix_p, xp, wo, wnf, wq, k1, k2, u, vt_bf, wn_final)
    ys = _ffn(mix_s, xs, wo, wnf, wq, k1, k2, u, vt_bf, wn_final)
    return (yp.reshape(Bp, Lp, D), ys.reshape(Bs, Ls, D), rp[None], gp[None], rs, gs)
```

```python
import functools
import math

import numpy as np
import jax
import jax.numpy as jnp
from jax import lax
from jax.experimental import pallas as pl
from jax.experimental.pallas import tpu as pltpu

F32 = jnp.float32
BF16 = jnp.bfloat16

D_MODEL = 2048
H_RET, DK_RET, DV_RET = 4, 128, 256
H_GLA, DK_GLA, DV_GLA = 4, 128, 256
GLA_RANK = 16
GLA_TEMP = 16.0
ROPE_BASE = 10000.0
PAST_LEN = 16384
PEER_HEADS, PEER_NKEYS, PEER_DQ, PEER_TOPK = 8, 128, 256, 16
N_EXPERTS = PEER_NKEYS * PEER_NKEYS
EPS = 1e-6

LANES = 128
D_MIX = H_RET * DV_RET + H_GLA * DV_GLA
D_IN_MAIN = 2 * H_RET * DK_RET + 2 * H_RET * DV_RET + 2 * H_GLA * DK_GLA + 2 * H_GLA * DV_GLA
OFF_QR = 0
OFF_KR = OFF_QR + H_RET * DK_RET
OFF_VR = OFF_KR + H_RET * DK_RET
OFF_GR = OFF_VR + H_RET * DV_RET
OFF_QG = OFF_GR + H_RET * DV_RET
OFF_KG = OFF_QG + H_GLA * DK_GLA
OFF_VG = OFF_KG + H_GLA * DK_GLA
OFF_GG = OFF_VG + H_GLA * DV_GLA

IN_PROJ_TOKEN_TILE = 1024
IN_PROJ_COL_TILE = 1536
QUERY_TOKEN_TILE = 512
ROUTE_TOKEN_TILE = 256
PEER_TOKEN_TILE = 512
PEER_EXPERT_BLOCK = 1024
PROMPT_CHUNK = 128
SAMPLE_GROUP = 8
VMEM_LIMIT = 56 * 1024 * 1024


def _cparams(sem):
    return pltpu.CompilerParams(dimension_semantics=sem, vmem_limit_bytes=VMEM_LIMIT)


def _norm_matmul_kernel(x_ref, wn_ref, w_ref, wa_ref, o_ref, oa_ref, h_ref):
    @pl.when(pl.program_id(1) == 0)
    def _():
        x = x_ref[...]
        ms = jnp.mean(x * x, axis=-1, keepdims=True)
        h = (x * lax.rsqrt(ms + EPS) * wn_ref[...]).astype(BF16)
        h_ref[...] = h
        oa_ref[...] = jnp.dot(h, wa_ref[...], preferred_element_type=F32)

    o_ref[...] = jnp.dot(h_ref[...], w_ref[...], preferred_element_type=F32)


def _norm_matmul(x, wn, w, wa, tm, tn):
    T, D = x.shape
    N = D_IN_MAIN
    Na = wa.shape[1]
    return pl.pallas_call(
        _norm_matmul_kernel,
        grid=(T // tm, N // tn),
        in_specs=[
            pl.BlockSpec((tm, D), lambda i, j: (i, 0)),
            pl.BlockSpec((1, D), lambda i, j: (0, 0)),
            pl.BlockSpec((D, tn), lambda i, j: (0, j)),
            pl.BlockSpec((D, Na), lambda i, j: (0, 0)),
        ],
        out_specs=[
            pl.BlockSpec((tm, tn), lambda i, j: (i, j)),
            pl.BlockSpec((tm, Na), lambda i, j: (i, 0)),
        ],
        out_shape=[jax.ShapeDtypeStruct((T, N), F32), jax.ShapeDtypeStruct((T, Na), F32)],
        scratch_shapes=[pltpu.VMEM((tm, D), BF16)],
        compiler_params=_cparams(("parallel", "arbitrary")),
        name="norm_in_proj",
    )(x, wn, w, wa)


def _gla_consts(rows, blk):
    idx = np.arange(rows)
    seq = idx // blk
    t = idx[None, :]
    i = idx[:, None]
    same = seq[:, None] == seq[None, :]
    mats = [same & (t <= i), same & (t > i)]
    pmasks = []
    s = blk
    while s >= 2:
        bstart = (idx // s) * s
        mid = bstart + s // 2
        upper = idx >= mid
        mats.append((upper[:, None] & (t >= mid[:, None]) & (t <= i))
                    | ((~upper)[:, None] & (t > i) & (t <= mid[:, None] - 1)))
        pmasks.append((bstart[:, None] == bstart[None, :]) & upper[:, None] & (~upper)[None, :])
        s //= 2
    pmasks.append(np.eye(rows, dtype=bool))
    return (np.concatenate(mats, 0).astype(np.float32), np.stack(pmasks).astype(np.float32))


def _ret_consts(rows, blk):
    log_g = jnp.log1p(-jnp.exp2(-5.0 - jnp.arange(H_RET, dtype=F32)))
    idx = np.arange(rows)
    p = jnp.asarray(idx % blk, F32)
    same = jnp.asarray((idx[:, None] // blk) == (idx[None, :] // blk))
    dist = p[:, None] - p[None, :]
    causal = same & (dist >= 0)
    dmask = jnp.where(causal[None], jnp.exp(jnp.maximum(dist, 0.0)[None] * log_g[:, None, None]), 0.0)
    qdec = jnp.exp((p + 1.0)[None, :] * log_g[:, None])
    kdec = jnp.exp((blk - 1.0 - p)[None, :] * log_g[:, None])
    cdec = jnp.exp(blk * log_g)
    qdec_full = jnp.broadcast_to(qdec[:, :, None], (H_RET, rows, DV_RET))
    kdec_full = jnp.broadcast_to(kdec[:, :, None], (H_RET, rows, DK_RET))
    cdec_full = jnp.broadcast_to(cdec[:, None, None], (H_RET, DK_RET, DV_RET))
    return dmask.astype(F32), qdec_full.astype(F32), kdec_full.astype(F32), cdec_full.astype(F32)


def _rope_tables(pos):
    d = DK_RET
    freqs = ROPE_BASE ** (-jnp.arange(0, d, 2, dtype=F32) / d)
    ang = pos[:, None] * freqs[None, :]
    cos = jnp.cos(ang)
    sin = jnp.sin(ang)
    return jnp.concatenate([cos, cos], axis=-1), jnp.concatenate([-sin, sin], axis=-1)


def _dot_nt(a, b):
    return lax.dot_general(a, b, (((1,), (1,)), ((), ())), preferred_element_type=F32)


def _rotary(x, cosf, sinf):
    return x * cosf + pltpu.roll(x, DK_RET // 2, axis=1) * sinf


def _sigmoid(x):
    return 1.0 / (1.0 + jnp.exp(-x))


def _ret_heads(p_ref, cosf, sinf, dmask_ref, kdec_ref):
    heads = range(H_RET)
    q = [_rotary(p_ref[:, OFF_QR + h * DK_RET:OFF_QR + (h + 1) * DK_RET], cosf, sinf) for h in heads]
    k = [_rotary(p_ref[:, OFF_KR + h * DK_RET:OFF_KR + (h + 1) * DK_RET], cosf, sinf) * (DK_RET ** -0.5)
         for h in heads]
    v = [p_ref[:, OFF_VR + h * DV_RET:OFF_VR + (h + 1) * DV_RET].astype(BF16) for h in heads]
    qb = [q[h].astype(BF16) for h in heads]
    scores = [_dot_nt(qb[h], k[h].astype(BF16)) * dmask_ref[h] for h in heads]
    inner = [jnp.dot(scores[h].astype(BF16), v[h], preferred_element_type=F32) for h in heads]
    kt = [(k[h] * kdec_ref[h]).T for h in heads]
    return qb, inner, kt, v


def _ret_finish(o, p_ref, h, gnw_ref, gnb_ref):
    mu = jnp.mean(o, axis=-1, keepdims=True)
    d = o - mu
    var = jnp.mean(d * d, axis=-1, keepdims=True)
    sl = slice(h * DV_RET, (h + 1) * DV_RET)
    y = d * lax.rsqrt(var + EPS) * gnw_ref[:, sl] + gnb_ref[:, sl]
    g = p_ref[:, OFF_GR + h * DV_RET:OFF_GR + (h + 1) * DV_RET]
    return y * (g * _sigmoid(g))


def _gla_log_decay(a_ref, w2_ref, b2_ref):
    a = a_ref[...].astype(BF16)
    z = jnp.dot(a, w2_ref[...], preferred_element_type=F32) + b2_ref[...]
    return (jnp.minimum(z, 0.0) - jnp.log1p(jnp.exp(-jnp.abs(z)))) * (1.0 / GLA_TEMP)


def _gla_heads(p_ref, la, sums_ref, pm_ref, rows, n_levels):
    heads = range(H_GLA)
    hilo = []
    for h in heads:
        la_h = la[:, h * DK_GLA:(h + 1) * DK_GLA]
        hi = la_h.astype(BF16)
        hilo += [hi, (la_h - hi.astype(F32)).astype(BF16)]
    e2 = jnp.dot(sums_ref[...], jnp.concatenate(hilo, axis=1), preferred_element_type=F32)
    ex = [e2[:, 2 * h * DK_GLA:(2 * h + 1) * DK_GLA] + e2[:, (2 * h + 1) * DK_GLA:(2 * h + 2) * DK_GLA]
          for h in heads]
    q = [p_ref[:, OFF_QG + h * DK_GLA:OFF_QG + (h + 1) * DK_GLA] * (DK_GLA ** -0.5) for h in heads]
    k = [p_ref[:, OFF_KG + h * DK_GLA:OFF_KG + (h + 1) * DK_GLA] for h in heads]
    v = [p_ref[:, OFF_VG + h * DV_GLA:OFF_VG + (h + 1) * DV_GLA].astype(BF16) for h in heads]
    amat = [_dot_nt(q[h].astype(BF16), k[h].astype(BF16)) * pm_ref[n_levels] for h in heads]
    for l in range(n_levels):
        for h in heads:
            dec = jnp.exp(ex[h][(2 + l) * rows:(3 + l) * rows])
            amat[h] = amat[h] + _dot_nt((q[h] * dec).astype(BF16), (k[h] * dec).astype(BF16)) * pm_ref[l]
    inner = [jnp.dot(amat[h].astype(BF16), v[h], preferred_element_type=F32) for h in heads]
    bcum = [ex[h][0:rows] for h in heads]
    qd = [(q[h] * jnp.exp(bcum[h])).astype(BF16) for h in heads]
    kt = [(k[h] * jnp.exp(ex[h][rows:2 * rows])).T for h in heads]
    return qd, inner, kt, v, bcum


def _gla_finish(o, p_ref, h, gw_ref):
    sl = slice(h * DV_GLA, (h + 1) * DV_GLA)
    y = o * lax.rsqrt(jnp.mean(o * o, axis=-1, keepdims=True) + EPS) * gw_ref[:, sl]
    g = p_ref[:, OFF_GG + h * DV_GLA:OFF_GG + (h + 1) * DV_GLA]
    return y * (g * _sigmoid(g))


def _rows_to_cols(row, n):
    return jnp.broadcast_to(row, (n, n)).T


def _mixer_prompt_kernel(p_ref, a_ref, cos_ref, sin_ref, dmask_ref, qdec_ref, kdec_ref, cdec_ref,
                         w2_ref, b2_ref, sums_ref, pm_ref, gnrw_ref, gnrb_ref, gngw_ref,
                         mix_ref, sret_ref, sgla_ref, *, rows, n_levels):
    @pl.when(pl.program_id(1) == 0)
    def _():
        sret_ref[...] = jnp.zeros_like(sret_ref)
        sgla_ref[...] = jnp.zeros_like(sgla_ref)

    cosf = cos_ref[...]
    sinf = sin_ref[...]
    rh = range(H_RET)
    qb, inner, kt, v = _ret_heads(p_ref, cosf, sinf, dmask_ref, kdec_ref)
    s_old = [sret_ref[0, h] for h in rh]
    cross = [jnp.dot(qb[h], s_old[h].astype(BF16), preferred_element_type=F32) * qdec_ref[h] for h in rh]
    upd = [jnp.dot(kt[h].astype(BF16), v[h], preferred_element_type=F32) for h in rh]
    for h in rh:
        sret_ref[0, h] = s_old[h] * cdec_ref[h] + upd[h]
    for h in rh:
        y = _ret_finish(inner[h] + cross[h], p_ref, h, gnrw_ref, gnrb_ref)
        mix_ref[:, h * DV_RET:(h + 1) * DV_RET] = y.astype(mix_ref.dtype)

    la = _gla_log_decay(a_ref, w2_ref, b2_ref)
    gh = range(H_GLA)
    qd, inner, kt, v, bcum = _gla_heads(p_ref, la, sums_ref, pm_ref, rows, n_levels)
    g_old = [sgla_ref[0, h] for h in gh]
    cross = [jnp.dot(qd[h], g_old[h].astype(BF16), preferred_element_type=F32) for h in gh]
    upd = [jnp.dot(kt[h].astype(BF16), v[h], preferred_element_type=F32) for h in gh]
    for h in gh:
        dec = jnp.exp(_rows_to_cols(bcum[h][rows - 1:rows, :], DK_GLA))
        sgla_ref[0, h] = g_old[h] * jnp.concatenate([dec, dec], axis=1) + upd[h]
    for h in gh:
        y = _gla_finish(inner[h] + cross[h], p_ref, h, gngw_ref)
        off = H_RET * DV_RET + h * DV_GLA
        mix_ref[:, off:off + DV_GLA] = y.astype(mix_ref.dtype)


def _mixer_prompt(proj, a_lr, B, L, consts):
    rows = PROMPT_CHUNK
    nc = L // rows
    cosf, sinf, dmask, qdec, kdec, cdec, w2, b2, sums, pm, gnrw, gnrb, gngw = consts
    n_levels = pm.shape[0] - 1
    const2 = lambda b, c: (0, 0)
    const3 = lambda b, c: (0, 0, 0)
    kern = functools.partial(_mixer_prompt_kernel, rows=rows, n_levels=n_levels)
    return pl.pallas_call(
        kern,
        grid=(B, nc),
        in_specs=[
            pl.BlockSpec((rows, D_IN_MAIN), lambda b, c: (b * nc + c, 0)),
            pl.BlockSpec((rows, LANES), lambda b, c: (b * nc + c, 0)),
            pl.BlockSpec((rows, DK_RET), lambda b, c: (c, 0)),
            pl.BlockSpec((rows, DK_RET), lambda b, c: (c, 0)),
            pl.BlockSpec(dmask.shape, const3),
            pl.BlockSpec(qdec.shape, const3),
            pl.BlockSpec(kdec.shape, const3),
            pl.BlockSpec(cdec.shape, const3),
            pl.BlockSpec(w2.shape, const2),
            pl.BlockSpec(b2.shape, const2),
            pl.BlockSpec(sums.shape, const2),
            pl.BlockSpec(pm.shape, const3),
            pl.BlockSpec(gnrw.shape, const2),
            pl.BlockSpec(gnrb.shape, const2),
            pl.BlockSpec(gngw.shape, const2),
        ],
        out_specs=[
            pl.BlockSpec((rows, D_MIX), lambda b, c: (b * nc + c, 0)),
            pl.BlockSpec((1, H_RET, DK_RET, DV_RET), lambda b, c: (b, 0, 0, 0)),
            pl.BlockSpec((1, H_GLA, DK_GLA, DV_GLA), lambda b, c: (b, 0, 0, 0)),
        ],
        out_shape=[
            jax.ShapeDtypeStruct((B * L, D_MIX), BF16),
            jax.ShapeDtypeStruct((B, H_RET, DK_RET, DV_RET), F32),
            jax.ShapeDtypeStruct((B, H_GLA, DK_GLA, DV_GLA), F32),
        ],
        compiler_params=_cparams(("parallel", "arbitrary")),
        name="mixer_prompt",
    )(proj, a_lr, cosf, sinf, dmask, qdec, kdec, cdec, w2, b2, sums, pm, gnrw, gnrb, gngw)


def _mixer_sample_kernel(p_ref, a_ref, cos_ref, sin_ref, dmask_ref, qdec_ref, kdec_ref, cdec_ref,
                         w2_ref, b2_ref, sums_ref, pm_ref, gnrw_ref, gnrb_ref, gngw_ref,
                         sret0_ref, sgla0_ref,
                         mix_ref, sret_ref, sgla_ref, *, rows, seq_len, n_levels):
    n_seq = rows // seq_len
    row_seq = lax.broadcasted_iota(jnp.int32, (rows, 1), 0) // seq_len
    col_seq = lax.broadcasted_iota(jnp.int32, (1, rows), 1) // seq_len
    cosf = cos_ref[...]
    sinf = sin_ref[...]
    qb, inner, kt, v = _ret_heads(p_ref, cosf, sinf, dmask_ref, kdec_ref)
    for h in range(H_RET):
        cross = jnp.zeros((rows, DV_RET), F32)
        for s in range(n_seq):
            s_old = sret0_ref[s, h]
            c_s = jnp.dot(qb[h], s_old.astype(BF16), preferred_element_type=F32)
            cross = cross + jnp.where(row_seq == s, c_s, 0.0)
            kt_s = jnp.where(col_seq == s, kt[h], 0.0).astype(BF16)
            sret_ref[s, h] = s_old * cdec_ref[h] + jnp.dot(kt_s, v[h], preferred_element_type=F32)
        y = _ret_finish(inner[h] + cross * qdec_ref[h], p_ref, h, gnrw_ref, gnrb_ref)
        mix_ref[:, h * DV_RET:(h + 1) * DV_RET] = y.astype(mix_ref.dtype)

    la = _gla_log_decay(a_ref, w2_ref, b2_ref)
    qd, inner, kt, v, bcum = _gla_heads(p_ref, la, sums_ref, pm_ref, rows, n_levels)
    for h in range(H_GLA):
        cross = jnp.zeros((rows, DV_GLA), F32)
        for s in range(n_seq):
            s_old = sgla0_ref[s, h]
            c_s = jnp.dot(qd[h], s_old.astype(BF16), preferred_element_type=F32)
            cross = cross + jnp.where(row_seq == s, c_s, 0.0)
            kt_s = jnp.where(col_seq == s, kt[h], 0.0).astype(BF16)
            last = (s + 1) * seq_len - 1
            dec = jnp.exp(_rows_to_cols(bcum[h][last:last + 1, :], DK_GLA))
            dec = jnp.concatenate([dec, dec], axis=1)
            sgla_ref[s, h] = s_old * dec + jnp.dot(kt_s, v[h], preferred_element_type=F32)
        y = _gla_finish(inner[h] + cross, p_ref, h, gngw_ref)
        off = H_RET * DV_RET + h * DV_GLA
        mix_ref[:, off:off + DV_GLA] = y.astype(mix_ref.dtype)


def _mixer_sample(proj, a_lr, sret0, sgla0, Bs, Ls, consts):
    n_seq = SAMPLE_GROUP
    rows = n_seq * Ls
    cosf, sinf, dmask, qdec, kdec, cdec, w2, b2, sums, pm, gnrw, gnrb, gngw = consts
    n_levels = pm.shape[0] - 1
    const2 = lambda g: (0, 0)
    const3 = lambda g: (0, 0, 0)
    kern = functools.partial(_mixer_sample_kernel, rows=rows, seq_len=Ls, n_levels=n_levels)
    st_spec = pl.BlockSpec((None, n_seq, H_RET, DK_RET, DV_RET), lambda g: (0, g, 0, 0, 0))
    return pl.pallas_call(
        kern,
        grid=(Bs // n_seq,),
        in_specs=[
            pl.BlockSpec((rows, D_IN_MAIN), lambda g: (g, 0)),
            pl.BlockSpec((rows, LANES), lambda g: (g, 0)),
            pl.BlockSpec(cosf.shape, const2),
            pl.BlockSpec(sinf.shape, const2),
            pl.BlockSpec(dmask.shape, const3),
            pl.BlockSpec(qdec.shape, const3),
            pl.BlockSpec(kdec.shape, const3),
            pl.BlockSpec(cdec.shape, const3),
            pl.BlockSpec(w2.shape, const2),
            pl.BlockSpec(b2.shape, const2),
            pl.BlockSpec(sums.shape, const2),
            pl.BlockSpec(pm.shape, const3),
            pl.BlockSpec(gnrw.shape, const2),
            pl.BlockSpec(gnrb.shape, const2),
            pl.BlockSpec(gngw.shape, const2),
            st_spec,
            st_spec,
        ],
        out_specs=[
            pl.BlockSpec((rows, D_MIX), lambda g: (g, 0)),
            st_spec,
            st_spec,
        ],
        out_shape=[
            jax.ShapeDtypeStruct((Bs * Ls, D_MIX), BF16),
            jax.ShapeDtypeStruct((1, Bs, H_RET, DK_RET, DV_RET), F32),
            jax.ShapeDtypeStruct((1, Bs, H_GLA, DK_GLA, DV_GLA), F32),
        ],
        compiler_params=_cparams(("parallel",)),
        name="mixer_sample",
    )(proj, a_lr, cosf, sinf, dmask, qdec, kdec, cdec, w2, b2, sums, pm, gnrw, gnrb, gngw, sret0, sgla0)


def _out_query_kernel(mix_ref, x_ref, wo_ref, wn_ref, wq_ref, k1_ref, k2_ref,
                      x1_ref, h2_ref, s1_ref, s2_ref):
    x1 = x_ref[...] + jnp.dot(mix_ref[...], wo_ref[...], preferred_element_type=F32)
    x1_ref[...] = x1
    ms = jnp.mean(x1 * x1, axis=-1, keepdims=True)
    h2 = (x1 * lax.rsqrt(ms + EPS) * wn_ref[...]).astype(BF16)
    h2_ref[...] = h2
    q = jnp.dot(h2, wq_ref[...], preferred_element_type=F32).astype(BF16)
    half = PEER_DQ // 2
    for h in range(PEER_HEADS):
        q1 = q[:, h * PEER_DQ:h * PEER_DQ + half]
        q2 = q[:, h * PEER_DQ + half:(h + 1) * PEER_DQ]
        s1_ref[h] = _dot_nt(k1_ref[h], q1)
        s2_ref[h] = _dot_nt(k2_ref[h], q2)


def _out_query(mix, x, wo, wn, wq, k1, k2, tm):
    T, D = x.shape
    resident = dict(pipeline_mode=pl.Buffered(1))
    return pl.pallas_call(
        _out_query_kernel,
        grid=(T // tm,),
        in_specs=[
            pl.BlockSpec((tm, D_MIX), lambda i: (i, 0)),
            pl.BlockSpec((tm, D), lambda i: (i, 0)),
            pl.BlockSpec(wo.shape, lambda i: (0, 0), **resident),
            pl.BlockSpec((1, D), lambda i: (0, 0)),
            pl.BlockSpec(wq.shape, lambda i: (0, 0), **resident),
            pl.BlockSpec(k1.shape, lambda i: (0, 0, 0)),
            pl.BlockSpec(k2.shape, lambda i: (0, 0, 0)),
        ],
        out_specs=[
            pl.BlockSpec((tm, D), lambda i: (i, 0)),
            pl.BlockSpec((tm, D), lambda i: (i, 0)),
            pl.BlockSpec((PEER_HEADS, PEER_NKEYS, tm), lambda i: (0, 0, i)),
            pl.BlockSpec((PEER_HEADS, PEER_NKEYS, tm), lambda i: (0, 0, i)),
        ],
        out_shape=[
            jax.ShapeDtypeStruct((T, D), F32),
            jax.ShapeDtypeStruct((T, D), BF16),
            jax.ShapeDtypeStruct((PEER_HEADS, PEER_NKEYS, T), F32),
            jax.ShapeDtypeStruct((PEER_HEADS, PEER_NKEYS, T), F32),
        ],
        compiler_params=_cparams(("parallel",)),
        name="out_proj_peer_query",
    )(mix, x, wo, wn, wq, k1, k2)


SUBLANES = 8


def _compare_exchange(a, i, l, descending=True):
    hi, lo = jnp.maximum(a[i], a[l]), jnp.minimum(a[i], a[l])
    a[i], a[l] = (hi, lo) if descending else (lo, hi)


def _bitonic_merge(a):
    n = len(a)
    j = n // 2
    while j >= 1:
        for i in range(n):
            if i ^ j > i:
                _compare_exchange(a, i, i ^ j)
        j //= 2


def _merge_sublane_lists(a):
    n = len(a)
    shift = SUBLANES // 2
    while shift >= 1:
        other = [pltpu.roll(x, shift, axis=0) for x in a]
        a = [jnp.maximum(a[i], other[n - 1 - i]) for i in range(n)]
        _bitonic_merge(a)
        shift //= 2
    return a


def _sorted_top(s):
    n = s.shape[0] // SUBLANES
    a = [s[SUBLANES * j:SUBLANES * (j + 1), :] for j in range(n)]
    k = 2
    while k <= n:
        j = k // 2
        while j >= 1:
            for i in range(n):
                if i ^ j > i:
                    _compare_exchange(a, i, i ^ j, descending=(i & k) == 0)
            j //= 2
        k *= 2
    return [x[0:1, :] for x in _merge_sublane_lists(a)]


def _top_pair_sums(v1, v2, stack_ref):
    k, half = PEER_TOPK, PEER_TOPK // 2
    for i in range(k):
        stack_ref[0, i:i + 1, :] = v1[i]
        stack_ref[1, i:i + 1, :] = v2[i]
    v1_low = stack_ref[0, 0:half, :]
    v2_low = stack_ref[1, 0:half, :]
    first = [v1_low + v2[b] for b in range(k)]
    second = [v1[half + j] + v2_low for j in range(half)]
    t = [first[i] if i < k - half else jnp.maximum(first[i], second[k - 1 - i]) for i in range(k)]
    _bitonic_merge(t)
    return [x[0:1, :] for x in _merge_sublane_lists(t)]


def _selection_threshold(s1, v2, tau):
    k = len(v2)
    steps = [k >> (i + 1) for i in range(k.bit_length() - 1)]
    preds = []

    def candidate(step, i=0, base=0):
        if i == len(preds):
            return v2[base + step - 1]
        return jnp.where(preds[i], candidate(step, i + 1, base + steps[i]), candidate(step, i + 1, base))

    last = jnp.full(s1.shape, jnp.inf, F32)
    for step in steps + [None]:
        thr = v2[k - 1] if step is None else candidate(step)
        passed = s1 + thr >= tau
        last = jnp.where(passed, thr, last)
        preds.append(passed)
    return last


def _route_weights_kernel(s1_ref, s2_ref, g_ref, th_ref, e1_ref, e2_ref, stack_ref, *, lane_chunk):
    tt = s1_ref.shape[-1]
    for h in range(PEER_HEADS):
        for c0 in range(0, tt, lane_chunk):
            cs = slice(c0, c0 + lane_chunk)
            s1 = s1_ref[h, :, cs]
            s2 = s2_ref[h, :, cs]
            v1 = _sorted_top(s1)
            v2 = _sorted_top(s2)
            top = _top_pair_sums(v1, v2, stack_ref.at[:, :, cs])
            tau = top[-1]
            z = sum(jnp.exp(c - top[0]) for c in top)
            th_ref[h, :, cs] = _selection_threshold(s1, v2, tau)
            e1_ref[h, :, cs] = jnp.exp(s1 - v1[0]) * (0.5 / z)
            e2_ref[h, :, cs] = jnp.exp(s2 - v2[0])

    def row_block(i1, carry):
        th_rows = [th_ref[h, pl.ds(i1, 1), :] for h in range(PEER_HEADS)]
        e1_rows = [e1_ref[h, pl.ds(i1, 1), :] for h in range(PEER_HEADS)]
        row0 = pl.multiple_of(i1 * PEER_NKEYS, PEER_NKEYS)
        for c0 in range(0, tt, lane_chunk):
            cs = slice(c0, c0 + lane_chunk)
            g = None
            for h in range(PEER_HEADS):
                e2 = e2_ref[h, :, cs]
                term = jnp.where(s2_ref[h, :, cs] >= th_rows[h][:, cs], e2, jnp.zeros_like(e2)) * e1_rows[h][:, cs]
                g = term if g is None else g + term
            g_ref[pl.ds(row0, PEER_NKEYS), cs] = g.astype(g_ref.dtype)
        return carry

    lax.fori_loop(0, PEER_NKEYS, row_block, 0)


def _route_weights(s1, s2, tt):
    H, K, T = s1.shape
    blk = pl.BlockSpec((H, K, tt), lambda i: (0, 0, i))
    kern = functools.partial(_route_weights_kernel, lane_chunk=min(tt, LANES))
    return pl.pallas_call(
        kern,
        grid=(T // tt,),
        in_specs=[blk, blk],
        out_specs=pl.BlockSpec((None, N_EXPERTS, tt), lambda i: (i, 0, 0)),
        out_shape=jax.ShapeDtypeStruct((T // tt, N_EXPERTS, tt), BF16),
        scratch_shapes=[
            pltpu.VMEM((H, K, tt), F32),
            pltpu.VMEM((H, K, tt), F32),
            pltpu.VMEM((H, K, tt), F32),
            pltpu.VMEM((2, PEER_TOPK, tt), F32),
        ],
        compiler_params=_cparams(("parallel",)),
        name="peer_route_weights",
    )(s1, s2)


def _peer_kernel(h2_ref, u_ref, vt_ref, g_ref, x1_ref, wn_ref, y_ref, acc_ref, w_ref, h2s_ref):
    e = pl.program_id(1)

    @pl.when(e == 0)
    def _():
        acc_ref[...] = jnp.zeros_like(acc_ref)
        h2s_ref[...] = h2_ref[...].astype(F32).T.astype(BF16)

    n_sub, _, sub = g_ref.shape
    for t in range(n_sub):
        cols = slice(t * sub, (t + 1) * sub)
        s = jnp.dot(u_ref[...].astype(BF16), h2s_ref[:, cols], preferred_element_type=F32)
        act = s * (1.0 + lax.erf(s * (2.0 ** -0.5)))
        w_ref[:, cols] = g_ref[t] * act.astype(BF16)
    for m0 in range(0, acc_ref.shape[0], 512):
        acc_ref[m0:m0 + 512, :] += jnp.dot(vt_ref[m0:m0 + 512, :], w_ref[...], preferred_element_type=F32)

    @pl.when(e == pl.num_programs(1) - 1)
    def _():
        out = x1_ref[...] + acc_ref[...].T
        ms = jnp.mean(out * out, axis=-1, keepdims=True)
        y_ref[...] = out * lax.rsqrt(ms + EPS) * wn_ref[...]


def _peer(h2, u, vt_bf, g, x1, wn, tt):
    T, D = x1.shape
    nb, _, eb = vt_bf.shape
    sub = g.shape[2]
    return pl.pallas_call(
        _peer_kernel,
        grid=(T // tt, nb),
        in_specs=[
            pl.BlockSpec((tt, D), lambda i, e: (i, 0), pipeline_mode=pl.Buffered(1)),
            pl.BlockSpec((eb, D), lambda i, e: (e, 0)),
            pl.BlockSpec((None, D, eb), lambda i, e: (e, 0, 0)),
            pl.BlockSpec((tt // sub, eb, sub), lambda i, e: (i, e, 0)),
            pl.BlockSpec((tt, D), lambda i, e: (i, 0)),
            pl.BlockSpec((1, D), lambda i, e: (0, 0)),
        ],
        out_specs=pl.BlockSpec((tt, D), lambda i, e: (i, 0), pipeline_mode=pl.Buffered(1)),
        out_shape=jax.ShapeDtypeStruct((T, D), F32),
        scratch_shapes=[
            pltpu.VMEM((D, tt), F32),
            pltpu.VMEM((eb, tt), BF16),
            pltpu.VMEM((D, tt), BF16),
        ],
        compiler_params=_cparams(("parallel", "arbitrary")),
        name="peer_dense",
    )(h2, u, vt_bf, g, x1, wn)


def _block_transpose_kernel(v_ref, o_ref):
    o_ref[...] = v_ref[...].T.astype(o_ref.dtype)


def _value_blocks(v, eb):
    E, D = v.shape
    return pl.pallas_call(
        _block_transpose_kernel,
        grid=(E // eb,),
        in_specs=[pl.BlockSpec((eb, D), lambda i: (i, 0))],
        out_specs=pl.BlockSpec((None, D, eb), lambda i: (i, 0, 0)),
        out_shape=jax.ShapeDtypeStruct((E // eb, D, eb), BF16),
        compiler_params=_cparams(("parallel",)),
        name="peer_value_blocks",
    )(v)


def _token_tile(T, pref):
    t = min(pref, T)
    while T % t:
        t //= 2
    return t


def _ffn(mix, x2d, wo, wnf, wq, k1, k2, u, vt_bf, wn_final):
    T = x2d.shape[0]
    x1, h2, s1, s2 = _out_query(mix, x2d, wo, wnf, wq, k1, k2, _token_tile(T, QUERY_TOKEN_TILE))
    g = _route_weights(s1, s2, _token_tile(T, ROUTE_TOKEN_TILE))
    return _peer(h2, u, vt_bf, g, x1, wn_final, _token_tile(T, PEER_TOKEN_TILE))


def kernel(x_prompt, x_sample, state_ret, state_gla, w_norm_mix, w_in, w_gla_a2, b_gla_a, gn_ret_w, gn_ret_b, gn_gla_w, w_out, w_norm_ffn, w_pq, sub_keys1, sub_keys2, u_tab, v_tab, w_norm_final):
    Bp, Lp, D = x_prompt.shape
    Bs, Ls, _ = x_sample.shape
    depth = w_in.shape[0]
    assert depth == 1 and D == D_MODEL
    l = 0
    xp = x_prompt.reshape(Bp * Lp, D)
    xs = x_sample.reshape(Bs * Ls, D)

    w_main = w_in[l].astype(BF16)
    w_gate = jnp.pad(w_in[l, :, D_IN_MAIN:], ((0, 0), (0, LANES - GLA_RANK))).astype(BF16)
    w2 = jnp.pad(w_gla_a2[l], ((0, LANES - GLA_RANK), (0, 0))).astype(BF16)
    b2 = b_gla_a[l].reshape(1, -1)
    gnrw = gn_ret_w[l].reshape(1, -1)
    gnrb = gn_ret_b[l].reshape(1, -1)
    gngw = gn_gla_w[l].reshape(1, -1)
    wnm = w_norm_mix[l].reshape(1, D)
    wnf = w_norm_ffn[l].reshape(1, D)
    wn_final = w_norm_final.reshape(1, D)
    wo = w_out[l].astype(BF16)
    wq = w_pq[l].astype(BF16)
    k1 = sub_keys1[l].astype(BF16)
    k2 = sub_keys2[l].astype(BF16)
    u = u_tab[l]
    vt_bf = _value_blocks(v_tab[l], PEER_EXPERT_BLOCK)

    cos_p, sin_p = _rope_tables(jnp.arange(Lp, dtype=F32))
    pos_s = PAST_LEN + jnp.arange(Ls, dtype=F32)
    cos_s, sin_s = _rope_tables(jnp.tile(pos_s, SAMPLE_GROUP))
    sums_p, pm_p = _gla_consts(PROMPT_CHUNK, PROMPT_CHUNK)
    sums_s, pm_s = _gla_consts(SAMPLE_GROUP * Ls, Ls)
    consts_p = (cos_p, sin_p) + _ret_consts(PROMPT_CHUNK, PROMPT_CHUNK) + (
        w2, b2, jnp.asarray(sums_p, BF16), jnp.asarray(pm_p), gnrw, gnrb, gngw)
    consts_s = (cos_s, sin_s) + _ret_consts(SAMPLE_GROUP * Ls, Ls) + (
        w2, b2, jnp.asarray(sums_s, BF16), jnp.asarray(pm_s), gnrw, gnrb, gngw)

    proj_p, a_p = _norm_matmul(xp, wnm, w_main, w_gate, _token_tile(Bp * Lp, IN_PROJ_TOKEN_TILE), IN_PROJ_COL_TILE)
    proj_s, a_s = _norm_matmul(xs, wnm, w_main, w_gate, _token_tile(Bs * Ls, IN_PROJ_TOKEN_TILE), IN_PROJ_COL_TILE)
    mix_p, rp, gp = _mixer_prompt(proj_p, a_p, Bp, Lp, consts_p)
    mix_s, rs, gs = _mixer_sample(proj_s, a_s, state_ret, state_gla, Bs, Ls, consts_s)

    yp = _ffn(mix_p, xp, wo, wnf, wq, k1, k2, u, vt_bf, wn_final)
    ys = _ffn(mix_s, xs, wo, wnf, wq, k1, k2, u, vt_bf, wn_final)
    return (yp.reshape(Bp, Lp, D), ys.reshape(Bs, Ls, D), rp[None], gp[None], rs, gs)
```
